```python
import math
import jax, jax.numpy as jnp
from jax import lax
import numpy as np

D_MODEL = 1024
BATCH = 32
SEQ = 256
DEPTH = 4
DEC_BATCH = 8
DEC_SEQ = 4096
PAST_LEN = 256

GRID_W = 64
N_HEADS = 8
QK_NOPE = 64
QK_ROPE = 32
V_DIM = 64
Q_RANK = 384
KV_RANK = 256
ROPE_BASE = 10000.0
Q_BLOCK = 128
HY_WIDTH = 512
HY_ORDER = 2
HY_BANDS = 16
HY_EMB = 1 + 2 * HY_BANDS
HY_HIDDEN = 64
SHORT_K = 3
P_HEADS = 8
N_KEYS = 128
N_EXPERTS = N_KEYS * N_KEYS
KEY_DIM = 256
KEY_HALF = KEY_DIM // 2
P_TOPK = 16
TOK_BLOCK = 128
IN_COLS = Q_RANK + KV_RANK + QK_ROPE + (HY_ORDER + 1) * HY_WIDTH + 2 * D_MODEL
EPS = 1e-6

kernel_name = 'hybrid_mla_hyena_peer_diffusion_step'


def _rmsnorm(x, g):
    xf = x.astype(jnp.float32)
    y = xf * lax.rsqrt(jnp.mean(xf * xf, axis=-1, keepdims=True) + EPS)
    return (y * g.astype(jnp.float32)).astype(x.dtype)


def _adaln(cvec, w, b):
    m = jax.nn.silu(cvec) @ w + b
    return jnp.split(m[..., None, :], 6, axis=-1)


def _axial_rope(n_tok):
    rows = n_tok // GRID_W
    row = jnp.repeat(jnp.arange(rows, dtype=jnp.float32), GRID_W)
    col = jnp.tile(jnp.arange(GRID_W, dtype=jnp.float32), rows)
    n_freq = QK_ROPE // 4
    inv = 1.0 / (ROPE_BASE ** (jnp.arange(n_freq, dtype=jnp.float32) / n_freq))
    ang = jnp.concatenate([row[:, None] * inv, col[:, None] * inv], axis=-1)
    return jnp.cos(ang), jnp.sin(ang)


def _apply_rope(x, cos, sin):
    half = QK_ROPE // 2
    x1 = x[..., :half].astype(jnp.float32)
    x2 = x[..., half:].astype(jnp.float32)
    return jnp.concatenate([x1 * cos - x2 * sin, x2 * cos + x1 * sin], axis=-1).astype(x.dtype)


def _mla_attend(qn, qr, kn, kr, v):
    B, Lq = qn.shape[0], qn.shape[1]
    nb = Lq // Q_BLOCK
    scale = 1.0 / math.sqrt(QK_NOPE + QK_ROPE)
    qn_b = qn.reshape(B, nb, Q_BLOCK, N_HEADS, QK_NOPE).swapaxes(0, 1)
    qr_b = qr.reshape(B, nb, Q_BLOCK, N_HEADS, QK_ROPE).swapaxes(0, 1)

    def block(args):
        qn_i, qr_i = args
        s = jnp.einsum('bqhd,bkhd->bhqk', qn_i, kn) + jnp.einsum('bqhr,bkr->bhqk', qr_i, kr)
        p = jax.nn.softmax(s.astype(jnp.float32) * scale, axis=-1).astype(v.dtype)
        return jnp.einsum('bhqk,bkhd->bqhd', p, v)

    o = lax.map(block, (qn_b, qr_b))
    return o.swapaxes(0, 1).reshape(B, Lq, N_HEADS * V_DIM)


def _hyena_filters(n_tok, w1, b1, w2, b2, w3, decay):
    f32 = jnp.float32
    t = jnp.arange(n_tok, dtype=f32) / n_tok
    bands = jnp.arange(1, HY_BANDS + 1, dtype=f32)
    ang = 2.0 * math.pi * t[:, None] * bands
    z = jnp.concatenate([t[:, None], jnp.sin(ang), jnp.cos(ang)], axis=-1)
    f = jnp.sin(z @ w1.astype(f32) + b1.astype(f32))
    f = jnp.sin(f @ w2.astype(f32) + b2.astype(f32))
    h = (f @ w3.astype(f32)).reshape(n_tok, HY_ORDER, 2, HY_WIDTH)
    window = jnp.exp(-decay.astype(f32)[None] * t[:, None, None, None])
    return h * window


def _fft_conv(u, h):
    n = u.shape[1]
    uf = jnp.fft.rfft(u, n=2 * n, axis=1)
    hf = jnp.fft.rfft(h, n=2 * n, axis=0)
    return jnp.fft.irfft(uf * hf[None], n=2 * n, axis=1)[:, :n]


def _bidir_long_conv(u, h_fwd, h_bwd, bias):
    fwd = _fft_conv(u, h_fwd)
    bwd = jnp.flip(_fft_conv(jnp.flip(u, axis=1), h_bwd), axis=1)
    return fwd + bwd + u * bias


def _short_conv(x, w, b):
    n = x.shape[1]
    pad = SHORT_K // 2
    xp = jnp.pad(x, ((0, 0), (pad, pad), (0, 0)))
    y = xp[:, 0:n] * w[0]
    for k in range(1, SHORT_K):
        y = y + xp[:, k:k + n] * w[k]
    return y + b


def _hyena(z, conv_w, conv_b, filt, bias):
    z = _short_conv(z, conv_w, conv_b).astype(jnp.float32)
    parts = jnp.split(z, HY_ORDER + 1, axis=-1)
    y = parts[0]
    for o in range(HY_ORDER):
        y = parts[o + 1] * _bidir_long_conv(y, filt[:, o, 0], filt[:, o, 1], bias[o].astype(jnp.float32))
    return y


def _peer(h, wq, sub_keys, u_tab, v_tab):
    B, L, D = h.shape
    T = B * L
    xt = h.reshape(T, D)
    q = (xt @ wq).reshape(T, P_HEADS, 2, KEY_HALF)
    s = jnp.einsum('thpk,hpnk->thpn', q, sub_keys).astype(jnp.float32)
    s1, i1 = lax.top_k(s[:, :, 0], P_TOPK)
    s2, i2 = lax.top_k(s[:, :, 1], P_TOPK)
    cand = (s1[..., :, None] + s2[..., None, :]).reshape(T, P_HEADS, P_TOPK * P_TOPK)
    cidx = (i1[..., :, None] * N_KEYS + i2[..., None, :]).reshape(T, P_HEADS, P_TOPK * P_TOPK)
    top, pos = lax.top_k(cand, P_TOPK)
    idx = jnp.take_along_axis(cidx, pos, axis=-1)
    g = jax.nn.softmax(top, axis=-1).astype(h.dtype)
    nb = T // TOK_BLOCK
    xb = xt.reshape(nb, TOK_BLOCK, D)
    ib = idx.reshape(nb, TOK_BLOCK, P_HEADS, P_TOPK)
    gb = g.reshape(nb, TOK_BLOCK, P_HEADS, P_TOPK)

    def block(args):
        x_i, i_i, g_i = args
        u = jnp.take(u_tab, i_i, axis=0)
        a = jax.nn.gelu(jnp.einsum('td,thkd->thk', x_i, u))
        return jnp.einsum('thk,thkd->td', a * g_i, jnp.take(v_tab, i_i, axis=0))

    out = lax.map(block, (xb, ib, gb))
    return out.reshape(B, L, D)


def _layer(x, mod, lp, rope, ctx):
    shift1, scale1, gate1, shift2, scale2, gate2 = mod
    B, L, _ = x.shape
    h = _rmsnorm(x, lp['norm_mix']) * (1 + scale1) + shift1
    proj = h @ lp['w_in']
    cuts = [Q_RANK, Q_RANK + KV_RANK, Q_RANK + KV_RANK + QK_ROPE,
            Q_RANK + KV_RANK + QK_ROPE + (HY_ORDER + 1) * HY_WIDTH]
    cq, ckv, kr, hy_in, gates = jnp.split(proj, cuts, axis=-1)
    ckv = _rmsnorm(ckv, lp['kv_norm'])
    q = (_rmsnorm(cq, lp['q_norm']) @ lp['w_uq']).reshape(B, L, N_HEADS, QK_NOPE + QK_ROPE)
    qn, qr = q[..., :QK_NOPE], q[..., QK_NOPE:]
    kr_keys = kr
    if rope is not None:
        cos, sin = rope
        qr = _apply_rope(qr, cos[:, None], sin[:, None])
        kr_keys = _apply_rope(kr, cos, sin)
    ckv_keys = ckv
    if ctx is not None:
        ctx_ckv, ctx_kr = ctx
        ckv_keys = jnp.concatenate([ctx_ckv, ckv], axis=1)
        kr_keys = jnp.concatenate([ctx_kr, kr_keys], axis=1)
    kv = (ckv_keys @ lp['w_ukv']).reshape(B, -1, N_HEADS, QK_NOPE + V_DIM)
    attn = _mla_attend(qn, qr, kv[..., :QK_NOPE], kr_keys, kv[..., QK_NOPE:])
    filt = _hyena_filters(L, lp['hy_w1'], lp['hy_b1'], lp['hy_w2'], lp['hy_b2'], lp['hy_w3'], lp['hy_decay'])
    hy = _hyena(hy_in, lp['hy_conv_w'], lp['hy_conv_b'], filt, lp['hy_bias']).astype(x.dtype)
    g_a, g_b = jnp.split(jax.nn.sigmoid(gates), 2, axis=-1)
    merged = g_a * (attn @ lp['w_oa']) + g_b * (hy @ lp['w_ob'])
    x = x + gate1 * (merged @ lp['w_out'])
    h2 = _rmsnorm(x, lp['norm_ffn']) * (1 + scale2) + shift2
    x = x + gate2 * _peer(h2, lp['peer_wq'], lp['peer_keys'], lp['peer_u'], lp['peer_v'])
    return x, ckv, kr


def setup_inputs(seed: int = 0) -> dict:
    key = jax.random.key(seed)
    ks = iter(jax.random.split(key, 40))

    def nrm(shape, s):
        return jax.random.normal(next(ks), shape, jnp.float32) * s

    D = D_MODEL
    return {
        'x_prompt': nrm((BATCH, SEQ, D), 1.0),
        'x_sample': nrm((DEC_BATCH, DEC_SEQ, D), 1.0),
        'cache_ckv': nrm((DEC_BATCH, DEPTH, PAST_LEN, KV_RANK), 1.0),
        'cache_krope': nrm((DEC_BATCH, DEPTH, PAST_LEN, QK_ROPE), 1.0),
        'c': nrm((DEC_BATCH, D), 1.0),
        'c_ctx': nrm((D,), 1.0),
        'w_ada': nrm((DEPTH, D, 6 * D), 0.5 * D ** -0.5),
        'b_ada': nrm((DEPTH, 6 * D), 0.01),
        'norm_mix': 1.0 + nrm((DEPTH, D), 0.02),
        'norm_ffn': 1.0 + nrm((DEPTH, D), 0.02),
        'w_in': nrm((DEPTH, D, IN_COLS), D ** -0.5),
        'q_norm': 1.0 + nrm((DEPTH, Q_RANK), 0.02),
        'kv_norm': 1.0 + nrm((DEPTH, KV_RANK), 0.02),
        'w_uq': nrm((DEPTH, Q_RANK, N_HEADS * (QK_NOPE + QK_ROPE)), Q_RANK ** -0.5),
        'w_ukv': nrm((DEPTH, KV_RANK, N_HEADS * (QK_NOPE + V_DIM)), KV_RANK ** -0.5),
        'w_oa': nrm((DEPTH, N_HEADS * V_DIM, D), (N_HEADS * V_DIM) ** -0.5),
        'w_ob': nrm((DEPTH, HY_WIDTH, D), HY_WIDTH ** -0.5),
        'w_out': nrm((DEPTH, D, D), D ** -0.5),
        'hy_conv_w': nrm((DEPTH, SHORT_K, (HY_ORDER + 1) * HY_WIDTH), SHORT_K ** -0.5),
        'hy_conv_b': nrm((DEPTH, (HY_ORDER + 1) * HY_WIDTH), 0.01),
        'hy_w1': nrm((DEPTH, HY_EMB, HY_HIDDEN), HY_EMB ** -0.5),
        'hy_b1': nrm((DEPTH, HY_HIDDEN), 0.1),
        'hy_w2': nrm((DEPTH, HY_HIDDEN, HY_HIDDEN), HY_HIDDEN ** -0.5),
        'hy_b2': nrm((DEPTH, HY_HIDDEN), 0.1),
        'hy_w3': nrm((DEPTH, HY_HIDDEN, HY_ORDER * 2 * HY_WIDTH), 0.1 * HY_HIDDEN ** -0.5),
        'hy_decay': jax.random.uniform(next(ks), (DEPTH, HY_ORDER, 2, HY_WIDTH), jnp.float32, 1.0, 10.0),
        'hy_bias': nrm((DEPTH, HY_ORDER, HY_WIDTH), 0.1),
        'peer_wq': nrm((DEPTH, D, P_HEADS * KEY_DIM), D ** -0.5),
        'peer_keys': nrm((DEPTH, P_HEADS, 2, N_KEYS, KEY_HALF), KEY_HALF ** -0.5),
        'peer_u': nrm((DEPTH, N_EXPERTS, D), D ** -0.5),
        'peer_v': nrm((DEPTH, N_EXPERTS, D), P_HEADS ** -0.5),
        'final_norm': 1.0 + nrm((D,), 0.02),
    }


def reference(x_prompt, x_sample, cache_ckv, cache_krope, c, c_ctx, w_ada, b_ada, norm_mix, norm_ffn,
              w_in, q_norm, kv_norm, w_uq, w_ukv, w_oa, w_ob, w_out, hy_conv_w, hy_conv_b,
              hy_w1, hy_b1, hy_w2, hy_b2, hy_w3, hy_decay, hy_bias,
              peer_wq, peer_keys, peer_u, peer_v, final_norm):
    rope = _axial_rope(x_sample.shape[1])
    xp, xs = x_prompt, x_sample
    new_ckv, new_kr = [], []
    for l in range(DEPTH):
        lp = {
            'norm_mix': norm_mix[l], 'norm_ffn': norm_ffn[l], 'w_in': w_in[l],
            'q_norm': q_norm[l], 'kv_norm': kv_norm[l], 'w_uq': w_uq[l], 'w_ukv': w_ukv[l],
            'w_oa': w_oa[l], 'w_ob': w_ob[l], 'w_out': w_out[l],
            'hy_conv_w': hy_conv_w[l], 'hy_conv_b': hy_conv_b[l],
            'hy_w1': hy_w1[l], 'hy_b1': hy_b1[l], 'hy_w2': hy_w2[l], 'hy_b2': hy_b2[l], 'hy_w3': hy_w3[l],
            'hy_decay': hy_decay[l], 'hy_bias': hy_bias[l],
            'peer_wq': peer_wq[l], 'peer_keys': peer_keys[l], 'peer_u': peer_u[l], 'peer_v': peer_v[l],
        }
        mod_ctx = _adaln(c_ctx, w_ada[l], b_ada[l])
        xp, ckv_p, kr_p = _layer(xp, mod_ctx, lp, None, None)
        new_ckv.append(ckv_p)
        new_kr.append(kr_p)
        mod_lat = _adaln(c, w_ada[l], b_ada[l])
        xs, _, _ = _layer(xs, mod_lat, lp, rope, (cache_ckv[:, l], cache_krope[:, l]))
    y_prompt = _rmsnorm(xp, final_norm)
    y_sample = _rmsnorm(xs, final_norm)
    state_ckv = jnp.stack(new_ckv, axis=1)
    state_krope = jnp.stack(new_kr, axis=1)
    return (y_prompt, y_sample, state_ckv, state_krope)
```

```python
import functools
import math

import numpy as np
import jax
import jax.numpy as jnp
from jax import lax
from jax.experimental import pallas as pl
from jax.experimental.pallas import tpu as pltpu

F32 = jnp.float32
BF16 = jnp.bfloat16

GRID_W = 64
N_HEADS = 8
QK_NOPE = 64
QK_ROPE = 32
V_DIM = 64
ROPE_BASE = 10000.0
HY_ORDER = 2
HY_BANDS = 16
SHORT_K = 3
P_HEADS = 8
N_KEYS = 128
P_TOPK = 16
EPS = 1e-6

LANES = 128
SUBLANES = 8
FFT_B = 64
DIRECT_DFT_MAX_L = 256
VMEM_LIMIT_BYTES = 48 * 1024 * 1024


def _cp(*sem):
    return pltpu.CompilerParams(dimension_semantics=sem, vmem_limit_bytes=VMEM_LIMIT_BYTES)


def _full(shape):
    n = len(shape)
    return pl.BlockSpec(shape, lambda *_: (0,) * n)


def _tile(n, pref, unit=LANES):
    if n <= pref:
        return n
    return max(t for t in range(unit, pref + 1, unit) if n % t == 0)


def _dot(a, b):
    return jnp.dot(a, b, preferred_element_type=F32)


def _dot_hi(a, b):
    return jnp.dot(a, b, precision=lax.Precision.HIGHEST, preferred_element_type=F32)


def _dot_nt(a, b):
    return lax.dot_general(a, b, (((1,), (1,)), ((), ())), preferred_element_type=F32)


def _rms(x, g):
    return x * lax.rsqrt(jnp.mean(x * x, axis=-1, keepdims=True) + EPS) * g


def _adaln_kernel(c_ref, w_ref, b_ref, o_ref):
    c = c_ref[...]
    s = (c * jax.nn.sigmoid(c)).astype(BF16)
    o_ref[...] = _dot(s, w_ref[...].astype(BF16)) + b_ref[...]


def _adaln(cc, w_ada, b_ada):
    depth, d, d6 = w_ada.shape
    rows = cc.shape[0]
    tn = d6 // 4
    return pl.pallas_call(
        _adaln_kernel,
        grid=(depth, d6 // tn),
        in_specs=[
            _full((rows, d)),
            pl.BlockSpec((None, d, tn), lambda l, j: (l, 0, j)),
            pl.BlockSpec((None, 1, tn), lambda l, j: (l, 0, j)),
        ],
        out_specs=pl.BlockSpec((None, rows, tn), lambda l, j: (l, 0, j)),
        out_shape=jax.ShapeDtypeStruct((depth, rows, d6), F32),
        compiler_params=_cp("parallel", "parallel"),
        name="adaln",
    )(cc, w_ada, b_ada.reshape(depth, 1, d6))


def _rope_slab(x, c, s1, s2):
    half = QK_ROPE // 2
    return x * c + pltpu.roll(x, LANES - half, 1) * s1 + pltpu.roll(x, half, 1) * s2


def _inproj_kernel(*refs, rope):
    if rope:
        (x_ref, mod_ref, gn_ref, wq_ref, wckv_ref, wkr_ref, why_ref, wg_ref, qn_ref, kvn_ref,
         wuq_ref, wukv_ref, c_ref, s1_ref, s2_ref,
         q_out, k_out, v_out, ckv_out, kr_out, hy_out, ga_out, gb_out) = refs
    else:
        (x_ref, mod_ref, gn_ref, wq_ref, wckv_ref, wkr_ref, why_ref, wg_ref, qn_ref, kvn_ref,
         wuq_ref, wukv_ref,
         q_out, k_out, v_out, ckv_out, kr_out, hy_out, ga_out, gb_out) = refs
    x = x_ref[...]
    mod = mod_ref[...]
    d = x.shape[1]
    hl = N_HEADS * LANES
    h = (_rms(x, gn_ref[...]) * (1.0 + mod[1:2]) + mod[0:1]).astype(BF16)
    q = _dot(_rms(_dot(h, wq_ref[...]), qn_ref[...]).astype(BF16), wuq_ref[...])
    ckv = _rms(_dot(h, wckv_ref[...]), kvn_ref[...])
    kr = _dot(h, wkr_ref[...])
    kv = _dot(ckv.astype(BF16), wukv_ref[...])
    if rope:
        c, s1, s2 = c_ref[...], s1_ref[...], s2_ref[...]
        kr_k = _rope_slab(kr, c, s1, s2)
        q = jnp.concatenate(
            [_rope_slab(q[:, i * LANES:(i + 1) * LANES], c, s1, s2) for i in range(N_HEADS)], axis=1)
    else:
        kr_k = kr
    q_out[...] = q.astype(BF16)
    k_out[...] = (kv[:, :hl] + jnp.concatenate([kr_k] * N_HEADS, axis=1)).astype(BF16)
    v_out[...] = kv[:, hl:].astype(BF16)
    ckv_out[...] = ckv
    kr_out[...] = kr
    hy_out[...] = _dot(h, why_ref[...])
    g = _dot(h, wg_ref[...])
    ga_out[...] = jax.nn.sigmoid(g[:, :d])
    gb_out[...] = jax.nn.sigmoid(g[:, d:])


def _inproj(x, mod, lw, rope_tabs):
    b, l, d = x.shape
    tm = min(l, 256)
    pb = 1 if mod.shape[0] > 1 else 0
    hl = N_HEADS * LANES
    kvr = lw["wckv"].shape[1]
    hyw = lw["why"].shape[1]
    rope = rope_tabs is not None
    tok = lambda bi, i: (bi, i, 0)
    w_names = ["gn", "wq", "wckv", "wkr", "why", "wg", "qn", "kvn", "wuq", "wukv"]
    ins = [x, mod] + [lw[n] for n in w_names]
    in_specs = [pl.BlockSpec((None, tm, d), tok), pl.BlockSpec((None, 6, d), lambda bi, i: (bi * pb, 0, 0))]
    in_specs += [_full(lw[n].shape) for n in w_names]
    if rope:
        ins += list(rope_tabs)
        in_specs += [pl.BlockSpec((tm, LANES), lambda bi, i: (i, 0))] * 3
    widths = [(hl, BF16), (hl, BF16), (hl, BF16), (kvr, F32), (LANES, F32), (hyw, F32), (d, F32), (d, F32)]
    return pl.pallas_call(
        functools.partial(_inproj_kernel, rope=rope),
        grid=(b, l // tm),
        in_specs=in_specs,
        out_specs=[pl.BlockSpec((None, tm, w), tok) for w, _ in widths],
        out_shape=[jax.ShapeDtypeStruct((b, l, w), dt) for w, dt in widths],
        compiler_params=_cp("parallel", "parallel"),
        name="inproj_rope" if rope else "inproj",
    )(*ins)


def _kvcache_kernel(ckv_ref, kr_ref, wukv_ref, e_ref, k_out, v_out):
    hl = N_HEADS * LANES
    kv = _dot(ckv_ref[...].astype(BF16), wukv_ref[...])
    k_out[...] = (kv[:, :hl] + _dot(kr_ref[...].astype(BF16), e_ref[...])).astype(BF16)
    v_out[...] = kv[:, hl:].astype(BF16)


def _kvcache(cache_ckv, cache_kr, layer, wukv, e_mat):
    b, _, p, kvr = cache_ckv.shape
    hl = N_HEADS * LANES
    return pl.pallas_call(
        _kvcache_kernel,
        grid=(b,),
        in_specs=[
            pl.BlockSpec((None, None, p, kvr), lambda bi: (bi, layer, 0, 0)),
            pl.BlockSpec((None, None, p, QK_ROPE), lambda bi: (bi, layer, 0, 0)),
            _full(wukv.shape),
            _full(e_mat.shape),
        ],
        out_specs=[pl.BlockSpec((None, p, hl), lambda bi: (bi, 0, 0))] * 2,
        out_shape=[jax.ShapeDtypeStruct((b, p, hl), BF16)] * 2,
        compiler_params=_cp("parallel"),
        name="kvcache",
    )(cache_ckv, cache_kr, wukv, e_mat)


def _attn_kernel(*refs, ctx, scale):
    if ctx:
        q_ref, k_ref, v_ref, kc_ref, vc_ref, o_ref = refs
    else:
        q_ref, k_ref, v_ref, o_ref = refs
    q = q_ref[...]
    s = _dot_nt(q, k_ref[...]) * scale
    m = jnp.max(s, axis=1, keepdims=True)
    if ctx:
        sc = _dot_nt(q, kc_ref[...]) * scale
        m = jnp.maximum(m, jnp.max(sc, axis=1, keepdims=True))
    p = jnp.exp(s - m)
    den = jnp.sum(p, axis=1, keepdims=True)
    o = _dot(p.astype(BF16), v_ref[...])
    if ctx:
        pc = jnp.exp(sc - m)
        den = den + jnp.sum(pc, axis=1, keepdims=True)
        o = o + _dot(pc.astype(BF16), vc_ref[...])
    o_ref[...] = (o / den).astype(BF16)


def _attention(q, k, v, kc=None, vc=None):
    b, l, hl = q.shape
    tq = min(l, 256)
    ctx = kc is not None
    ins = [q, k, v]
    in_specs = [
        pl.BlockSpec((None, tq, LANES), lambda bi, h, i: (bi, i, h)),
        pl.BlockSpec((None, l, LANES), lambda bi, h, i: (bi, 0, h)),
        pl.BlockSpec((None, l, LANES), lambda bi, h, i: (bi, 0, h)),
    ]
    if ctx:
        p = kc.shape[1]
        ins += [kc, vc]
        in_specs += [pl.BlockSpec((None, p, LANES), lambda bi, h, i: (bi, 0, h))] * 2
    return pl.pallas_call(
        functools.partial(_attn_kernel, ctx=ctx, scale=1.0 / math.sqrt(QK_NOPE + QK_ROPE)),
        grid=(b, N_HEADS, l // tq),
        in_specs=in_specs,
        out_specs=pl.BlockSpec((None, tq, LANES), lambda bi, h, i: (bi, i, h)),
        out_shape=jax.ShapeDtypeStruct((b, l, hl), BF16),
        compiler_params=_cp("parallel", "parallel", "parallel"),
        name="attn_ctx" if ctx else "attn",
    )(*ins)


def _shortconv_kernel(x_ref, w_ref, b_ref, o_ref):
    x = x_ref[...]
    n = x.shape[0]
    row = lax.broadcasted_iota(jnp.int32, x.shape, 0)
    xm = jnp.where(row == 0, 0.0, pltpu.roll(x, 1, 0))
    xp = jnp.where(row == n - 1, 0.0, pltpu.roll(x, n - 1, 0))
    w = w_ref[...]
    o_ref[...] = xm * w[0:1] + x * w[1:2] + xp * w[2:3] + b_ref[...]


def _shortconv(x, w, bias):
    b, l, c = x.shape
    ct = _tile(c, 256)
    return pl.pallas_call(
        _shortconv_kernel,
        grid=(b, c // ct),
        in_specs=[
            pl.BlockSpec((None, l, ct), lambda bi, j: (bi, 0, j)),
            pl.BlockSpec((SHORT_K, ct), lambda bi, j: (0, j)),
            pl.BlockSpec((1, ct), lambda bi, j: (0, j)),
        ],
        out_specs=pl.BlockSpec((None, l, ct), lambda bi, j: (bi, 0, j)),
        out_shape=jax.ShapeDtypeStruct((b, l, c), F32),
        compiler_params=_cp("parallel", "parallel"),
        name="shortconv",
    )(x, w, bias.reshape(1, c))


def _filter_kernel(w1_ref, b1_ref, w2_ref, b2_ref, w3_ref, dec_ref, o_ref, *, seq, width):
    tr = o_ref.shape[0]
    n = 2 * seq
    d = lax.broadcasted_iota(jnp.int32, (tr, LANES), 0) + pl.program_id(0) * tr
    pos = jnp.where(d <= seq, d, n - d)
    t = pos.astype(F32) / seq
    lane = lax.broadcasted_iota(jnp.int32, (tr, LANES), 1)
    band = jnp.where(lane <= HY_BANDS, lane, lane - HY_BANDS).astype(F32)
    ang = (2.0 * math.pi * t) * band
    z = jnp.where(lane == 0, t,
                  jnp.where(lane <= HY_BANDS, jnp.sin(ang),
                            jnp.where(lane <= 2 * HY_BANDS, jnp.cos(ang), 0.0)))
    f = jnp.sin(_dot_hi(z, w1_ref[...]) + b1_ref[...])
    f = jnp.sin(_dot_hi(f, w2_ref[...]) + b2_ref[...])
    hw = _dot_hi(f, w3_ref[...]) * jnp.exp(-dec_ref[...] * t[:, :1])
    dcol = d[:, :1]
    mf = jnp.where(dcol < seq, 1.0, 0.0)
    mb = jnp.where((dcol == 0) | (dcol > seq), 1.0, 0.0)
    for o in range(HY_ORDER):
        fwd = hw[:, (2 * o) * width:(2 * o + 1) * width]
        bwd = hw[:, (2 * o + 1) * width:(2 * o + 2) * width]
        o_ref[:, o * width:(o + 1) * width] = mf * fwd + mb * bwd


def _hyena_filter(seq, w1, b1, w2, b2, w3, decay):
    width = decay.shape[-1]
    n = 2 * seq
    tr = min(n, 512)
    hid = w2.shape[0]
    w1p = jnp.zeros((LANES, hid), F32).at[:w1.shape[0]].set(w1)
    ins = [w1p, b1.reshape(1, hid), w2, b2.reshape(1, hid), w3, decay.reshape(1, -1)]
    return pl.pallas_call(
        functools.partial(_filter_kernel, seq=seq, width=width),
        grid=(n // tr,),
        in_specs=[_full(a.shape) for a in ins],
        out_specs=pl.BlockSpec((tr, HY_ORDER * width), lambda i: (i, 0)),
        out_shape=jax.ShapeDtypeStruct((n, HY_ORDER * width), F32),
        compiler_params=_cp("parallel"),
        name="hyena_filter",
    )(*ins)


def _direct_consts(seq):
    n = 2 * seq
    k = np.arange(n, dtype=np.float64)[:, None]
    pos = np.arange(n, dtype=np.float64)[None, :]
    ang = 2.0 * np.pi * k * pos / n
    fwd = np.concatenate([np.cos(ang), -np.sin(ang)], axis=0)
    inv = np.concatenate([np.cos(ang.T), -np.sin(ang.T)], axis=1)[:seq] / n
    return (jnp.asarray(fwd, F32), jnp.asarray(fwd[:, :seq], BF16), jnp.asarray(inv, BF16))


def _dfilt_kernel(kc_ref, f_ref, o_ref):
    o_ref[...] = _dot_hi(f_ref[...], kc_ref[...])


def _direct_filter_spectrum(kc, fwd_full):
    n, ch = kc.shape
    ct = _tile(ch, 256)
    return pl.pallas_call(
        _dfilt_kernel,
        grid=(ch // ct,),
        in_specs=[pl.BlockSpec((n, ct), lambda j: (0, j)), _full(fwd_full.shape)],
        out_specs=pl.BlockSpec((2 * n, ct), lambda j: (0, j)),
        out_shape=jax.ShapeDtypeStruct((2 * n, ch), F32),
        compiler_params=_cp("parallel"),
        name="filter_spectrum_direct",
    )(kc, fwd_full)


def _dconv_kernel(v_ref, x1_ref, x2_ref, kf_ref, bias_ref, f_ref, fi_ref, o_ref):
    n = f_ref.shape[0] // 2
    ct = o_ref.shape[1]
    gates = (x1_ref, x2_ref)
    y = v_ref[...]
    for o in range(HY_ORDER):
        z = _dot(f_ref[...], y.astype(BF16))
        zr, zi = z[:n], z[n:]
        kr = kf_ref[:n, o * ct:(o + 1) * ct]
        ki = kf_ref[n:, o * ct:(o + 1) * ct]
        prod = jnp.concatenate([zr * kr - zi * ki, zr * ki + zi * kr], axis=0).astype(BF16)
        y = gates[o][...] * (_dot(fi_ref[...], prod) + y * bias_ref[o:o + 1, :])
    o_ref[...] = y.astype(BF16)


def _hyena_direct(hs, kf, bias, fwd, inv):
    b, l, c3 = hs.shape
    w = c3 // (HY_ORDER + 1)
    ct = _tile(w, 256)
    nt = w // ct
    return pl.pallas_call(
        _dconv_kernel,
        grid=(nt, b),
        in_specs=[
            pl.BlockSpec((None, l, ct), lambda j, bi: (bi, 0, j)),
            pl.BlockSpec((None, l, ct), lambda j, bi: (bi, 0, nt + j)),
            pl.BlockSpec((None, l, ct), lambda j, bi: (bi, 0, 2 * nt + j)),
            pl.BlockSpec((kf.shape[0], HY_ORDER * ct), lambda j, bi: (0, j)),
            pl.BlockSpec((HY_ORDER, ct), lambda j, bi: (0, j)),
            _full(fwd.shape),
            _full(inv.shape),
        ],
        out_specs=pl.BlockSpec((None, l, ct), lambda j, bi: (bi, 0, j)),
        out_shape=jax.ShapeDtypeStruct((b, l, w), BF16),
        compiler_params=_cp("parallel", "parallel"),
        name="hyena_conv_direct",
    )(hs, hs, hs, kf, bias, fwd, inv)


def _two_stage_consts(seq):
    n = 2 * seq
    n1 = n // FFT_B
    a_in = n1 // 2
    k1 = np.arange(n1, dtype=np.float64)[:, None]
    a = np.arange(n1, dtype=np.float64)[None, :]
    ang1 = 2.0 * np.pi * k1 * a / n1
    f1_full = np.concatenate([np.cos(ang1), -np.sin(ang1)], axis=0)
    f1_inv = np.concatenate([np.cos(ang1.T), -np.sin(ang1.T)], axis=1)[:a_in] / n
    bb = np.arange(FFT_B, dtype=np.float64)[None, :]
    angt = 2.0 * np.pi * k1 * bb / n
    rep = LANES // FFT_B
    tw = np.stack([np.tile(np.cos(angt), (1, rep)), np.tile(-np.sin(angt), (1, rep))])
    b_ = np.arange(FFT_B, dtype=np.float64)[:, None]
    k2 = np.arange(FFT_B, dtype=np.float64)[None, :]
    ang2 = 2.0 * np.pi * b_ * k2 / FFT_B
    bd = lambda m: np.kron(np.eye(rep), m)
    cr, ci = np.cos(ang2), -np.sin(ang2)
    w2 = np.block([[bd(cr), bd(ci)], [bd(-ci), bd(cr)]])
    er, ei = np.cos(ang2), np.sin(ang2)
    w2i = np.block([[bd(er), bd(ei)], [bd(-ei), bd(er)]])
    return dict(
        f1_full=jnp.asarray(f1_full, F32), f1=jnp.asarray(f1_full[:, :a_in], BF16),
        f1_inv=jnp.asarray(f1_inv, BF16), tw=jnp.asarray(tw, F32),
        w2=jnp.asarray(w2, BF16), w2_f32=jnp.asarray(w2, F32), w2i=jnp.asarray(w2i, BF16))


def _fft_fwd(x, f1, tw_r, tw_i, w2, dot, cast):
    n1 = tw_r.shape[0]
    z = dot(f1, cast(x))
    zr, zi = z[:n1], z[n1:]
    rows = []
    for g in range(x.shape[1] // LANES):
        a = zr[:, g * LANES:(g + 1) * LANES]
        b = zi[:, g * LANES:(g + 1) * LANES]
        rows.append(jnp.concatenate([a * tw_r - b * tw_i, a * tw_i + b * tw_r], axis=1))
    return dot(cast(jnp.concatenate(rows, axis=0)), w2)


def _fft_inv(yf, w2i, tw_r, tw_i, f1_inv):
    n1 = tw_r.shape[0]
    gm = _dot(yf.astype(BF16), w2i)
    cols = []
    for g in range(yf.shape[0] // n1):
        blk = gm[g * n1:(g + 1) * n1]
        gr, gi = blk[:, :LANES], blk[:, LANES:]
        cols.append(jnp.concatenate([gr * tw_r + gi * tw_i, gi * tw_r - gr * tw_i], axis=0))
    return _dot(f1_inv, jnp.concatenate(cols, axis=1).astype(BF16))


def _ffilt_kernel(kc_ref, f1_ref, tw_ref, w2_ref, o_ref):
    x = _fft_fwd(kc_ref[...], f1_ref[...], tw_ref[0], tw_ref[1], w2_ref[...], _dot_hi, lambda v: v)
    o_ref[...] = x.reshape(o_ref.shape)


def _two_stage_filter_spectrum(kc_t, cs, groups_per_tile):
    n1, lanes = kc_t.shape
    wt = groups_per_tile * LANES
    ng = lanes // LANES
    return pl.pallas_call(
        _ffilt_kernel,
        grid=(lanes // wt,),
        in_specs=[pl.BlockSpec((n1, wt), lambda j: (0, j)), _full(cs["f1_full"].shape),
                  _full(cs["tw"].shape), _full(cs["w2_f32"].shape)],
        out_specs=pl.BlockSpec((groups_per_tile, n1, 2 * LANES), lambda j: (j, 0, 0)),
        out_shape=jax.ShapeDtypeStruct((ng, n1, 2 * LANES), F32),
        compiler_params=_cp("parallel"),
        name="filter_spectrum_two_stage",
    )(kc_t, cs["f1_full"], cs["tw"], cs["w2_f32"])


def _fconv_kernel(v_ref, x1_ref, x2_ref, kf_ref, bias_ref, f1_ref, tw_ref, w2_ref, w2i_ref, f1i_ref, o_ref):
    tw_r, tw_i = tw_ref[0], tw_ref[1]
    gates = (x1_ref, x2_ref)
    y = v_ref[...]
    for o in range(HY_ORDER):
        x = _fft_fwd(y, f1_ref[...], tw_r, tw_i, w2_ref[...], _dot, lambda v: v.astype(BF16))
        kf = kf_ref[o]
        kf = kf.reshape(kf.shape[0] * kf.shape[1], kf.shape[2])
        xr, xi = x[:, :LANES], x[:, LANES:]
        kr, ki = kf[:, :LANES], kf[:, LANES:]
        prod = jnp.concatenate([xr * kr - xi * ki, xr * ki + xi * kr], axis=1)
        conv = _fft_inv(prod, w2i_ref[...], tw_r, tw_i, f1i_ref[...])
        y = gates[o][...] * (conv + y * bias_ref[o:o + 1, :])
    o_ref[...] = y


def _hyena_two_stage(hs_t, kf, bias_t, cs, width):
    b, a_in, _ = hs_t.shape
    cg = min(width, 32)
    wt = cg * FFT_B
    gpt = wt // LANES
    nt = width // cg
    n1 = kf.shape[2]
    consts = [cs["f1"], cs["tw"], cs["w2"], cs["w2i"], cs["f1_inv"]]
    return pl.pallas_call(
        _fconv_kernel,
        grid=(nt, b),
        in_specs=[
            pl.BlockSpec((None, a_in, wt), lambda j, bi: (bi, 0, j)),
            pl.BlockSpec((None, a_in, wt), lambda j, bi: (bi, 0, nt + j)),
            pl.BlockSpec((None, a_in, wt), lambda j, bi: (bi, 0, 2 * nt + j)),
            pl.BlockSpec((HY_ORDER, gpt, n1, 2 * LANES), lambda j, bi: (0, j, 0, 0)),
            pl.BlockSpec((HY_ORDER, wt), lambda j, bi: (0, j)),
        ] + [_full(c.shape) for c in consts],
        out_specs=pl.BlockSpec((None, a_in, wt), lambda j, bi: (bi, 0, j)),
        out_shape=jax.ShapeDtypeStruct((b, a_in, width * FFT_B), F32),
        compiler_params=_cp("parallel", "parallel"),
        name="hyena_conv_two_stage",
    )(hs_t, hs_t, hs_t, kf, bias_t, *consts)


def _hyena(hy_in, lp):
    b, l, c3 = hy_in.shape
    width = c3 // (HY_ORDER + 1)
    hs = _shortconv(hy_in, lp["hy_conv_w"], lp["hy_conv_b"])
    kc = _hyena_filter(l, lp["hy_w1"], lp["hy_b1"], lp["hy_w2"], lp["hy_b2"], lp["hy_w3"], lp["hy_decay"])
    n = 2 * l
    if l <= DIRECT_DFT_MAX_L:
        fwd_full, fwd, inv = _direct_consts(l)
        ct = _tile(width, 256)
        kf = _direct_filter_spectrum(kc, fwd_full)
        kf = kf.reshape(2 * n, HY_ORDER, width // ct, ct).transpose(0, 2, 1, 3).reshape(2 * n, HY_ORDER * width)
        return _hyena_direct(hs, kf, lp["hy_bias"], fwd, inv)
    cs = _two_stage_consts(l)
    n1 = n // FFT_B
    a_in = l // FFT_B
    ow = HY_ORDER * width
    kc_t = kc.reshape(n1, FFT_B, ow).transpose(0, 2, 1).reshape(n1, ow * FFT_B)
    kf = _two_stage_filter_spectrum(kc_t, cs, min(ow * FFT_B // LANES, 16))
    kf = kf.reshape(HY_ORDER, width * FFT_B // LANES, n1, 2 * LANES)
    hs_t = hs.reshape(b, a_in, FFT_B, c3).transpose(0, 1, 3, 2).reshape(b, a_in, c3 * FFT_B)
    bias_t = jnp.repeat(lp["hy_bias"], FFT_B, axis=1)
    hy_t = _hyena_two_stage(hs_t, kf, bias_t, cs, width)
    return hy_t.reshape(b, a_in, width, FFT_B).transpose(0, 1, 3, 2).reshape(b, l, width).astype(BF16)


def _merge_kernel(x_ref, o_ref, hy_ref, ga_ref, gb_ref, mod_ref, g2_ref, woa_ref, wob_ref, wout_ref,
                  wpq_ref, keys_ref, xn_out, h2_out, s_out):
    mod = mod_ref[...]
    merged = ga_ref[...] * _dot(o_ref[...], woa_ref[...]) + gb_ref[...] * _dot(hy_ref[...], wob_ref[...])
    xn = x_ref[...] + mod[2:3] * _dot(merged.astype(BF16), wout_ref[...])
    xn_out[...] = xn
    h2 = (_rms(xn, g2_ref[...]) * (1.0 + mod[4:5]) + mod[3:4]).astype(BF16)
    h2_out[...] = h2
    q = _dot(h2, wpq_ref[...]).astype(BF16)
    half = keys_ref.shape[2]
    for hp in range(keys_ref.shape[0]):
        s_out[hp] = _dot_nt(keys_ref[hp], q[:, hp * half:(hp + 1) * half])


def _merge(x, attn, hy, ga, gb, mod, lw):
    b, l, d = x.shape
    tm = min(l, 256)
    pb = 1 if mod.shape[0] > 1 else 0
    tok = lambda bi, i: (bi, i, 0)
    w_names = ["g2", "woa", "wob", "wout", "wpq", "keys"]
    nhp = lw["keys"].shape[0]
    return pl.pallas_call(
        _merge_kernel,
        grid=(b, l // tm),
        in_specs=[
            pl.BlockSpec((None, tm, d), tok),
            pl.BlockSpec((None, tm, attn.shape[2]), tok),
            pl.BlockSpec((None, tm, hy.shape[2]), tok),
            pl.BlockSpec((None, tm, d), tok),
            pl.BlockSpec((None, tm, d), tok),
            pl.BlockSpec((None, 6, d), lambda bi, i: (bi * pb, 0, 0)),
        ] + [_full(lw[n].shape) for n in w_names],
        out_specs=[
            pl.BlockSpec((None, tm, d), tok),
            pl.BlockSpec((None, tm, d), tok),
            pl.BlockSpec((None, nhp, N_KEYS, tm), lambda bi, i: (bi, 0, 0, i)),
        ],
        out_shape=[
            jax.ShapeDtypeStruct((b, l, d), F32),
            jax.ShapeDtypeStruct((b, l, d), BF16),
            jax.ShapeDtypeStruct((b, nhp, N_KEYS, l), F32),
        ],
        compiler_params=_cp("parallel", "parallel"),
        name="merge_peer_query",
    )(x, attn, hy, ga, gb, mod, *[lw[n] for n in w_names])


def _top_rows(s, k):
    r = s.shape[0]
    rid = lax.broadcasted_iota(jnp.int32, s.shape, 0).astype(F32)
    vals, idxs = [], []
    for _ in range(k):
        m = jnp.max(s, axis=0, keepdims=True)
        ix = jnp.min(jnp.where(s == m, rid, float(r)), axis=0, keepdims=True)
        vals.append(m)
        idxs.append(ix)
        s = jnp.where(rid == ix, -jnp.inf, s)
    return jnp.concatenate(vals, axis=0), jnp.concatenate(idxs, axis=0)


def _pick_rows(sel, table):
    out = jnp.zeros(sel.shape, F32)
    for r in range(table.shape[0]):
        out = out + jnp.where(sel == float(r), table[r:r + 1], 0.0)
    return out


def _topk_kernel(s_ref, i1_out, i2_out, g_out, i1_sc, i2_sc, g_sc):
    def head(h, carry):
        v1, r1 = _top_rows(s_ref[2 * h], P_TOPK)
        v2, r2 = _top_rows(s_ref[2 * h + 1], P_TOPK)
        cand = jnp.concatenate([v1[a:a + 1] + v2 for a in range(P_TOPK)], axis=0)
        top, pos = _top_rows(cand, P_TOPK)
        pa = jnp.floor(pos / P_TOPK)
        pb = pos - pa * P_TOPK
        e = jnp.exp(top - top[0:1])
        rows = pl.ds(pl.multiple_of(h * P_TOPK, P_TOPK), P_TOPK)
        i1_sc[rows, :] = _pick_rows(pa, r1)
        i2_sc[rows, :] = _pick_rows(pb, r2)
        g_sc[rows, :] = e / jnp.sum(e, axis=0, keepdims=True)
        return carry

    lax.fori_loop(0, P_HEADS, head, 0)
    i1_out[...] = i1_sc[...].T
    i2_out[...] = i2_sc[...].T
    g_out[...] = g_sc[...].T


def _peer_topk(s_t):
    b, nhp, nk, l = s_t.shape
    tt = min(l, 256)
    hk = P_HEADS * P_TOPK
    return pl.pallas_call(
        _topk_kernel,
        grid=(b, l // tt),
        in_specs=[pl.BlockSpec((None, nhp, nk, tt), lambda bi, i: (bi, 0, 0, i))],
        out_specs=[pl.BlockSpec((None, tt, hk), lambda bi, i: (bi, i, 0))] * 3,
        out_shape=[jax.ShapeDtypeStruct((b, l, hk), F32)] * 3,
        scratch_shapes=[pltpu.VMEM((hk, tt), F32)] * 3,
        compiler_params=_cp("parallel", "parallel"),
        name="peer_topk",
    )(s_t)


def _dense_kernel(h_ref, i1_ref, i2_ref, g_ref, u_ref, v_ref, x_ref, mod_ref, o_ref, ws_ref, acc_ref, *, pitch):
    j = pl.program_id(2)
    tm = h_ref.shape[0]
    r = u_ref.shape[0] // N_KEYS

    @pl.when(j == 0)
    def _():
        acc_ref[...] = jnp.zeros_like(acc_ref)
        sub = lax.broadcasted_iota(jnp.int32, (N_KEYS, i1_ref.shape[1]), 0).astype(F32)

        def tok(t, carry):
            row = pl.ds(t, 1)
            pt = jnp.where(i1_ref[row, :] == sub, g_ref[row, :], 0.0).astype(BF16)
            qt = jnp.where(i2_ref[row, :] == sub, 1.0, 0.0).astype(BF16)
            ws_ref[pl.ds(t, N_KEYS, stride=pitch), :] = _dot_nt(pt, qt)
            return carry

        lax.fori_loop(0, tm, tok, 0)

    a = jax.nn.gelu(_dot_nt(h_ref[...], u_ref[...]))
    w = jnp.concatenate(
        [ws_ref[pl.ds(pl.multiple_of((j * r + q) * pitch, SUBLANES), tm), :] for q in range(r)], axis=1)
    acc_ref[...] += _dot((a * w).astype(BF16), v_ref[...])

    @pl.when(j == pl.num_programs(2) - 1)
    def _():
        o_ref[...] = x_ref[...] + mod_ref[5:6, :] * acc_ref[...]


def _peer_dense(h2, i1, i2, g, u_tab, v_tab, xn, mod):
    b, l, d = h2.shape
    tm = min(l, 256)
    te = 512
    ne = u_tab.shape[0]
    hk = i1.shape[2]
    pitch = tm + SUBLANES
    pb = 1 if mod.shape[0] > 1 else 0
    tok = lambda bi, i, j: (bi, i, 0)
    return pl.pallas_call(
        functools.partial(_dense_kernel, pitch=pitch),
        grid=(b, l // tm, ne // te),
        in_specs=[
            pl.BlockSpec((None, tm, d), tok),
            pl.BlockSpec((None, tm, hk), tok),
            pl.BlockSpec((None, tm, hk), tok),
            pl.BlockSpec((None, tm, hk), tok),
            pl.BlockSpec((te, d), lambda bi, i, j: (j, 0)),
            pl.BlockSpec((te, d), lambda bi, i, j: (j, 0)),
            pl.BlockSpec((None, tm, d), tok),
            pl.BlockSpec((None, 6, d), lambda bi, i, j: (bi * pb, 0, 0)),
        ],
        out_specs=pl.BlockSpec((None, tm, d), tok),
        out_shape=jax.ShapeDtypeStruct((b, l, d), F32),
        scratch_shapes=[pltpu.VMEM((N_KEYS * pitch, N_KEYS), F32), pltpu.VMEM((tm, d), F32)],
        compiler_params=_cp("parallel", "parallel", "arbitrary"),
        name="peer_dense",
    )(h2, i1, i2, g, u_tab, v_tab, xn, mod)


def _final_kernel(x_ref, g_ref, o_ref):
    o_ref[...] = _rms(x_ref[...], g_ref[...])


def _final_norm(x, g):
    b, l, d = x.shape
    tm = min(l, 512)
    return pl.pallas_call(
        _final_kernel,
        grid=(b, l // tm),
        in_specs=[pl.BlockSpec((None, tm, d), lambda bi, i: (bi, i, 0)), _full((1, d))],
        out_specs=pl.BlockSpec((None, tm, d), lambda bi, i: (bi, i, 0)),
        out_shape=jax.ShapeDtypeStruct((b, l, d), F32),
        compiler_params=_cp("parallel", "parallel"),
        name="final_norm",
    )(x, g.reshape(1, d))


def _rope_tables(n_tok):
    rows = n_tok // GRID_W
    row = jnp.repeat(jnp.arange(rows, dtype=F32), GRID_W)
    col = jnp.tile(jnp.arange(GRID_W, dtype=F32), rows)
    n_freq = QK_ROPE // 4
    inv = 1.0 / (ROPE_BASE ** (jnp.arange(n_freq, dtype=F32) / n_freq))
    ang = jnp.concatenate([row[:, None] * inv, col[:, None] * inv], axis=-1)
    cos, sin = jnp.cos(ang), jnp.sin(ang)
    half = QK_ROPE // 2
    pad = LANES - QK_NOPE - QK_ROPE
    ones = jnp.ones((n_tok, QK_NOPE), F32)
    z = lambda w: jnp.zeros((n_tok, w), F32)
    c = jnp.concatenate([ones, cos, cos, z(pad)], axis=1)
    s1 = jnp.concatenate([z(QK_NOPE), -sin, z(half), z(pad)], axis=1)
    s2 = jnp.concatenate([z(QK_NOPE), z(half), sin, z(pad)], axis=1)
    return c, s1, s2


def _pad_heads(w, per_head):
    rows = w.shape[0]
    w = w.reshape(rows, N_HEADS, per_head)
    return jnp.pad(w, ((0, 0), (0, 0), (0, LANES - per_head))).reshape(rows, N_HEADS * LANES)


def _layer_weights(p, l):
    d = p["w_in"].shape[1]
    q_rank = p["q_norm"].shape[1]
    kv_rank = p["kv_norm"].shape[1]
    hy3 = p["hy_conv_w"].shape[2]
    w_in = p["w_in"][l]
    c0, c1, c2, c3 = q_rank, q_rank + kv_rank, q_rank + kv_rank + QK_ROPE, q_rank + kv_rank + QK_ROPE + hy3
    wkr = jnp.zeros((d, LANES), F32).at[:, QK_NOPE:QK_NOPE + QK_ROPE].set(w_in[:, c1:c2])
    ukv = p["w_ukv"][l].reshape(kv_rank, N_HEADS, QK_NOPE + V_DIM)
    wuk = _pad_heads(ukv[:, :, :QK_NOPE].reshape(kv_rank, -1), QK_NOPE)
    wuv = _pad_heads(ukv[:, :, QK_NOPE:].reshape(kv_rank, -1), V_DIM)
    woa = p["w_oa"][l].reshape(N_HEADS, V_DIM, d)
    woa = jnp.pad(woa, ((0, 0), (0, LANES - V_DIM), (0, 0))).reshape(N_HEADS * LANES, d)
    keys = p["peer_keys"][l]
    row = lambda v: v.reshape(1, -1)
    return dict(
        gn=row(p["norm_mix"][l]), wq=w_in[:, :c0].astype(BF16), wckv=w_in[:, c0:c1].astype(BF16),
        wkr=wkr.astype(BF16), why=w_in[:, c2:c3].astype(BF16), wg=w_in[:, c3:].astype(BF16),
        qn=row(p["q_norm"][l]), kvn=row(p["kv_norm"][l]),
        wuq=_pad_heads(p["w_uq"][l], QK_NOPE + QK_ROPE).astype(BF16),
        wukv=jnp.concatenate([wuk, wuv], axis=1).astype(BF16),
        g2=row(p["norm_ffn"][l]), woa=woa.astype(BF16), wob=p["w_ob"][l].astype(BF16),
        wout=p["w_out"][l].astype(BF16), wpq=p["peer_wq"][l].astype(BF16),
        keys=keys.reshape(keys.shape[0] * keys.shape[1], keys.shape[2], keys.shape[3]).astype(BF16),
        u=p["peer_u"][l].astype(BF16), v=p["peer_v"][l].astype(BF16),
    )


def _layer(x, mod, lw, lp, rope_tabs, ctx_kv):
    q, k, v, ckv, kr, hy_in, ga, gb = _inproj(x, mod, lw, rope_tabs)
    if ctx_kv is None:
        attn = _attention(q, k, v)
    else:
        attn = _attention(q, k, v, *ctx_kv)
    hy = _hyena(hy_in, lp)
    xn, h2, s_t = _merge(x, attn, hy, ga, gb, mod, lw)
    i1, i2, g = _peer_topk(s_t)
    x = _peer_dense(h2, i1, i2, g, lw["u"], lw["v"], xn, mod)
    return x, ckv, kr[:, :, QK_NOPE:QK_NOPE + QK_ROPE]


def kernel(x_prompt, x_sample, cache_ckv, cache_krope, c, c_ctx, w_ada, b_ada, norm_mix, norm_ffn,
           w_in, q_norm, kv_norm, w_uq, w_ukv, w_oa, w_ob, w_out, hy_conv_w, hy_conv_b,
           hy_w1, hy_b1, hy_w2, hy_b2, hy_w3, hy_decay, hy_bias,
           peer_wq, peer_keys, peer_u, peer_v, final_norm):
    p = dict(norm_mix=norm_mix, norm_ffn=norm_ffn, w_in=w_in, q_norm=q_norm, kv_norm=kv_norm, w_uq=w_uq,
             w_ukv=w_ukv, w_oa=w_oa, w_ob=w_ob, w_out=w_out, peer_wq=peer_wq, peer_keys=peer_keys,
             peer_u=peer_u, peer_v=peer_v, hy_conv_w=hy_conv_w)
    depth, d = norm_mix.shape
    bd = x_sample.shape[0]
    rows = -(-(bd + 1) // SUBLANES) * SUBLANES
    cc = jnp.zeros((rows, d), F32).at[:bd].set(c).at[bd].set(c_ctx)
    mods = _adaln(cc, w_ada, b_ada).reshape(depth, rows, 6, d)
    rope_tabs = _rope_tables(x_sample.shape[1])
    e_mat = jnp.zeros((QK_ROPE, N_HEADS, LANES), F32)
    e_mat = e_mat.at[:, :, QK_NOPE:QK_NOPE + QK_ROPE].set(jnp.eye(QK_ROPE, dtype=F32)[:, None, :])
    e_mat = e_mat.reshape(QK_ROPE, N_HEADS * LANES).astype(BF16)
    xp, xs = x_prompt, x_sample
    new_ckv, new_kr = [], []
    for l in range(depth):
        lw = _layer_weights(p, l)
        lp = dict(hy_conv_w=hy_conv_w[l], hy_conv_b=hy_conv_b[l], hy_w1=hy_w1[l], hy_b1=hy_b1[l],
                  hy_w2=hy_w2[l], hy_b2=hy_b2[l], hy_w3=hy_w3[l], hy_decay=hy_decay[l], hy_bias=hy_bias[l])
        xp, ckv_p, kr_p = _layer(xp, mods[l, bd:bd + 1], lw, lp, None, None)
        new_ckv.append(ckv_p)
        new_kr.append(kr_p)
        ctx_kv = _kvcache(cache_ckv, cache_krope, l, lw["wukv"], e_mat)
        xs, _, _ = _layer(xs, mods[l, :bd], lw, lp, rope_tabs, ctx_kv)
    return (_final_norm(xp, final_norm), _final_norm(xs, final_norm),
            jnp.stack(new_ckv, axis=1), jnp.stack(new_kr, axis=1))
```

```python
import functools
import math

import numpy as np
import jax
import jax.numpy as jnp
from jax import lax
from jax.experimental import pallas as pl
from jax.experimental.pallas import tpu as pltpu

F32 = jnp.float32
BF16 = jnp.bfloat16

GRID_W = 64
N_HEADS = 8
QK_NOPE = 64
QK_ROPE = 32
V_DIM = 64
ROPE_BASE = 10000.0
HY_ORDER = 2
HY_BANDS = 16
SHORT_K = 3
P_HEADS = 8
N_KEYS = 128
P_TOPK = 16
EPS = 1e-6

LANES = 128
SUBLANES = 8
FFT_B = 64
DIRECT_DFT_MAX_L = 256
VMEM_LIMIT_BYTES = 48 * 1024 * 1024
DENSE_VMEM_LIMIT_BYTES = 56 * 1024 * 1024


def _cp(*sem):
    return pltpu.CompilerParams(dimension_semantics=sem, vmem_limit_bytes=VMEM_LIMIT_BYTES)


def _full(shape):
    n = len(shape)
    return pl.BlockSpec(shape, lambda *_: (0,) * n)


def _tile(n, pref, unit=LANES):
    if n <= pref:
        return n
    return max(t for t in range(unit, pref + 1, unit) if n % t == 0)


def _dot(a, b):
    return jnp.dot(a, b, preferred_element_type=F32)


def _dot_hi(a, b):
    return jnp.dot(a, b, precision=lax.Precision.HIGHEST, preferred_element_type=F32)


def _dot_nt(a, b):
    return lax.dot_general(a, b, (((1,), (1,)), ((), ())), preferred_element_type=F32)


def _rms(x, g):
    return x * lax.rsqrt(jnp.mean(x * x, axis=-1, keepdims=True) + EPS) * g


def _adaln_kernel(c_ref, w_ref, b_ref, o_ref):
    c = c_ref[...]
    s = (c * jax.nn.sigmoid(c)).astype(BF16)
    o_ref[...] = _dot(s, w_ref[...].astype(BF16)) + b_ref[...]


def _adaln(cc, w_ada, b_ada):
    depth, d, d6 = w_ada.shape
    rows = cc.shape[0]
    tn = d6 // 4
    return pl.pallas_call(
        _adaln_kernel,
        grid=(depth, d6 // tn),
        in_specs=[
            _full((rows, d)),
            pl.BlockSpec((None, d, tn), lambda l, j: (l, 0, j)),
            pl.BlockSpec((None, 1, tn), lambda l, j: (l, 0, j)),
        ],
        out_specs=pl.BlockSpec((None, rows, tn), lambda l, j: (l, 0, j)),
        out_shape=jax.ShapeDtypeStruct((depth, rows, d6), F32),
        compiler_params=_cp("parallel", "parallel"),
        name="adaln",
    )(cc, w_ada, b_ada.reshape(depth, 1, d6))


def _rope_slab(x, c, s1, s2):
    half = QK_ROPE // 2
    return x * c + pltpu.roll(x, LANES - half, 1) * s1 + pltpu.roll(x, half, 1) * s2


def _inproj_kernel(*refs, rope):
    if rope:
        (x_ref, mod_ref, gn_ref, wq_ref, wckv_ref, wkr_ref, why_ref, wg_ref, qn_ref, kvn_ref,
         wuq_ref, wukv_ref, c_ref, s1_ref, s2_ref,
         q_out, k_out, v_out, ckv_out, kr_out, hy_out, ga_out, gb_out) = refs
    else:
        (x_ref, mod_ref, gn_ref, wq_ref, wckv_ref, wkr_ref, why_ref, wg_ref, qn_ref, kvn_ref,
         wuq_ref, wukv_ref,
         q_out, k_out, v_out, ckv_out, kr_out, hy_out, ga_out, gb_out) = refs
    x = x_ref[...]
    mod = mod_ref[...]
    d = x.shape[1]
    hl = N_HEADS * LANES
    h = (_rms(x, gn_ref[...]) * (1.0 + mod[1:2]) + mod[0:1]).astype(BF16)
    q = _dot(_rms(_dot(h, wq_ref[...]), qn_ref[...]).astype(BF16), wuq_ref[...])
    ckv = _rms(_dot(h, wckv_ref[...]), kvn_ref[...])
    kr = _dot(h, wkr_ref[...])
    kv = _dot(ckv.astype(BF16), wukv_ref[...])
    if rope:
        c, s1, s2 = c_ref[...], s1_ref[...], s2_ref[...]
        kr_k = _rope_slab(kr, c, s1, s2)
        q = jnp.concatenate(
            [_rope_slab(q[:, i * LANES:(i + 1) * LANES], c, s1, s2) for i in range(N_HEADS)], axis=1)
    else:
        kr_k = kr
    q_out[...] = q.astype(BF16)
    k_out[...] = (kv[:, :hl] + jnp.concatenate([kr_k] * N_HEADS, axis=1)).astype(BF16)
    v_out[...] = kv[:, hl:].astype(BF16)
    ckv_out[...] = ckv
    kr_out[...] = kr
    hy_out[...] = _dot(h, why_ref[...])
    g = _dot(h, wg_ref[...])
    ga_out[...] = jax.nn.sigmoid(g[:, :d])
    gb_out[...] = jax.nn.sigmoid(g[:, d:])


def _inproj(x, mod, lw, rope_tabs):
    b, l, d = x.shape
    tm = min(l, 256)
    pb = 1 if mod.shape[0] > 1 else 0
    hl = N_HEADS * LANES
    kvr = lw["wckv"].shape[1]
    hyw = lw["why"].shape[1]
    rope = rope_tabs is not None
    tok = lambda bi, i: (bi, i, 0)
    w_names = ["gn", "wq", "wckv", "wkr", "why", "wg", "qn", "kvn", "wuq", "wukv"]
    ins = [x, mod] + [lw[n] for n in w_names]
    in_specs = [pl.BlockSpec((None, tm, d), tok), pl.BlockSpec((None, 6, d), lambda bi, i: (bi * pb, 0, 0))]
    in_specs += [_full(lw[n].shape) for n in w_names]
    if rope:
        ins += list(rope_tabs)
        in_specs += [pl.BlockSpec((tm, LANES), lambda bi, i: (i, 0))] * 3
    widths = [(hl, BF16), (hl, BF16), (hl, BF16), (kvr, F32), (LANES, F32), (hyw, F32), (d, F32), (d, F32)]
    return pl.pallas_call(
        functools.partial(_inproj_kernel, rope=rope),
        grid=(b, l // tm),
        in_specs=in_specs,
        out_specs=[pl.BlockSpec((None, tm, w), tok) for w, _ in widths],
        out_shape=[jax.ShapeDtypeStruct((b, l, w), dt) for w, dt in widths],
        compiler_params=_cp("parallel", "parallel"),
        name="inproj_rope" if rope else "inproj",
    )(*ins)


def _kvcache_kernel(ckv_ref, kr_ref, wukv_ref, e_ref, k_out, v_out):
    hl = N_HEADS * LANES
    kv = _dot(ckv_ref[...].astype(BF16), wukv_ref[...])
    k_out[...] = (kv[:, :hl] + _dot(kr_ref[...].astype(BF16), e_ref[...])).astype(BF16)
    v_out[...] = kv[:, hl:].astype(BF16)


def _kvcache(cache_ckv, cache_kr, layer, wukv, e_mat):
    b, _, p, kvr = cache_ckv.shape
    hl = N_HEADS * LANES
    return pl.pallas_call(
        _kvcache_kernel,
        grid=(b,),
        in_specs=[
            pl.BlockSpec((None, None, p, kvr), lambda bi: (bi, layer, 0, 0)),
            pl.BlockSpec((None, None, p, QK_ROPE), lambda bi: (bi, layer, 0, 0)),
            _full(wukv.shape),
            _full(e_mat.shape),
        ],
        out_specs=[pl.BlockSpec((None, p, hl), lambda bi: (bi, 0, 0))] * 2,
        out_shape=[jax.ShapeDtypeStruct((b, p, hl), BF16)] * 2,
        compiler_params=_cp("parallel"),
        name="kvcache",
    )(cache_ckv, cache_kr, wukv, e_mat)


def _attn_kernel(*refs, ctx, scale):
    if ctx:
        q_ref, k_ref, v_ref, kc_ref, vc_ref, o_ref = refs
    else:
        q_ref, k_ref, v_ref, o_ref = refs
    q = q_ref[...]
    s = _dot_nt(q, k_ref[...]) * scale
    m = jnp.max(s, axis=1, keepdims=True)
    if ctx:
        sc = _dot_nt(q, kc_ref[...]) * scale
        m = jnp.maximum(m, jnp.max(sc, axis=1, keepdims=True))
    p = jnp.exp(s - m)
    den = jnp.sum(p, axis=1, keepdims=True)
    o = _dot(p.astype(BF16), v_ref[...])
    if ctx:
        pc = jnp.exp(sc - m)
        den = den + jnp.sum(pc, axis=1, keepdims=True)
        o = o + _dot(pc.astype(BF16), vc_ref[...])
    o_ref[...] = (o / den).astype(BF16)


def _attention(q, k, v, kc=None, vc=None):
    b, l, hl = q.shape
    tq = min(l, 256)
    ctx = kc is not None
    ins = [q, k, v]
    in_specs = [
        pl.BlockSpec((None, tq, LANES), lambda bi, h, i: (bi, i, h)),
        pl.BlockSpec((None, l, LANES), lambda bi, h, i: (bi, 0, h)),
        pl.BlockSpec((None, l, LANES), lambda bi, h, i: (bi, 0, h)),
    ]
    if ctx:
        p = kc.shape[1]
        ins += [kc, vc]
        in_specs += [pl.BlockSpec((None, p, LANES), lambda bi, h, i: (bi, 0, h))] * 2
    return pl.pallas_call(
        functools.partial(_attn_kernel, ctx=ctx, scale=1.0 / math.sqrt(QK_NOPE + QK_ROPE)),
        grid=(b, N_HEADS, l // tq),
        in_specs=in_specs,
        out_specs=pl.BlockSpec((None, tq, LANES), lambda bi, h, i: (bi, i, h)),
        out_shape=jax.ShapeDtypeStruct((b, l, hl), BF16),
        compiler_params=_cp("parallel", "parallel", "parallel"),
        name="attn_ctx" if ctx else "attn",
    )(*ins)


def _shortconv_kernel(x_ref, w_ref, b_ref, o_ref):
    x = x_ref[...]
    n = x.shape[0]
    row = lax.broadcasted_iota(jnp.int32, x.shape, 0)
    xm = jnp.where(row == 0, 0.0, pltpu.roll(x, 1, 0))
    xp = jnp.where(row == n - 1, 0.0, pltpu.roll(x, n - 1, 0))
    w = w_ref[...]
    o_ref[...] = xm * w[0:1] + x * w[1:2] + xp * w[2:3] + b_ref[...]


def _shortconv(x, w, bias):
    b, l, c = x.shape
    ct = _tile(c, 256)
    return pl.pallas_call(
        _shortconv_kernel,
        grid=(b, c // ct),
        in_specs=[
            pl.BlockSpec((None, l, ct), lambda bi, j: (bi, 0, j)),
            pl.BlockSpec((SHORT_K, ct), lambda bi, j: (0, j)),
            pl.BlockSpec((1, ct), lambda bi, j: (0, j)),
        ],
        out_specs=pl.BlockSpec((None, l, ct), lambda bi, j: (bi, 0, j)),
        out_shape=jax.ShapeDtypeStruct((b, l, c), F32),
        compiler_params=_cp("parallel", "parallel"),
        name="shortconv",
    )(x, w, bias.reshape(1, c))


def _filter_kernel(w1_ref, b1_ref, w2_ref, b2_ref, w3_ref, dec_ref, o_ref, *, seq, width):
    tr = o_ref.shape[0]
    n = 2 * seq
    d = lax.broadcasted_iota(jnp.int32, (tr, LANES), 0) + pl.program_id(0) * tr
    pos = jnp.where(d <= seq, d, n - d)
    t = pos.astype(F32) / seq
    lane = lax.broadcasted_iota(jnp.int32, (tr, LANES), 1)
    band = jnp.where(lane <= HY_BANDS, lane, lane - HY_BANDS).astype(F32)
    ang = (2.0 * math.pi * t) * band
    z = jnp.where(lane == 0, t,
                  jnp.where(lane <= HY_BANDS, jnp.sin(ang),
                            jnp.where(lane <= 2 * HY_BANDS, jnp.cos(ang), 0.0)))
    f = jnp.sin(_dot_hi(z, w1_ref[...]) + b1_ref[...])
    f = jnp.sin(_dot_hi(f, w2_ref[...]) + b2_ref[...])
    hw = _dot_hi(f, w3_ref[...]) * jnp.exp(-dec_ref[...] * t[:, :1])
    dcol = d[:, :1]
    mf = jnp.where(dcol < seq, 1.0, 0.0)
    mb = jnp.where((dcol == 0) | (dcol > seq), 1.0, 0.0)
    for o in range(HY_ORDER):
        fwd = hw[:, (2 * o) * width:(2 * o + 1) * width]
        bwd = hw[:, (2 * o + 1) * width:(2 * o + 2) * width]
        o_ref[:, o * width:(o + 1) * width] = mf * fwd + mb * bwd


def _hyena_filter(seq, w1, b1, w2, b2, w3, decay):
    width = decay.shape[-1]
    n = 2 * seq
    tr = min(n, 512)
    hid = w2.shape[0]
    w1p = jnp.zeros((LANES, hid), F32).at[:w1.shape[0]].set(w1)
    ins = [w1p, b1.reshape(1, hid), w2, b2.reshape(1, hid), w3, decay.reshape(1, -1)]
    return pl.pallas_call(
        functools.partial(_filter_kernel, seq=seq, width=width),
        grid=(n // tr,),
        in_specs=[_full(a.shape) for a in ins],
        out_specs=pl.BlockSpec((tr, HY_ORDER * width), lambda i: (i, 0)),
        out_shape=jax.ShapeDtypeStruct((n, HY_ORDER * width), F32),
        compiler_params=_cp("parallel"),
        name="hyena_filter",
    )(*ins)


def _direct_consts(seq):
    n = 2 * seq
    k = np.arange(n, dtype=np.float64)[:, None]
    pos = np.arange(n, dtype=np.float64)[None, :]
    ang = 2.0 * np.pi * k * pos / n
    fwd = np.concatenate([np.cos(ang), -np.sin(ang)], axis=0)
    inv = np.concatenate([np.cos(ang.T), -np.sin(ang.T)], axis=1)[:seq] / n
    return (jnp.asarray(fwd, F32), jnp.asarray(fwd[:, :seq], BF16), jnp.asarray(inv, BF16))


def _dfilt_kernel(kc_ref, f_ref, o_ref):
    o_ref[...] = _dot_hi(f_ref[...], kc_ref[...])


def _direct_filter_spectrum(kc, fwd_full):
    n, ch = kc.shape
    ct = _tile(ch, 256)
    return pl.pallas_call(
        _dfilt_kernel,
        grid=(ch // ct,),
        in_specs=[pl.BlockSpec((n, ct), lambda j: (0, j)), _full(fwd_full.shape)],
        out_specs=pl.BlockSpec((2 * n, ct), lambda j: (0, j)),
        out_shape=jax.ShapeDtypeStruct((2 * n, ch), F32),
        compiler_params=_cp("parallel"),
        name="filter_spectrum_direct",
    )(kc, fwd_full)


def _dconv_kernel(v_ref, x1_ref, x2_ref, kf_ref, bias_ref, f_ref, fi_ref, o_ref):
    n = f_ref.shape[0] // 2
    ct = o_ref.shape[1]
    gates = (x1_ref, x2_ref)
    y = v_ref[...]
    for o in range(HY_ORDER):
        z = _dot(f_ref[...], y.astype(BF16))
        zr, zi = z[:n], z[n:]
        kr = kf_ref[:n, o * ct:(o + 1) * ct]
        ki = kf_ref[n:, o * ct:(o + 1) * ct]
        prod = jnp.concatenate([zr * kr - zi * ki, zr * ki + zi * kr], axis=0).astype(BF16)
        y = gates[o][...] * (_dot(fi_ref[...], prod) + y * bias_ref[o:o + 1, :])
    o_ref[...] = y.astype(BF16)


def _hyena_direct(hs, kf, bias, fwd, inv):
    b, l, c3 = hs.shape
    w = c3 // (HY_ORDER + 1)
    ct = _tile(w, 256)
    nt = w // ct
    return pl.pallas_call(
        _dconv_kernel,
        grid=(nt, b),
        in_specs=[
            pl.BlockSpec((None, l, ct), lambda j, bi: (bi, 0, j)),
            pl.BlockSpec((None, l, ct), lambda j, bi: (bi, 0, nt + j)),
            pl.BlockSpec((None, l, ct), lambda j, bi: (bi, 0, 2 * nt + j)),
            pl.BlockSpec((kf.shape[0], HY_ORDER * ct), lambda j, bi: (0, j)),
            pl.BlockSpec((HY_ORDER, ct), lambda j, bi: (0, j)),
            _full(fwd.shape),
            _full(inv.shape),
        ],
        out_specs=pl.BlockSpec((None, l, ct), lambda j, bi: (bi, 0, j)),
        out_shape=jax.ShapeDtypeStruct((b, l, w), BF16),
        compiler_params=_cp("parallel", "parallel"),
        name="hyena_conv_direct",
    )(hs, hs, hs, kf, bias, fwd, inv)


def _two_stage_consts(seq):
    n = 2 * seq
    n1 = n // FFT_B
    a_in = n1 // 2
    k1 = np.arange(n1, dtype=np.float64)[:, None]
    a = np.arange(n1, dtype=np.float64)[None, :]
    ang1 = 2.0 * np.pi * k1 * a / n1
    f1_full = np.concatenate([np.cos(ang1), -np.sin(ang1)], axis=0)
    f1_inv = np.concatenate([np.cos(ang1.T), -np.sin(ang1.T)], axis=1)[:a_in] / n
    bb = np.arange(FFT_B, dtype=np.float64)[None, :]
    angt = 2.0 * np.pi * k1 * bb / n
    rep = LANES // FFT_B
    tw = np.stack([np.tile(np.cos(angt), (1, rep)), np.tile(-np.sin(angt), (1, rep))])
    b_ = np.arange(FFT_B, dtype=np.float64)[:, None]
    k2 = np.arange(FFT_B, dtype=np.float64)[None, :]
    ang2 = 2.0 * np.pi * b_ * k2 / FFT_B
    bd = lambda m: np.kron(np.eye(rep), m)
    cr, ci = np.cos(ang2), -np.sin(ang2)
    w2 = np.block([[bd(cr), bd(ci)], [bd(-ci), bd(cr)]])
    er, ei = np.cos(ang2), np.sin(ang2)
    w2i = np.block([[bd(er), bd(ei)], [bd(-ei), bd(er)]])
    return dict(
        f1_full=jnp.asarray(f1_full, F32), f1=jnp.asarray(f1_full[:, :a_in], BF16),
        f1_inv=jnp.asarray(f1_inv, BF16), tw=jnp.asarray(tw, F32),
        w2=jnp.asarray(w2, BF16), w2_f32=jnp.asarray(w2, F32), w2i=jnp.asarray(w2i, BF16))


def _fft_fwd(x, f1, tw_r, tw_i, w2, dot, cast):
    n1 = tw_r.shape[0]
    z = dot(f1, cast(x))
    zr, zi = z[:n1], z[n1:]
    rows = []
    for g in range(x.shape[1] // LANES):
        a = zr[:, g * LANES:(g + 1) * LANES]
        b = zi[:, g * LANES:(g + 1) * LANES]
        rows.append(jnp.concatenate([a * tw_r - b * tw_i, a * tw_i + b * tw_r], axis=1))
    return dot(cast(jnp.concatenate(rows, axis=0)), w2)


def _fft_inv(yf, w2i, tw_r, tw_i, f1_inv):
    n1 = tw_r.shape[0]
    gm = _dot(yf.astype(BF16), w2i)
    cols = []
    for g in range(yf.shape[0] // n1):
        blk = gm[g * n1:(g + 1) * n1]
        gr, gi = blk[:, :LANES], blk[:, LANES:]
        cols.append(jnp.concatenate([gr * tw_r + gi * tw_i, gi * tw_r - gr * tw_i], axis=0))
    return _dot(f1_inv, jnp.concatenate(cols, axis=1).astype(BF16))


def _ffilt_kernel(kc_ref, f1_ref, tw_ref, w2_ref, o_ref):
    x = _fft_fwd(kc_ref[...], f1_ref[...], tw_ref[0], tw_ref[1], w2_ref[...], _dot_hi, lambda v: v)
    o_ref[...] = x.reshape(o_ref.shape)


def _two_stage_filter_spectrum(kc_t, cs, groups_per_tile):
    n1, lanes = kc_t.shape
    wt = groups_per_tile * LANES
    ng = lanes // LANES
    return pl.pallas_call(
        _ffilt_kernel,
        grid=(lanes // wt,),
        in_specs=[pl.BlockSpec((n1, wt), lambda j: (0, j)), _full(cs["f1_full"].shape),
                  _full(cs["tw"].shape), _full(cs["w2_f32"].shape)],
        out_specs=pl.BlockSpec((groups_per_tile, n1, 2 * LANES), lambda j: (j, 0, 0)),
        out_shape=jax.ShapeDtypeStruct((ng, n1, 2 * LANES), F32),
        compiler_params=_cp("parallel"),
        name="filter_spectrum_two_stage",
    )(kc_t, cs["f1_full"], cs["tw"], cs["w2_f32"])


def _fconv_kernel(v_ref, x1_ref, x2_ref, kf_ref, bias_ref, f1_ref, tw_ref, w2_ref, w2i_ref, f1i_ref, o_ref):
    tw_r, tw_i = tw_ref[0], tw_ref[1]
    gates = (x1_ref, x2_ref)
    y = v_ref[...]
    for o in range(HY_ORDER):
        x = _fft_fwd(y, f1_ref[...], tw_r, tw_i, w2_ref[...], _dot, lambda v: v.astype(BF16))
        kf = kf_ref[o]
        kf = kf.reshape(kf.shape[0] * kf.shape[1], kf.shape[2])
        xr, xi = x[:, :LANES], x[:, LANES:]
        kr, ki = kf[:, :LANES], kf[:, LANES:]
        prod = jnp.concatenate([xr * kr - xi * ki, xr * ki + xi * kr], axis=1)
        conv = _fft_inv(prod, w2i_ref[...], tw_r, tw_i, f1i_ref[...])
        y = gates[o][...] * (conv + y * bias_ref[o:o + 1, :])
    o_ref[...] = y


def _hyena_two_stage(hs_t, kf, bias_t, cs, width):
    b, a_in, _ = hs_t.shape
    cg = min(width, 32)
    wt = cg * FFT_B
    gpt = wt // LANES
    nt = width // cg
    n1 = kf.shape[2]
    consts = [cs["f1"], cs["tw"], cs["w2"], cs["w2i"], cs["f1_inv"]]
    return pl.pallas_call(
        _fconv_kernel,
        grid=(nt, b),
        in_specs=[
            pl.BlockSpec((None, a_in, wt), lambda j, bi: (bi, 0, j)),
            pl.BlockSpec((None, a_in, wt), lambda j, bi: (bi, 0, nt + j)),
            pl.BlockSpec((None, a_in, wt), lambda j, bi: (bi, 0, 2 * nt + j)),
            pl.BlockSpec((HY_ORDER, gpt, n1, 2 * LANES), lambda j, bi: (0, j, 0, 0)),
            pl.BlockSpec((HY_ORDER, wt), lambda j, bi: (0, j)),
        ] + [_full(c.shape) for c in consts],
        out_specs=pl.BlockSpec((None, a_in, wt), lambda j, bi: (bi, 0, j)),
        out_shape=jax.ShapeDtypeStruct((b, a_in, width * FFT_B), F32),
        compiler_params=_cp("parallel", "parallel"),
        name="hyena_conv_two_stage",
    )(hs_t, hs_t, hs_t, kf, bias_t, *consts)


def _hyena(hy_in, lp):
    b, l, c3 = hy_in.shape
    width = c3 // (HY_ORDER + 1)
    hs = _shortconv(hy_in, lp["hy_conv_w"], lp["hy_conv_b"])
    kc = _hyena_filter(l, lp["hy_w1"], lp["hy_b1"], lp["hy_w2"], lp["hy_b2"], lp["hy_w3"], lp["hy_decay"])
    n = 2 * l
    if l <= DIRECT_DFT_MAX_L:
        fwd_full, fwd, inv = _direct_consts(l)
        ct = _tile(width, 256)
        kf = _direct_filter_spectrum(kc, fwd_full)
        kf = kf.reshape(2 * n, HY_ORDER, width // ct, ct).transpose(0, 2, 1, 3).reshape(2 * n, HY_ORDER * width)
        return _hyena_direct(hs, kf, lp["hy_bias"], fwd, inv)
    cs = _two_stage_consts(l)
    n1 = n // FFT_B
    a_in = l // FFT_B
    ow = HY_ORDER * width
    kc_t = kc.reshape(n1, FFT_B, ow).transpose(0, 2, 1).reshape(n1, ow * FFT_B)
    kf = _two_stage_filter_spectrum(kc_t, cs, min(ow * FFT_B // LANES, 16))
    kf = kf.reshape(HY_ORDER, width * FFT_B // LANES, n1, 2 * LANES)
    hs_t = hs.reshape(b, a_in, FFT_B, c3).transpose(0, 1, 3, 2).reshape(b, a_in, c3 * FFT_B)
    bias_t = jnp.repeat(lp["hy_bias"], FFT_B, axis=1)
    hy_t = _hyena_two_stage(hs_t, kf, bias_t, cs, width)
    return hy_t.reshape(b, a_in, width, FFT_B).transpose(0, 1, 3, 2).reshape(b, l, width).astype(BF16)


def _merge_kernel(x_ref, o_ref, hy_ref, ga_ref, gb_ref, mod_ref, g2_ref, woa_ref, wob_ref, wout_ref,
                  wpq_ref, keys_ref, xn_out, h2_out, s_out):
    mod = mod_ref[...]
    merged = ga_ref[...] * _dot(o_ref[...], woa_ref[...]) + gb_ref[...] * _dot(hy_ref[...], wob_ref[...])
    xn = x_ref[...] + mod[2:3] * _dot(merged.astype(BF16), wout_ref[...])
    xn_out[...] = xn
    h2 = (_rms(xn, g2_ref[...]) * (1.0 + mod[4:5]) + mod[3:4]).astype(BF16)
    h2_out[...] = h2
    q = _dot(h2, wpq_ref[...]).astype(BF16)
    half = keys_ref.shape[2]
    for hp in range(keys_ref.shape[0]):
        s_out[hp] = _dot_nt(keys_ref[hp], q[:, hp * half:(hp + 1) * half])


def _merge(x, attn, hy, ga, gb, mod, lw):
    b, l, d = x.shape
    tm = min(l, 256)
    pb = 1 if mod.shape[0] > 1 else 0
    tok = lambda bi, i: (bi, i, 0)
    w_names = ["g2", "woa", "wob", "wout", "wpq", "keys"]
    nhp = lw["keys"].shape[0]
    return pl.pallas_call(
        _merge_kernel,
        grid=(b, l // tm),
        in_specs=[
            pl.BlockSpec((None, tm, d), tok),
            pl.BlockSpec((None, tm, attn.shape[2]), tok),
            pl.BlockSpec((None, tm, hy.shape[2]), tok),
            pl.BlockSpec((None, tm, d), tok),
            pl.BlockSpec((None, tm, d), tok),
            pl.BlockSpec((None, 6, d), lambda bi, i: (bi * pb, 0, 0)),
        ] + [_full(lw[n].shape) for n in w_names],
        out_specs=[
            pl.BlockSpec((None, tm, d), tok),
            pl.BlockSpec((None, tm, d), tok),
            pl.BlockSpec((None, nhp, N_KEYS, tm), lambda bi, i: (bi, 0, 0, i)),
        ],
        out_shape=[
            jax.ShapeDtypeStruct((b, l, d), F32),
            jax.ShapeDtypeStruct((b, l, d), BF16),
            jax.ShapeDtypeStruct((b, nhp, N_KEYS, l), F32),
        ],
        compiler_params=_cp("parallel", "parallel"),
        name="merge_peer_query",
    )(x, attn, hy, ga, gb, mod, *[lw[n] for n in w_names])


def _top_rows(s, k, payloads=()):
    r = s.shape[0]
    rid = lax.broadcasted_iota(jnp.int32, s.shape, 0).astype(F32)
    vals, idxs, picked = [], [], [[] for _ in payloads]
    for _ in range(k):
        m = jnp.max(s, axis=0, keepdims=True)
        ix = jnp.min(jnp.where(s == m, rid, float(r)), axis=0, keepdims=True)
        hit = rid == ix
        vals.append(m)
        idxs.append(ix)
        for p, acc in zip(payloads, picked):
            acc.append(jnp.sum(jnp.where(hit, p, 0.0), axis=0, keepdims=True))
        s = jnp.where(hit, -jnp.inf, s)
    cat = lambda rows: jnp.concatenate(rows, axis=0)
    return cat(vals), cat(idxs), [cat(a) for a in picked]


def _staircase(v1, r1, v2, r2):
    k = v1.shape[0]
    cand, e1, e2 = [], [], []
    for a in range(k // 2):
        nb = k // (a + 1)
        rows = -(-nb // SUBLANES) * SUBLANES
        c = v1[a:a + 1] + v2[:rows]
        if nb < rows:
            c = jnp.where(lax.broadcasted_iota(jnp.int32, c.shape, 0) < nb, c, -jnp.inf)
        cand.append(c)
        e1.append(jnp.broadcast_to(r1[a:a + 1], c.shape))
        e2.append(r2[:rows])
    cand.append(v1[k // 2:] + v2[0:1])
    e1.append(r1[k // 2:])
    e2.append(jnp.broadcast_to(r2[0:1], e1[-1].shape))
    cat = lambda rows: jnp.concatenate(rows, axis=0)
    return cat(cand), cat(e1), cat(e2)


def _topk_kernel(s_ref, i1_out, i2_out, g_out, i1_sc, i2_sc, g_sc):
    def head(h, carry):
        v1, r1, _ = _top_rows(s_ref[2 * h], P_TOPK)
        v2, r2, _ = _top_rows(s_ref[2 * h + 1], P_TOPK)
        cand, e1, e2 = _staircase(v1, r1, v2, r2)
        top, _, (k1, k2) = _top_rows(cand, P_TOPK, (e1, e2))
        e = jnp.exp(top - top[0:1])
        rows = pl.ds(pl.multiple_of(h * P_TOPK, P_TOPK), P_TOPK)
        i1_sc[rows, :] = k1
        i2_sc[rows, :] = k2
        g_sc[rows, :] = e / jnp.sum(e, axis=0, keepdims=True)
        return carry

    lax.fori_loop(0, P_HEADS, head, 0)
    i1_out[...] = i1_sc[...].T
    i2_out[...] = i2_sc[...].T
    g_out[...] = g_sc[...].T


def _peer_topk(s_t):
    b, nhp, nk, l = s_t.shape
    tt = min(l, 256)
    hk = P_HEADS * P_TOPK
    return pl.pallas_call(
        _topk_kernel,
        grid=(b, l // tt),
        in_specs=[pl.BlockSpec((None, nhp, nk, tt), lambda bi, i: (bi, 0, 0, i))],
        out_specs=[pl.BlockSpec((None, tt, hk), lambda bi, i: (bi, i, 0))] * 3,
        out_shape=[jax.ShapeDtypeStruct((b, l, hk), F32)] * 3,
        scratch_shapes=[pltpu.VMEM((hk, tt), F32)] * 3,
        compiler_params=_cp("parallel", "parallel"),
        name="peer_topk",
    )(s_t)


BF16_BITS = 0xFFFF0000
DENSE_TOKEN_UNROLL = 8


def _dense_kernel(h_ref, i1_ref, i2_ref, g_ref, u_ref, v_ref, x_ref, mod_ref, o_ref, ws_ref, acc_ref, *, pitch):
    j = pl.program_id(2)
    tm = h_ref.shape[0]
    r = u_ref.shape[0] // N_KEYS
    half = N_KEYS // 2

    @pl.when(j == 0)
    def _():
        acc_ref[...] = jnp.zeros_like(acc_ref)
        sub = lax.broadcasted_iota(jnp.int32, (N_KEYS, i1_ref.shape[1]), 0).astype(F32)

        def toks(tb, carry):
            for u in range(DENSE_TOKEN_UNROLL):
                t = tb * DENSE_TOKEN_UNROLL + u
                row = pl.ds(t, 1)
                pt = jnp.where(i1_ref[row, :] == sub, g_ref[row, :], 0.0).astype(BF16)
                qt = jnp.where(i2_ref[row, :] == sub, 1.0, 0.0).astype(BF16)
                bits = lax.bitcast_convert_type(_dot_nt(pt, qt).astype(BF16).astype(F32), jnp.uint32)
                word = (bits[:half] & jnp.uint32(BF16_BITS)) | (bits[half:] >> 16)
                ws_ref[pl.ds(t, half, stride=pitch), :] = word
            return carry

        lax.fori_loop(0, tm // DENSE_TOKEN_UNROLL, toks, 0)

    a = jax.nn.gelu(_dot_nt(h_ref[...], u_ref[...]))
    first = j * r
    shift = jnp.where(first >= half, 0, 16).astype(jnp.uint32)
    base = lax.rem(first, half)
    cols = []
    for q in range(r):
        word = ws_ref[pl.ds(pl.multiple_of((base + q) * pitch, SUBLANES), tm), :]
        cols.append(lax.bitcast_convert_type(((word >> shift) << 16), F32))
    acc_ref[...] += _dot((a * jnp.concatenate(cols, axis=1)).astype(BF16), v_ref[...])

    @pl.when(j == pl.num_programs(2) - 1)
    def _():
        o_ref[...] = x_ref[...] + mod_ref[5:6, :] * acc_ref[...]


def _peer_dense(h2, i1, i2, g, u_tab, v_tab, xn, mod):
    b, l, d = h2.shape
    tm = min(l, 512)
    te = 1024
    ne = u_tab.shape[0]
    hk = i1.shape[2]
    pitch = tm + SUBLANES
    pb = 1 if mod.shape[0] > 1 else 0
    assert tm % DENSE_TOKEN_UNROLL == 0 and (N_KEYS // 2) % (te // N_KEYS) == 0
    tok = lambda bi, i, j: (bi, i, 0)
    return pl.pallas_call(
        functools.partial(_dense_kernel, pitch=pitch),
        grid=(b, l // tm, ne // te),
        in_specs=[
            pl.BlockSpec((None, tm, d), tok),
            pl.BlockSpec((None, tm, hk), tok),
            pl.BlockSpec((None, tm, hk), tok),
            pl.BlockSpec((None, tm, hk), tok),
            pl.BlockSpec((te, d), lambda bi, i, j: (j, 0)),
            pl.BlockSpec((te, d), lambda bi, i, j: (j, 0)),
            pl.BlockSpec((None, tm, d), tok, pipeline_mode=pl.Buffered(1)),
            pl.BlockSpec((None, 6, d), lambda bi, i, j: (bi * pb, 0, 0)),
        ],
        out_specs=pl.BlockSpec((None, tm, d), tok),
        out_shape=jax.ShapeDtypeStruct((b, l, d), F32),
        scratch_shapes=[pltpu.VMEM((N_KEYS // 2 * pitch, N_KEYS), jnp.uint32), pltpu.VMEM((tm, d), F32)],
        compiler_params=pltpu.CompilerParams(dimension_semantics=("parallel", "parallel", "arbitrary"),
                                             vmem_limit_bytes=DENSE_VMEM_LIMIT_BYTES),
        name="peer_dense",
    )(h2, i1, i2, g, u_tab, v_tab, xn, mod)


def _final_kernel(x_ref, g_ref, o_ref):
    o_ref[...] = _rms(x_ref[...], g_ref[...])


def _final_norm(x, g):
    b, l, d = x.shape
    tm = min(l, 512)
    return pl.pallas_call(
        _final_kernel,
        grid=(b, l // tm),
        in_specs=[pl.BlockSpec((None, tm, d), lambda bi, i: (bi, i, 0)), _full((1, d))],
        out_specs=pl.BlockSpec((None, tm, d), lambda bi, i: (bi, i, 0)),
        out_shape=jax.ShapeDtypeStruct((b, l, d), F32),
        compiler_params=_cp("parallel", "parallel"),
        name="final_norm",
    )(x, g.reshape(1, d))


def _rope_tables(n_tok):
    rows = n_tok // GRID_W
    row = jnp.repeat(jnp.arange(rows, dtype=F32), GRID_W)
    col = jnp.tile(jnp.arange(GRID_W, dtype=F32), rows)
    n_freq = QK_ROPE // 4
    inv = 1.0 / (ROPE_BASE ** (jnp.arange(n_freq, dtype=F32) / n_freq))
    ang = jnp.concatenate([row[:, None] * inv, col[:, None] * inv], axis=-1)
    cos, sin = jnp.cos(ang), jnp.sin(ang)
    half = QK_ROPE // 2
    pad = LANES - QK_NOPE - QK_ROPE
    ones = jnp.ones((n_tok, QK_NOPE), F32)
    z = lambda w: jnp.zeros((n_tok, w), F32)
    c = jnp.concatenate([ones, cos, cos, z(pad)], axis=1)
    s1 = jnp.concatenate([z(QK_NOPE), -sin, z(half), z(pad)], axis=1)
    s2 = jnp.concatenate([z(QK_NOPE), z(half), sin, z(pad)], axis=1)
    return c, s1, s2


def _pad_heads(w, per_head):
    rows = w.shape[0]
    w = w.reshape(rows, N_HEADS, per_head)
    return jnp.pad(w, ((0, 0), (0, 0), (0, LANES - per_head))).reshape(rows, N_HEADS * LANES)


def _layer_weights(p, l):
    d = p["w_in"].shape[1]
    q_rank = p["q_norm"].shape[1]
    kv_rank = p["kv_norm"].shape[1]
    hy3 = p["hy_conv_w"].shape[2]
    w_in = p["w_in"][l]
    c0, c1, c2, c3 = q_rank, q_rank + kv_rank, q_rank + kv_rank + QK_ROPE, q_rank + kv_rank + QK_ROPE + hy3
    wkr = jnp.zeros((d, LANES), F32).at[:, QK_NOPE:QK_NOPE + QK_ROPE].set(w_in[:, c1:c2])
    ukv = p["w_ukv"][l].reshape(kv_rank, N_HEADS, QK_NOPE + V_DIM)
    wuk = _pad_heads(ukv[:, :, :QK_NOPE].reshape(kv_rank, -1), QK_NOPE)
    wuv = _pad_heads(ukv[:, :, QK_NOPE:].reshape(kv_rank, -1), V_DIM)
    woa = p["w_oa"][l].reshape(N_HEADS, V_DIM, d)
    woa = jnp.pad(woa, ((0, 0), (0, LANES - V_DIM), (0, 0))).reshape(N_HEADS * LANES, d)
    keys = p["peer_keys"][l]
    row = lambda v: v.reshape(1, -1)
    return dict(
        gn=row(p["norm_mix"][l]), wq=w_in[:, :c0].astype(BF16), wckv=w_in[:, c0:c1].astype(BF16),
        wkr=wkr.astype(BF16), why=w_in[:, c2:c3].astype(BF16), wg=w_in[:, c3:].astype(BF16),
        qn=row(p["q_norm"][l]), kvn=row(p["kv_norm"][l]),
        wuq=_pad_heads(p["w_uq"][l], QK_NOPE + QK_ROPE).astype(BF16),
        wukv=jnp.concatenate([wuk, wuv], axis=1).astype(BF16),
        g2=row(p["norm_ffn"][l]), woa=woa.astype(BF16), wob=p["w_ob"][l].astype(BF16),
        wout=p["w_out"][l].astype(BF16), wpq=p["peer_wq"][l].astype(BF16),
        keys=keys.reshape(keys.shape[0] * keys.shape[1], keys.shape[2], keys.shape[3]).astype(BF16),
        u=p["peer_u"][l].astype(BF16), v=p["peer_v"][l].astype(BF16),
    )


def _layer(x, mod, lw, lp, rope_tabs, ctx_kv):
    q, k, v, ckv, kr, hy_in, ga, gb = _inproj(x, mod, lw, rope_tabs)
    if ctx_kv is None:
        attn = _attention(q, k, v)
    else:
        attn = _attention(q, k, v, *ctx_kv)
    hy = _hyena(hy_in, lp)
    xn, h2, s_t = _merge(x, attn, hy, ga, gb, mod, lw)
    i1, i2, g = _peer_topk(s_t)
    x = _peer_dense(h2, i1, i2, g, lw["u"], lw["v"], xn, mod)
    return x, ckv, kr[:, :, QK_NOPE:QK_NOPE + QK_ROPE]


def kernel(x_prompt, x_sample, cache_ckv, cache_krope, c, c_ctx, w_ada, b_ada, norm_mix, norm_ffn,
           w_in, q_norm, kv_norm, w_uq, w_ukv, w_oa, w_ob, w_out, hy_conv_w, hy_conv_b,
           hy_w1, hy_b1, hy_w2, hy_b2, hy_w3, hy_decay, hy_bias,
           peer_wq, peer_keys, peer_u, peer_v, final_norm):
    p = dict(norm_mix=norm_mix, norm_ffn=norm_ffn, w_in=w_in, q_norm=q_norm, kv_norm=kv_norm, w_uq=w_uq,
             w_ukv=w_ukv, w_oa=w_oa, w_ob=w_ob, w_out=w_out, peer_wq=peer_wq, peer_keys=peer_keys,
             peer_u=peer_u, peer_v=peer_v, hy_conv_w=hy_conv_w)
    depth, d = norm_mix.shape
    bd = x_sample.shape[0]
    rows = -(-(bd + 1) // SUBLANES) * SUBLANES
    cc = jnp.zeros((rows, d), F32).at[:bd].set(c).at[bd].set(c_ctx)
    mods = _adaln(cc, w_ada, b_ada).reshape(depth, rows, 6, d)
    rope_tabs = _rope_tables(x_sample.shape[1])
    e_mat = jnp.zeros((QK_ROPE, N_HEADS, LANES), F32)
    e_mat = e_mat.at[:, :, QK_NOPE:QK_NOPE + QK_ROPE].set(jnp.eye(QK_ROPE, dtype=F32)[:, None, :])
    e_mat = e_mat.reshape(QK_ROPE, N_HEADS * LANES).astype(BF16)
    xp, xs = x_prompt, x_sample
    new_ckv, new_kr = [], []
    for l in range(depth):
        lw = _layer_weights(p, l)
        lp = dict(hy_conv_w=hy_conv_w[l], hy_conv_b=hy_conv_b[l], hy_w1=hy_w1[l], hy_b1=hy_b1[l],
                  hy_w2=hy_w2[l], hy_b2=hy_b2[l], hy_w3=hy_w3[l], hy_decay=hy_decay[l], hy_bias=hy_bias[l])
        xp, ckv_p, kr_p = _layer(xp, mods[l, bd:bd + 1], lw, lp, None, None)
        new_ckv.append(ckv_p)
        new_kr.append(kr_p)
        ctx_kv = _kvcache(cache_ckv, cache_krope, l, lw["wukv"], e_mat)
        xs, _, _ = _layer(xs, mods[l, :bd], lw, lp, rope_tabs, ctx_kv)
    return (_final_norm(xp, final_norm), _final_norm(xs, final_norm),
            jnp.stack(new_ckv, axis=1), jnp.stack(new_kr, axis=1))
```

```python
import functools
import math

import numpy as np
import jax
import jax.numpy as jnp
from jax import lax
from jax.experimental import pallas as pl
from jax.experimental.pallas import tpu as pltpu

F32 = jnp.float32
BF16 = jnp.bfloat16

GRID_W = 64
N_HEADS = 8
QK_NOPE = 64
QK_ROPE = 32
V_DIM = 64
ROPE_BASE = 10000.0
HY_ORDER = 2
HY_BANDS = 16
SHORT_K = 3
P_HEADS = 8
N_KEYS = 128
P_TOPK = 16
EPS = 1e-6

LANES = 128
SUBLANES = 8
FFT_B = 64
DIRECT_DFT_MAX_L = 256
VMEM_LIMIT_BYTES = 48 * 1024 * 1024
DENSE_VMEM_LIMIT_BYTES = 56 * 1024 * 1024


def _cp(*sem):
    return pltpu.CompilerParams(dimension_semantics=sem, vmem_limit_bytes=VMEM_LIMIT_BYTES)


def _full(shape):
    n = len(shape)
    return pl.BlockSpec(shape, lambda *_: (0,) * n)


def _tile(n, pref, unit=LANES):
    if n <= pref:
        return n
    return max(t for t in range(unit, pref + 1, unit) if n % t == 0)


def _dot(a, b):
    return jnp.dot(a, b, preferred_element_type=F32)


def _dot_hi(a, b):
    return jnp.dot(a, b, precision=lax.Precision.HIGHEST, preferred_element_type=F32)


def _dot_nt(a, b):
    return lax.dot_general(a, b, (((1,), (1,)), ((), ())), preferred_element_type=F32)


def _rms(x, g):
    return x * lax.rsqrt(jnp.mean(x * x, axis=-1, keepdims=True) + EPS) * g


def _adaln_kernel(c_ref, w_ref, b_ref, o_ref):
    c = c_ref[...]
    s = (c * jax.nn.sigmoid(c)).astype(BF16)
    o_ref[...] = _dot(s, w_ref[...].astype(BF16)) + b_ref[...]


def _adaln(cc, w_ada, b_ada):
    depth, d, d6 = w_ada.shape
    rows = cc.shape[0]
    tn = d6 // 4
    return pl.pallas_call(
        _adaln_kernel,
        grid=(depth, d6 // tn),
        in_specs=[
            _full((rows, d)),
            pl.BlockSpec((None, d, tn), lambda l, j: (l, 0, j)),
            pl.BlockSpec((None, 1, tn), lambda l, j: (l, 0, j)),
        ],
        out_specs=pl.BlockSpec((None, rows, tn), lambda l, j: (l, 0, j)),
        out_shape=jax.ShapeDtypeStruct((depth, rows, d6), F32),
        compiler_params=_cp("parallel", "parallel"),
        name="adaln",
    )(cc, w_ada, b_ada.reshape(depth, 1, d6))


ATTN_SCALE = 1.0 / math.sqrt(QK_NOPE + QK_ROPE)
ATTN_KEY_CHUNK = 1024


def _rope_slab(x, c, s1, s2):
    half = QK_ROPE // 2
    return x * c + pltpu.roll(x, LANES - half, 1) * s1 + pltpu.roll(x, half, 1) * s2


def _inproj_kernel(*refs, rope):
    if rope:
        (x_ref, mod_ref, gn_ref, wq_ref, wckv_ref, wkr_ref, why_ref, wg_ref, qn_ref, kvn_ref,
         wuq_ref, wuk_ref, c_ref, s1_ref, s2_ref,
         q_out, k_out, cb_out, ckv_out, kr_out, hy_out, ga_out, gb_out) = refs
    else:
        (x_ref, mod_ref, gn_ref, wq_ref, wckv_ref, wkr_ref, why_ref, wg_ref, qn_ref, kvn_ref,
         wuq_ref, wuk_ref,
         q_out, k_out, cb_out, ckv_out, kr_out, hy_out, ga_out, gb_out) = refs
    x = x_ref[...]
    mod = mod_ref[...]
    d = x.shape[1]
    h = (_rms(x, gn_ref[...]) * (1.0 + mod[1:2]) + mod[0:1]).astype(BF16)
    q = _dot(_rms(_dot(h, wq_ref[...]), qn_ref[...]).astype(BF16), wuq_ref[...])
    ckv = _rms(_dot(h, wckv_ref[...]), kvn_ref[...])
    kr = _dot(h, wkr_ref[...])
    ckv_b = ckv.astype(BF16)
    kn = _dot(ckv_b, wuk_ref[...])
    if rope:
        c, s1, s2 = c_ref[...], s1_ref[...], s2_ref[...]
        kr_k = _rope_slab(kr, c, s1, s2)
        q = jnp.concatenate(
            [_rope_slab(q[:, i * LANES:(i + 1) * LANES], c, s1, s2) for i in range(N_HEADS)], axis=1)
    else:
        kr_k = kr
    q_out[...] = (q * ATTN_SCALE).astype(BF16)
    k_out[...] = (kn + jnp.concatenate([kr_k] * N_HEADS, axis=1)).astype(BF16)
    cb_out[...] = ckv_b
    ckv_out[...] = ckv
    kr_out[...] = kr
    hy_out[...] = _dot(h, why_ref[...])
    g = _dot(h, wg_ref[...])
    ga_out[...] = jax.nn.sigmoid(g[:, :d])
    gb_out[...] = jax.nn.sigmoid(g[:, d:])


def _inproj(x, mod, lw, rope_tabs):
    b, l, d = x.shape
    tm = min(l, 256)
    pb = 1 if mod.shape[0] > 1 else 0
    hl = N_HEADS * LANES
    kvr = lw["wckv"].shape[1]
    hyw = lw["why"].shape[1]
    rope = rope_tabs is not None
    tok = lambda bi, i: (bi, i, 0)
    w_names = ["gn", "wq", "wckv", "wkr", "why", "wg", "qn", "kvn", "wuq", "wuk"]
    ins = [x, mod] + [lw[n] for n in w_names]
    in_specs = [pl.BlockSpec((None, tm, d), tok), pl.BlockSpec((None, 6, d), lambda bi, i: (bi * pb, 0, 0))]
    in_specs += [_full(lw[n].shape) for n in w_names]
    if rope:
        ins += list(rope_tabs)
        in_specs += [pl.BlockSpec((tm, LANES), lambda bi, i: (i, 0))] * 3
    widths = [(hl, BF16), (hl, BF16), (kvr, BF16), (kvr, F32), (LANES, F32), (hyw, F32), (d, F32), (d, F32)]
    return pl.pallas_call(
        functools.partial(_inproj_kernel, rope=rope),
        grid=(b, l // tm),
        in_specs=in_specs,
        out_specs=[pl.BlockSpec((None, tm, w), tok) for w, _ in widths],
        out_shape=[jax.ShapeDtypeStruct((b, l, w), dt) for w, dt in widths],
        compiler_params=_cp("parallel", "parallel"),
        name="inproj_rope" if rope else "inproj",
    )(*ins)


def _kvcache_kernel(ckv_ref, kr_ref, wuk_ref, e_ref, k_out, cb_out):
    ckv_b = ckv_ref[...].astype(BF16)
    k_out[...] = (_dot(ckv_b, wuk_ref[...]) + _dot(kr_ref[...].astype(BF16), e_ref[...])).astype(BF16)
    cb_out[...] = ckv_b


def _kvcache(cache_ckv, cache_kr, layer, wuk, e_mat):
    b, _, p, kvr = cache_ckv.shape
    hl = N_HEADS * LANES
    return pl.pallas_call(
        _kvcache_kernel,
        grid=(b,),
        in_specs=[
            pl.BlockSpec((None, None, p, kvr), lambda bi: (bi, layer, 0, 0)),
            pl.BlockSpec((None, None, p, QK_ROPE), lambda bi: (bi, layer, 0, 0)),
            _full(wuk.shape),
            _full(e_mat.shape),
        ],
        out_specs=[pl.BlockSpec((None, p, hl), lambda bi: (bi, 0, 0)),
                   pl.BlockSpec((None, p, kvr), lambda bi: (bi, 0, 0))],
        out_shape=[jax.ShapeDtypeStruct((b, p, hl), BF16), jax.ShapeDtypeStruct((b, p, kvr), BF16)],
        compiler_params=_cp("parallel"),
        name="kvcache",
    )(cache_ckv, cache_kr, wuk, e_mat)


def _attn_kernel(*refs, ctx):
    if ctx:
        q_ref, k_ref, c_ref, wuv_ref, kc_ref, cc_ref, o_ref = refs
    else:
        q_ref, k_ref, c_ref, wuv_ref, o_ref = refs
    q = q_ref[...]
    chunks = [(kc_ref, cc_ref, 0, kc_ref.shape[0])] if ctx else []
    n_keys = k_ref.shape[0]
    ck = min(n_keys, ATTN_KEY_CHUNK)
    chunks += [(k_ref, c_ref, c * ck, ck) for c in range(n_keys // ck)]
    m = acc = den = None
    for kr, cr, start, size in chunks:
        s = _dot_nt(q, kr[pl.ds(start, size), :])
        cmax = jnp.max(s, axis=1, keepdims=True)
        if m is None:
            m = cmax
            p = jnp.exp(s - m)
            den = jnp.sum(p, axis=1, keepdims=True)
            acc = _dot(p.astype(BF16), cr[pl.ds(start, size), :])
        else:
            m_new = jnp.maximum(m, cmax)
            alpha = jnp.exp(m - m_new)
            p = jnp.exp(s - m_new)
            den = den * alpha + jnp.sum(p, axis=1, keepdims=True)
            acc = acc * alpha + _dot(p.astype(BF16), cr[pl.ds(start, size), :])
            m = m_new
    o_ref[...] = _dot((acc / den).astype(BF16), wuv_ref[...]).astype(BF16)


def _attention(q, k, cb, wuv, kc=None, cc=None):
    b, l, hl = q.shape
    kvr = cb.shape[2]
    tq = min(l, 512)
    ctx = kc is not None
    ins = [q, k, cb, wuv]
    in_specs = [
        pl.BlockSpec((None, tq, LANES), lambda bi, h, i: (bi, i, h)),
        pl.BlockSpec((None, l, LANES), lambda bi, h, i: (bi, 0, h)),
        pl.BlockSpec((None, l, kvr), lambda bi, h, i: (bi, 0, 0)),
        pl.BlockSpec((kvr, LANES), lambda bi, h, i: (0, h)),
    ]
    if ctx:
        p = kc.shape[1]
        ins += [kc, cc]
        in_specs += [pl.BlockSpec((None, p, LANES), lambda bi, h, i: (bi, 0, h)),
                     pl.BlockSpec((None, p, kvr), lambda bi, h, i: (bi, 0, 0))]
    return pl.pallas_call(
        functools.partial(_attn_kernel, ctx=ctx),
        grid=(b, N_HEADS, l // tq),
        in_specs=in_specs,
        out_specs=pl.BlockSpec((None, tq, LANES), lambda bi, h, i: (bi, i, h)),
        out_shape=jax.ShapeDtypeStruct((b, l, hl), BF16),
        compiler_params=_cp("parallel", "parallel", "parallel"),
        name="attn_ctx" if ctx else "attn",
    )(*ins)


def _shortconv_kernel(x_ref, w_ref, b_ref, o_ref):
    x = x_ref[...]
    n = x.shape[0]
    row = lax.broadcasted_iota(jnp.int32, x.shape, 0)
    xm = jnp.where(row == 0, 0.0, pltpu.roll(x, 1, 0))
    xp = jnp.where(row == n - 1, 0.0, pltpu.roll(x, n - 1, 0))
    w = w_ref[...]
    o_ref[...] = xm * w[0:1] + x * w[1:2] + xp * w[2:3] + b_ref[...]


def _shortconv(x, w, bias):
    b, l, c = x.shape
    ct = _tile(c, 256)
    return pl.pallas_call(
        _shortconv_kernel,
        grid=(b, c // ct),
        in_specs=[
            pl.BlockSpec((None, l, ct), lambda bi, j: (bi, 0, j)),
            pl.BlockSpec((SHORT_K, ct), lambda bi, j: (0, j)),
            pl.BlockSpec((1, ct), lambda bi, j: (0, j)),
        ],
        out_specs=pl.BlockSpec((None, l, ct), lambda bi, j: (bi, 0, j)),
        out_shape=jax.ShapeDtypeStruct((b, l, c), F32),
        compiler_params=_cp("parallel", "parallel"),
        name="shortconv",
    )(x, w, bias.reshape(1, c))


def _filter_kernel(w1_ref, b1_ref, w2_ref, b2_ref, w3_ref, dec_ref, o_ref, *, seq, width):
    tr = o_ref.shape[0]
    n = 2 * seq
    d = lax.broadcasted_iota(jnp.int32, (tr, LANES), 0) + pl.program_id(0) * tr
    pos = jnp.where(d <= seq, d, n - d)
    t = pos.astype(F32) / seq
    lane = lax.broadcasted_iota(jnp.int32, (tr, LANES), 1)
    band = jnp.where(lane <= HY_BANDS, lane, lane - HY_BANDS).astype(F32)
    ang = (2.0 * math.pi * t) * band
    z = jnp.where(lane == 0, t,
                  jnp.where(lane <= HY_BANDS, jnp.sin(ang),
                            jnp.where(lane <= 2 * HY_BANDS, jnp.cos(ang), 0.0)))
    f = jnp.sin(_dot_hi(z, w1_ref[...]) + b1_ref[...])
    f = jnp.sin(_dot_hi(f, w2_ref[...]) + b2_ref[...])
    hw = _dot_hi(f, w3_ref[...]) * jnp.exp(-dec_ref[...] * t[:, :1])
    dcol = d[:, :1]
    mf = jnp.where(dcol < seq, 1.0, 0.0)
    mb = jnp.where((dcol == 0) | (dcol > seq), 1.0, 0.0)
    for o in range(HY_ORDER):
        fwd = hw[:, (2 * o) * width:(2 * o + 1) * width]
        bwd = hw[:, (2 * o + 1) * width:(2 * o + 2) * width]
        o_ref[:, o * width:(o + 1) * width] = mf * fwd + mb * bwd


def _hyena_filter(seq, w1, b1, w2, b2, w3, decay):
    width = decay.shape[-1]
    n = 2 * seq
    tr = min(n, 512)
    hid = w2.shape[0]
    w1p = jnp.zeros((LANES, hid), F32).at[:w1.shape[0]].set(w1)
    ins = [w1p, b1.reshape(1, hid), w2, b2.reshape(1, hid), w3, decay.reshape(1, -1)]
    return pl.pallas_call(
        functools.partial(_filter_kernel, seq=seq, width=width),
        grid=(n // tr,),
        in_specs=[_full(a.shape) for a in ins],
        out_specs=pl.BlockSpec((tr, HY_ORDER * width), lambda i: (i, 0)),
        out_shape=jax.ShapeDtypeStruct((n, HY_ORDER * width), F32),
        compiler_params=_cp("parallel"),
        name="hyena_filter",
    )(*ins)


def _direct_consts(seq):
    n = 2 * seq
    k = np.arange(n, dtype=np.float64)[:, None]
    pos = np.arange(n, dtype=np.float64)[None, :]
    ang = 2.0 * np.pi * k * pos / n
    fwd = np.concatenate([np.cos(ang), -np.sin(ang)], axis=0)
    inv = np.concatenate([np.cos(ang.T), -np.sin(ang.T)], axis=1)[:seq] / n
    return (jnp.asarray(fwd, F32), jnp.asarray(fwd[:, :seq], BF16), jnp.asarray(inv, BF16))


def _dfilt_kernel(kc_ref, f_ref, o_ref):
    o_ref[...] = _dot_hi(f_ref[...], kc_ref[...])


def _direct_filter_spectrum(kc, fwd_full):
    n, ch = kc.shape
    ct = _tile(ch, 256)
    return pl.pallas_call(
        _dfilt_kernel,
        grid=(ch // ct,),
        in_specs=[pl.BlockSpec((n, ct), lambda j: (0, j)), _full(fwd_full.shape)],
        out_specs=pl.BlockSpec((2 * n, ct), lambda j: (0, j)),
        out_shape=jax.ShapeDtypeStruct((2 * n, ch), F32),
        compiler_params=_cp("parallel"),
        name="filter_spectrum_direct",
    )(kc, fwd_full)


def _dconv_kernel(v_ref, x1_ref, x2_ref, kf_ref, bias_ref, f_ref, fi_ref, o_ref):
    n = f_ref.shape[0] // 2
    ct = o_ref.shape[1]
    gates = (x1_ref, x2_ref)
    y = v_ref[...]
    for o in range(HY_ORDER):
        z = _dot(f_ref[...], y.astype(BF16))
        zr, zi = z[:n], z[n:]
        kr = kf_ref[:n, o * ct:(o + 1) * ct]
        ki = kf_ref[n:, o * ct:(o + 1) * ct]
        prod = jnp.concatenate([zr * kr - zi * ki, zr * ki + zi * kr], axis=0).astype(BF16)
        y = gates[o][...] * (_dot(fi_ref[...], prod) + y * bias_ref[o:o + 1, :])
    o_ref[...] = y.astype(BF16)


def _hyena_direct(hs, kf, bias, fwd, inv):
    b, l, c3 = hs.shape
    w = c3 // (HY_ORDER + 1)
    ct = _tile(w, 256)
    nt = w // ct
    return pl.pallas_call(
        _dconv_kernel,
        grid=(nt, b),
        in_specs=[
            pl.BlockSpec((None, l, ct), lambda j, bi: (bi, 0, j)),
            pl.BlockSpec((None, l, ct), lambda j, bi: (bi, 0, nt + j)),
            pl.BlockSpec((None, l, ct), lambda j, bi: (bi, 0, 2 * nt + j)),
            pl.BlockSpec((kf.shape[0], HY_ORDER * ct), lambda j, bi: (0, j)),
            pl.BlockSpec((HY_ORDER, ct), lambda j, bi: (0, j)),
            _full(fwd.shape),
            _full(inv.shape),
        ],
        out_specs=pl.BlockSpec((None, l, ct), lambda j, bi: (bi, 0, j)),
        out_shape=jax.ShapeDtypeStruct((b, l, w), BF16),
        compiler_params=_cp("parallel", "parallel"),
        name="hyena_conv_direct",
    )(hs, hs, hs, kf, bias, fwd, inv)


def _two_stage_consts(seq):
    n = 2 * seq
    n1 = n // FFT_B
    a_in = n1 // 2
    k1 = np.arange(n1, dtype=np.float64)[:, None]
    a = np.arange(n1, dtype=np.float64)[None, :]
    ang1 = 2.0 * np.pi * k1 * a / n1
    f1_full = np.concatenate([np.cos(ang1), -np.sin(ang1)], axis=0)
    f1_inv = np.concatenate([np.cos(ang1.T), -np.sin(ang1.T)], axis=1)[:a_in] / n
    bb = np.arange(FFT_B, dtype=np.float64)[None, :]
    angt = 2.0 * np.pi * k1 * bb / n
    rep = LANES // FFT_B
    tw = np.stack([np.tile(np.cos(angt), (1, rep)), np.tile(-np.sin(angt), (1, rep))])
    b_ = np.arange(FFT_B, dtype=np.float64)[:, None]
    k2 = np.arange(FFT_B, dtype=np.float64)[None, :]
    ang2 = 2.0 * np.pi * b_ * k2 / FFT_B
    bd = lambda m: np.kron(np.eye(rep), m)
    cr, ci = np.cos(ang2), -np.sin(ang2)
    w2 = np.block([[bd(cr), bd(ci)], [bd(-ci), bd(cr)]])
    er, ei = np.cos(ang2), np.sin(ang2)
    w2i = np.block([[bd(er), bd(ei)], [bd(-ei), bd(er)]])
    return dict(
        f1_full=jnp.asarray(f1_full, F32), f1=jnp.asarray(f1_full[:, :a_in], BF16),
        f1_inv=jnp.asarray(f1_inv, BF16), tw=jnp.asarray(tw, F32),
        w2=jnp.asarray(w2, BF16), w2_f32=jnp.asarray(w2, F32), w2i=jnp.asarray(w2i, BF16))


def _fft_fwd(x, f1, tw_r, tw_i, w2, dot, cast):
    n1 = tw_r.shape[0]
    z = dot(f1, cast(x))
    zr, zi = z[:n1], z[n1:]
    rows = []
    for g in range(x.shape[1] // LANES):
        a = zr[:, g * LANES:(g + 1) * LANES]
        b = zi[:, g * LANES:(g + 1) * LANES]
        rows.append(jnp.concatenate([a * tw_r - b * tw_i, a * tw_i + b * tw_r], axis=1))
    return dot(cast(jnp.concatenate(rows, axis=0)), w2)


def _fft_inv(yf, w2i, tw_r, tw_i, f1_inv):
    n1 = tw_r.shape[0]
    gm = _dot(yf.astype(BF16), w2i)
    cols = []
    for g in range(yf.shape[0] // n1):
        blk = gm[g * n1:(g + 1) * n1]
        gr, gi = blk[:, :LANES], blk[:, LANES:]
        cols.append(jnp.concatenate([gr * tw_r + gi * tw_i, gi * tw_r - gr * tw_i], axis=0))
    return _dot(f1_inv, jnp.concatenate(cols, axis=1).astype(BF16))


def _ffilt_kernel(kc_ref, f1_ref, tw_ref, w2_ref, o_ref):
    x = _fft_fwd(kc_ref[...], f1_ref[...], tw_ref[0], tw_ref[1], w2_ref[...], _dot_hi, lambda v: v)
    o_ref[...] = x.reshape(o_ref.shape)


def _two_stage_filter_spectrum(kc_t, cs, groups_per_tile):
    n1, lanes = kc_t.shape
    wt = groups_per_tile * LANES
    ng = lanes // LANES
    return pl.pallas_call(
        _ffilt_kernel,
        grid=(lanes // wt,),
        in_specs=[pl.BlockSpec((n1, wt), lambda j: (0, j)), _full(cs["f1_full"].shape),
                  _full(cs["tw"].shape), _full(cs["w2_f32"].shape)],
        out_specs=pl.BlockSpec((groups_per_tile, n1, 2 * LANES), lambda j: (j, 0, 0)),
        out_shape=jax.ShapeDtypeStruct((ng, n1, 2 * LANES), F32),
        compiler_params=_cp("parallel"),
        name="filter_spectrum_two_stage",
    )(kc_t, cs["f1_full"], cs["tw"], cs["w2_f32"])


def _fconv_kernel(v_ref, x1_ref, x2_ref, kf_ref, bias_ref, f1_ref, tw_ref, w2_ref, w2i_ref, f1i_ref, o_ref):
    tw_r, tw_i = tw_ref[0], tw_ref[1]
    gates = (x1_ref, x2_ref)
    y = v_ref[...]
    for o in range(HY_ORDER):
        x = _fft_fwd(y, f1_ref[...], tw_r, tw_i, w2_ref[...], _dot, lambda v: v.astype(BF16))
        kf = kf_ref[o]
        kf = kf.reshape(kf.shape[0] * kf.shape[1], kf.shape[2])
        xr, xi = x[:, :LANES], x[:, LANES:]
        kr, ki = kf[:, :LANES], kf[:, LANES:]
        prod = jnp.concatenate([xr * kr - xi * ki, xr * ki + xi * kr], axis=1)
        conv = _fft_inv(prod, w2i_ref[...], tw_r, tw_i, f1i_ref[...])
        y = gates[o][...] * (conv + y * bias_ref[o:o + 1, :])
    o_ref[...] = y


def _hyena_two_stage(hs_t, kf, bias_t, cs, width):
    b, a_in, _ = hs_t.shape
    cg = min(width, 32)
    wt = cg * FFT_B
    gpt = wt // LANES
    nt = width // cg
    n1 = kf.shape[2]
    consts = [cs["f1"], cs["tw"], cs["w2"], cs["w2i"], cs["f1_inv"]]
    return pl.pallas_call(
        _fconv_kernel,
        grid=(nt, b),
        in_specs=[
            pl.BlockSpec((None, a_in, wt), lambda j, bi: (bi, 0, j)),
            pl.BlockSpec((None, a_in, wt), lambda j, bi: (bi, 0, nt + j)),
            pl.BlockSpec((None, a_in, wt), lambda j, bi: (bi, 0, 2 * nt + j)),
            pl.BlockSpec((HY_ORDER, gpt, n1, 2 * LANES), lambda j, bi: (0, j, 0, 0)),
            pl.BlockSpec((HY_ORDER, wt), lambda j, bi: (0, j)),
        ] + [_full(c.shape) for c in consts],
        out_specs=pl.BlockSpec((None, a_in, wt), lambda j, bi: (bi, 0, j)),
        out_shape=jax.ShapeDtypeStruct((b, a_in, width * FFT_B), F32),
        compiler_params=_cp("parallel", "parallel"),
        name="hyena_conv_two_stage",
    )(hs_t, hs_t, hs_t, kf, bias_t, *consts)


def _hyena(hy_in, lp):
    b, l, c3 = hy_in.shape
    width = c3 // (HY_ORDER + 1)
    hs = _shortconv(hy_in, lp["hy_conv_w"], lp["hy_conv_b"])
    kc = _hyena_filter(l, lp["hy_w1"], lp["hy_b1"], lp["hy_w2"], lp["hy_b2"], lp["hy_w3"], lp["hy_decay"])
    n = 2 * l
    if l <= DIRECT_DFT_MAX_L:
        fwd_full, fwd, inv = _direct_consts(l)
        ct = _tile(width, 256)
        kf = _direct_filter_spectrum(kc, fwd_full)
        kf = kf.reshape(2 * n, HY_ORDER, width // ct, ct).transpose(0, 2, 1, 3).reshape(2 * n, HY_ORDER * width)
        return _hyena_direct(hs, kf, lp["hy_bias"], fwd, inv)
    cs = _two_stage_consts(l)
    n1 = n // FFT_B
    a_in = l // FFT_B
    ow = HY_ORDER * width
    kc_t = kc.reshape(n1, FFT_B, ow).transpose(0, 2, 1).reshape(n1, ow * FFT_B)
    kf = _two_stage_filter_spectrum(kc_t, cs, min(ow * FFT_B // LANES, 16))
    kf = kf.reshape(HY_ORDER, width * FFT_B // LANES, n1, 2 * LANES)
    hs_t = hs.reshape(b, a_in, FFT_B, c3).transpose(0, 1, 3, 2).reshape(b, a_in, c3 * FFT_B)
    bias_t = jnp.repeat(lp["hy_bias"], FFT_B, axis=1)
    hy_t = _hyena_two_stage(hs_t, kf, bias_t, cs, width)
    return hy_t.reshape(b, a_in, width, FFT_B).transpose(0, 1, 3, 2).reshape(b, l, width).astype(BF16)


def _merge_kernel(x_ref, o_ref, hy_ref, ga_ref, gb_ref, mod_ref, g2_ref, woa_ref, wob_ref, wout_ref,
                  wpq_ref, keys_ref, xn_out, h2_out, s_out):
    mod = mod_ref[...]
    merged = ga_ref[...] * _dot(o_ref[...], woa_ref[...]) + gb_ref[...] * _dot(hy_ref[...], wob_ref[...])
    xn = x_ref[...] + mod[2:3] * _dot(merged.astype(BF16), wout_ref[...])
    xn_out[...] = xn
    h2 = (_rms(xn, g2_ref[...]) * (1.0 + mod[4:5]) + mod[3:4]).astype(BF16)
    h2_out[...] = h2
    q = _dot(h2, wpq_ref[...]).astype(BF16)
    half = keys_ref.shape[2]
    for hp in range(keys_ref.shape[0]):
        s_out[hp] = _dot_nt(keys_ref[hp], q[:, hp * half:(hp + 1) * half])


def _merge(x, attn, hy, ga, gb, mod, lw):
    b, l, d = x.shape
    tm = min(l, 256)
    pb = 1 if mod.shape[0] > 1 else 0
    tok = lambda bi, i: (bi, i, 0)
    w_names = ["g2", "woa", "wob", "wout", "wpq", "keys"]
    nhp = lw["keys"].shape[0]
    return pl.pallas_call(
        _merge_kernel,
        grid=(b, l // tm),
        in_specs=[
            pl.BlockSpec((None, tm, d), tok),
            pl.BlockSpec((None, tm, attn.shape[2]), tok),
            pl.BlockSpec((None, tm, hy.shape[2]), tok),
            pl.BlockSpec((None, tm, d), tok),
            pl.BlockSpec((None, tm, d), tok),
            pl.BlockSpec((None, 6, d), lambda bi, i: (bi * pb, 0, 0)),
        ] + [_full(lw[n].shape) for n in w_names],
        out_specs=[
            pl.BlockSpec((None, tm, d), tok),
            pl.BlockSpec((None, tm, d), tok),
            pl.BlockSpec((None, nhp, N_KEYS, tm), lambda bi, i: (bi, 0, 0, i)),
        ],
        out_shape=[
            jax.ShapeDtypeStruct((b, l, d), F32),
            jax.ShapeDtypeStruct((b, l, d), BF16),
            jax.ShapeDtypeStruct((b, nhp, N_KEYS, l), F32),
        ],
        compiler_params=_cp("parallel", "parallel"),
        name="merge_peer_query",
    )(x, attn, hy, ga, gb, mod, *[lw[n] for n in w_names])


def _top_rows(s, k, payloads=()):
    r = s.shape[0]
    rid = lax.broadcasted_iota(jnp.int32, s.shape, 0).astype(F32)
    vals, idxs, picked = [], [], [[] for _ in payloads]
    for _ in range(k):
        m = jnp.max(s, axis=0, keepdims=True)
        ix = jnp.min(jnp.where(s == m, rid, float(r)), axis=0, keepdims=True)
        hit = rid == ix
        vals.append(m)
        idxs.append(ix)
        for p, acc in zip(payloads, picked):
            acc.append(jnp.sum(jnp.where(hit, p, 0.0), axis=0, keepdims=True))
        s = jnp.where(hit, -jnp.inf, s)
    cat = lambda rows: jnp.concatenate(rows, axis=0)
    return cat(vals), cat(idxs), [cat(a) for a in picked]


def _staircase(v1, r1, v2, r2):
    k = v1.shape[0]
    r1 = r1 * float(N_KEYS)
    cand, expert = [], []
    for a in range(k // 2):
        nb = k // (a + 1)
        rows = -(-nb // SUBLANES) * SUBLANES
        c = v1[a:a + 1] + v2[:rows]
        if nb < rows:
            c = jnp.where(lax.broadcasted_iota(jnp.int32, c.shape, 0) < nb, c, -jnp.inf)
        cand.append(c)
        expert.append(r1[a:a + 1] + r2[:rows])
    cand.append(v1[k // 2:] + v2[0:1])
    expert.append(r1[k // 2:] + r2[0:1])
    return jnp.concatenate(cand, axis=0), jnp.concatenate(expert, axis=0)


def _topk_kernel(s_ref, i1_out, i2_out, g_out, i1_sc, i2_sc, g_sc):
    def head(h, carry):
        v1, r1, _ = _top_rows(s_ref[2 * h], P_TOPK)
        v2, r2, _ = _top_rows(s_ref[2 * h + 1], P_TOPK)
        cand, expert = _staircase(v1, r1, v2, r2)
        top, _, (ex,) = _top_rows(cand, P_TOPK, (expert,))
        e = jnp.exp(top - top[0:1])
        k1 = jnp.floor(ex * (1.0 / N_KEYS))
        rows = pl.ds(pl.multiple_of(h * P_TOPK, P_TOPK), P_TOPK)
        i1_sc[rows, :] = k1
        i2_sc[rows, :] = ex - k1 * float(N_KEYS)
        g_sc[rows, :] = e / jnp.sum(e, axis=0, keepdims=True)
        return carry

    lax.fori_loop(0, P_HEADS, head, 0)
    i1_out[...] = i1_sc[...].T
    i2_out[...] = i2_sc[...].T
    g_out[...] = g_sc[...].T


def _peer_topk(s_t):
    b, nhp, nk, l = s_t.shape
    tt = min(l, 256)
    hk = P_HEADS * P_TOPK
    return pl.pallas_call(
        _topk_kernel,
        grid=(b, l // tt),
        in_specs=[pl.BlockSpec((None, nhp, nk, tt), lambda bi, i: (bi, 0, 0, i))],
        out_specs=[pl.BlockSpec((None, tt, hk), lambda bi, i: (bi, i, 0))] * 3,
        out_shape=[jax.ShapeDtypeStruct((b, l, hk), F32)] * 3,
        scratch_shapes=[pltpu.VMEM((hk, tt), F32)] * 3,
        compiler_params=_cp("parallel", "parallel"),
        name="peer_topk",
    )(s_t)


BF16_BITS = 0xFFFF0000
BF16_ROUND = 0x00008000
DENSE_TOKEN_UNROLL = 32


def _dense_kernel(h_ref, i1_ref, i2_ref, g_ref, u_ref, v_ref, x_ref, mod_ref, o_ref, ws_ref, acc_ref, *, pitch):
    j = pl.program_id(2)
    tm = h_ref.shape[0]
    r = u_ref.shape[0] // N_KEYS
    half = N_KEYS // 2

    @pl.when(j == 0)
    def _():
        acc_ref[...] = jnp.zeros_like(acc_ref)
        sub = lax.broadcasted_iota(jnp.int32, (N_KEYS, i1_ref.shape[1]), 0).astype(F32).astype(BF16)
        one, zero = jnp.ones((), BF16), jnp.zeros((), BF16)

        def toks(tb, carry):
            for u in range(0, DENSE_TOKEN_UNROLL, 2):
                t = tb * DENSE_TOKEN_UNROLL + u
                pts, qts = [], []
                for k in range(2):
                    row = pl.ds(t + k, 1)
                    pts.append(jnp.where(i1_ref[row, :].astype(BF16) == sub, g_ref[row, :].astype(BF16), zero))
                    qts.append(jnp.where(i2_ref[row, :].astype(BF16) == sub, one, zero))
                z = jnp.zeros_like(qts[0])
                rhs = jnp.concatenate([jnp.concatenate([qts[0], z], axis=1),
                                       jnp.concatenate([z, qts[1]], axis=1)], axis=0)
                bits = lax.bitcast_convert_type(_dot_nt(jnp.concatenate(pts, axis=1), rhs), jnp.uint32)
                bits = bits + jnp.uint32(BF16_ROUND)
                for k in range(2):
                    bk = bits[:, k * N_KEYS:(k + 1) * N_KEYS]
                    word = (bk[:half] & jnp.uint32(BF16_BITS)) | (bk[half:] >> 16)
                    ws_ref[pl.ds(t + k, half, stride=pitch), :] = word
            return carry

        lax.fori_loop(0, tm // DENSE_TOKEN_UNROLL, toks, 0)

    a = jax.nn.gelu(_dot_nt(h_ref[...], u_ref[...]))
    first = j * r
    shift = jnp.where(first >= half, 0, 16).astype(jnp.uint32)
    base = lax.rem(first, half)
    cols = []
    for q in range(r):
        word = ws_ref[pl.ds(pl.multiple_of((base + q) * pitch, SUBLANES), tm), :]
        cols.append(lax.bitcast_convert_type(((word >> shift) << 16), F32))
    acc_ref[...] += _dot((a * jnp.concatenate(cols, axis=1)).astype(BF16), v_ref[...])

    @pl.when(j == pl.num_programs(2) - 1)
    def _():
        o_ref[...] = x_ref[...] + mod_ref[5:6, :] * acc_ref[...]


def _peer_dense(h2, i1, i2, g, u_tab, v_tab, xn, mod):
    b, l, d = h2.shape
    tm = min(l, 512)
    te = 1024
    ne = u_tab.shape[0]
    hk = i1.shape[2]
    pitch = tm + SUBLANES
    pb = 1 if mod.shape[0] > 1 else 0
    assert tm % DENSE_TOKEN_UNROLL == 0 and (N_KEYS // 2) % (te // N_KEYS) == 0
    tok = lambda bi, i, j: (bi, i, 0)
    return pl.pallas_call(
        functools.partial(_dense_kernel, pitch=pitch),
        grid=(b, l // tm, ne // te),
        in_specs=[
            pl.BlockSpec((None, tm, d), tok),
            pl.BlockSpec((None, tm, hk), tok),
            pl.BlockSpec((None, tm, hk), tok),
            pl.BlockSpec((None, tm, hk), tok),
            pl.BlockSpec((te, d), lambda bi, i, j: (j, 0)),
            pl.BlockSpec((te, d), lambda bi, i, j: (j, 0)),
            pl.BlockSpec((None, tm, d), tok, pipeline_mode=pl.Buffered(1)),
            pl.BlockSpec((None, 6, d), lambda bi, i, j: (bi * pb, 0, 0)),
        ],
        out_specs=pl.BlockSpec((None, tm, d), tok),
        out_shape=jax.ShapeDtypeStruct((b, l, d), F32),
        scratch_shapes=[pltpu.VMEM((N_KEYS // 2 * pitch, N_KEYS), jnp.uint32), pltpu.VMEM((tm, d), F32)],
        compiler_params=pltpu.CompilerParams(dimension_semantics=("parallel", "parallel", "arbitrary"),
                                             vmem_limit_bytes=DENSE_VMEM_LIMIT_BYTES),
        name="peer_dense",
    )(h2, i1, i2, g, u_tab, v_tab, xn, mod)


def _final_kernel(x_ref, g_ref, o_ref):
    o_ref[...] = _rms(x_ref[...], g_ref[...])


def _final_norm(x, g):
    b, l, d = x.shape
    tm = min(l, 512)
    return pl.pallas_call(
        _final_kernel,
        grid=(b, l // tm),
        in_specs=[pl.BlockSpec((None, tm, d), lambda bi, i: (bi, i, 0)), _full((1, d))],
        out_specs=pl.BlockSpec((None, tm, d), lambda bi, i: (bi, i, 0)),
        out_shape=jax.ShapeDtypeStruct((b, l, d), F32),
        compiler_params=_cp("parallel", "parallel"),
        name="final_norm",
    )(x, g.reshape(1, d))


def _rope_tables(n_tok):
    rows = n_tok // GRID_W
    row = jnp.repeat(jnp.arange(rows, dtype=F32), GRID_W)
    col = jnp.tile(jnp.arange(GRID_W, dtype=F32), rows)
    n_freq = QK_ROPE // 4
    inv = 1.0 / (ROPE_BASE ** (jnp.arange(n_freq, dtype=F32) / n_freq))
    ang = jnp.concatenate([row[:, None] * inv, col[:, None] * inv], axis=-1)
    cos, sin = jnp.cos(ang), jnp.sin(ang)
    half = QK_ROPE // 2
    pad = LANES - QK_NOPE - QK_ROPE
    ones = jnp.ones((n_tok, QK_NOPE), F32)
    z = lambda w: jnp.zeros((n_tok, w), F32)
    c = jnp.concatenate([ones, cos, cos, z(pad)], axis=1)
    s1 = jnp.concatenate([z(QK_NOPE), -sin, z(half), z(pad)], axis=1)
    s2 = jnp.concatenate([z(QK_NOPE), z(half), sin, z(pad)], axis=1)
    return c, s1, s2


def _pad_heads(w, per_head):
    rows = w.shape[0]
    w = w.reshape(rows, N_HEADS, per_head)
    return jnp.pad(w, ((0, 0), (0, 0), (0, LANES - per_head))).reshape(rows, N_HEADS * LANES)


def _layer_weights(p, l):
    d = p["w_in"].shape[1]
    q_rank = p["q_norm"].shape[1]
    kv_rank = p["kv_norm"].shape[1]
    hy3 = p["hy_conv_w"].shape[2]
    w_in = p["w_in"][l]
    c0, c1, c2, c3 = q_rank, q_rank + kv_rank, q_rank + kv_rank + QK_ROPE, q_rank + kv_rank + QK_ROPE + hy3
    wkr = jnp.zeros((d, LANES), F32).at[:, QK_NOPE:QK_NOPE + QK_ROPE].set(w_in[:, c1:c2])
    ukv = p["w_ukv"][l].reshape(kv_rank, N_HEADS, QK_NOPE + V_DIM)
    wuk = _pad_heads(ukv[:, :, :QK_NOPE].reshape(kv_rank, -1), QK_NOPE)
    wuv = _pad_heads(ukv[:, :, QK_NOPE:].reshape(kv_rank, -1), V_DIM)
    woa = p["w_oa"][l].reshape(N_HEADS, V_DIM, d)
    woa = jnp.pad(woa, ((0, 0), (0, LANES - V_DIM), (0, 0))).reshape(N_HEADS * LANES, d)
    keys = p["peer_keys"][l]
    row = lambda v: v.reshape(1, -1)
    return dict(
        gn=row(p["norm_mix"][l]), wq=w_in[:, :c0].astype(BF16), wckv=w_in[:, c0:c1].astype(BF16),
        wkr=wkr.astype(BF16), why=w_in[:, c2:c3].astype(BF16), wg=w_in[:, c3:].astype(BF16),
        qn=row(p["q_norm"][l]), kvn=row(p["kv_norm"][l]),
        wuq=_pad_heads(p["w_uq"][l], QK_NOPE + QK_ROPE).astype(BF16),
        wuk=wuk.astype(BF16), wuv=wuv.astype(BF16),
        g2=row(p["norm_ffn"][l]), woa=woa.astype(BF16), wob=p["w_ob"][l].astype(BF16),
        wout=p["w_out"][l].astype(BF16), wpq=p["peer_wq"][l].astype(BF16),
        keys=keys.reshape(keys.shape[0] * keys.shape[1], keys.shape[2], keys.shape[3]).astype(BF16),
        u=p["peer_u"][l].astype(BF16), v=p["peer_v"][l].astype(BF16),
    )


def _layer(x, mod, lw, lp, rope_tabs, ctx_kv):
    q, k, cb, ckv, kr, hy_in, ga, gb = _inproj(x, mod, lw, rope_tabs)
    if ctx_kv is None:
        attn = _attention(q, k, cb, lw["wuv"])
    else:
        attn = _attention(q, k, cb, lw["wuv"], *ctx_kv)
    hy = _hyena(hy_in, lp)
    xn, h2, s_t = _merge(x, attn, hy, ga, gb, mod, lw)
    i1, i2, g = _peer_topk(s_t)
    x = _peer_dense(h2, i1, i2, g, lw["u"], lw["v"], xn, mod)
    return x, ckv, kr[:, :, QK_NOPE:QK_NOPE + QK_ROPE]


def kernel(x_prompt, x_sample, cache_ckv, cache_krope, c, c_ctx, w_ada, b_ada, norm_mix, norm_ffn,
           w_in, q_norm, kv_norm, w_uq, w_ukv, w_oa, w_ob, w_out, hy_conv_w, hy_conv_b,
           hy_w1, hy_b1, hy_w2, hy_b2, hy_w3, hy_decay, hy_bias,
           peer_wq, peer_keys, peer_u, peer_v, final_norm):
    p = dict(norm_mix=norm_mix, norm_ffn=norm_ffn, w_in=w_in, q_norm=q_norm, kv_norm=kv_norm, w_uq=w_uq,
             w_ukv=w_ukv, w_oa=w_oa, w_ob=w_ob, w_out=w_out, peer_wq=peer_wq, peer_keys=peer_keys,
             peer_u=peer_u, peer_v=peer_v, hy_conv_w=hy_conv_w)
    depth, d = norm_mix.shape
    bd = x_sample.shape[0]
    rows = -(-(bd + 1) // SUBLANES) * SUBLANES
    cc = jnp.zeros((rows, d), F32).at[:bd].set(c).at[bd].set(c_ctx)
    mods = _adaln(cc, w_ada, b_ada).reshape(depth, rows, 6, d)
    rope_tabs = _rope_tables(x_sample.shape[1])
    e_mat = jnp.zeros((QK_ROPE, N_HEADS, LANES), F32)
    e_mat = e_mat.at[:, :, QK_NOPE:QK_NOPE + QK_ROPE].set(jnp.eye(QK_ROPE, dtype=F32)[:, None, :])
    e_mat = e_mat.reshape(QK_ROPE, N_HEADS * LANES).astype(BF16)
    xp, xs = x_prompt, x_sample
    new_ckv, new_kr = [], []
    for l in range(depth):
        lw = _layer_weights(p, l)
        lp = dict(hy_conv_w=hy_conv_w[l], hy_conv_b=hy_conv_b[l], hy_w1=hy_w1[l], hy_b1=hy_b1[l],
                  hy_w2=hy_w2[l], hy_b2=hy_b2[l], hy_w3=hy_w3[l], hy_decay=hy_decay[l], hy_bias=hy_bias[l])
        xp, ckv_p, kr_p = _layer(xp, mods[l, bd:bd + 1], lw, lp, None, None)
        new_ckv.append(ckv_p)
        new_kr.append(kr_p)
        ctx_kv = _kvcache(cache_ckv, cache_krope, l, lw["wuk"], e_mat)
        xs, _, _ = _layer(xs, mods[l, :bd], lw, lp, rope_tabs, ctx_kv)
    return (_final_norm(xp, final_norm), _final_norm(xs, final_norm),
            jnp.stack(new_ckv, axis=1), jnp.stack(new_kr, axis=1))
```

```python
import functools
import math

import numpy as np
import jax
import jax.numpy as jnp
from jax import lax
from jax.experimental import pallas as pl
from jax.experimental.pallas import tpu as pltpu

F32 = jnp.float32
BF16 = jnp.bfloat16

GRID_W = 64
N_HEADS = 8
QK_NOPE = 64
QK_ROPE = 32
V_DIM = 64
ROPE_BASE = 10000.0
HY_ORDER = 2
HY_BANDS = 16
SHORT_K = 3
P_HEADS = 8
N_KEYS = 128
P_TOPK = 16
EPS = 1e-6

LANES = 128
SUBLANES = 8
FFT_B = 64
DIRECT_DFT_MAX_L = 256
VMEM_LIMIT_BYTES = 48 * 1024 * 1024
DENSE_VMEM_LIMIT_BYTES = 56 * 1024 * 1024


def _cp(*sem):
    return pltpu.CompilerParams(dimension_semantics=sem, vmem_limit_bytes=VMEM_LIMIT_BYTES)


def _full(shape):
    n = len(shape)
    return pl.BlockSpec(shape, lambda *_: (0,) * n)


def _tile(n, pref, unit=LANES):
    if n <= pref:
        return n
    return max(t for t in range(unit, pref + 1, unit) if n % t == 0)


def _dot(a, b):
    return jnp.dot(a, b, preferred_element_type=F32)


def _dot_hi(a, b):
    return jnp.dot(a, b, precision=lax.Precision.HIGHEST, preferred_element_type=F32)


def _dot_nt(a, b):
    return lax.dot_general(a, b, (((1,), (1,)), ((), ())), preferred_element_type=F32)


def _rms(x, g):
    return x * lax.rsqrt(jnp.mean(x * x, axis=-1, keepdims=True) + EPS) * g


def _adaln_kernel(c_ref, w_ref, b_ref, o_ref):
    c = c_ref[...]
    s = (c * jax.nn.sigmoid(c)).astype(BF16)
    o_ref[...] = _dot(s, w_ref[...].astype(BF16)) + b_ref[...]


def _adaln(cc, w_ada, b_ada):
    depth, d, d6 = w_ada.shape
    rows = cc.shape[0]
    tn = d6 // 4
    return pl.pallas_call(
        _adaln_kernel,
        grid=(depth, d6 // tn),
        in_specs=[
            _full((rows, d)),
            pl.BlockSpec((None, d, tn), lambda l, j: (l, 0, j)),
            pl.BlockSpec((None, 1, tn), lambda l, j: (l, 0, j)),
        ],
        out_specs=pl.BlockSpec((None, rows, tn), lambda l, j: (l, 0, j)),
        out_shape=jax.ShapeDtypeStruct((depth, rows, d6), F32),
        compiler_params=_cp("parallel", "parallel"),
        name="adaln",
    )(cc, w_ada, b_ada.reshape(depth, 1, d6))


ATTN_SCALE = 1.0 / math.sqrt(QK_NOPE + QK_ROPE)
ATTN_KEY_CHUNK = 1024
ATTN_HEADS_PER_STEP = 4


def _rope_slab(x, c, s1, s2):
    half = QK_ROPE // 2
    return x * c + pltpu.roll(x, LANES - half, 1) * s1 + pltpu.roll(x, half, 1) * s2


def _inproj_kernel(*refs, rope):
    if rope:
        (x_ref, mod_ref, gn_ref, wq_ref, wckv_ref, wkr_ref, why_ref, wg_ref, qn_ref, kvn_ref,
         wuq_ref, wuk_ref, c_ref, s1_ref, s2_ref,
         q_out, k_out, cb_out, ckv_out, kr_out, hy_out, ga_out, gb_out) = refs
    else:
        (x_ref, mod_ref, gn_ref, wq_ref, wckv_ref, wkr_ref, why_ref, wg_ref, qn_ref, kvn_ref,
         wuq_ref, wuk_ref,
         q_out, k_out, cb_out, ckv_out, kr_out, hy_out, ga_out, gb_out) = refs
    x = x_ref[...]
    mod = mod_ref[...]
    d = x.shape[1]
    h = (_rms(x, gn_ref[...]) * (1.0 + mod[1:2]) + mod[0:1]).astype(BF16)
    q = _dot(_rms(_dot(h, wq_ref[...]), qn_ref[...]).astype(BF16), wuq_ref[...])
    ckv = _rms(_dot(h, wckv_ref[...]), kvn_ref[...])
    kr = _dot(h, wkr_ref[...])
    ckv_b = ckv.astype(BF16)
    kn = _dot(ckv_b, wuk_ref[...])
    if rope:
        c, s1, s2 = c_ref[...], s1_ref[...], s2_ref[...]
        kr_k = _rope_slab(kr, c, s1, s2)
        q = jnp.concatenate(
            [_rope_slab(q[:, i * LANES:(i + 1) * LANES], c, s1, s2) for i in range(N_HEADS)], axis=1)
    else:
        kr_k = kr
    q_out[...] = (q * ATTN_SCALE).astype(BF16)
    k_out[...] = (kn + jnp.concatenate([kr_k] * N_HEADS, axis=1)).astype(BF16)
    cb_out[...] = ckv_b
    ckv_out[...] = ckv
    kr_out[...] = kr
    hy_out[...] = _dot(h, why_ref[...])
    g = _dot(h, wg_ref[...])
    ga_out[...] = jax.nn.sigmoid(g[:, :d])
    gb_out[...] = jax.nn.sigmoid(g[:, d:])


def _inproj(x, mod, lw, rope_tabs):
    b, l, d = x.shape
    tm = min(l, 256)
    pb = 1 if mod.shape[0] > 1 else 0
    hl = N_HEADS * LANES
    kvr = lw["wckv"].shape[1]
    hyw = lw["why"].shape[1]
    rope = rope_tabs is not None
    tok = lambda bi, i: (bi, i, 0)
    w_names = ["gn", "wq", "wckv", "wkr", "why", "wg", "qn", "kvn", "wuq", "wuk"]
    ins = [x, mod] + [lw[n] for n in w_names]
    in_specs = [pl.BlockSpec((None, tm, d), tok), pl.BlockSpec((None, 6, d), lambda bi, i: (bi * pb, 0, 0))]
    in_specs += [_full(lw[n].shape) for n in w_names]
    if rope:
        ins += list(rope_tabs)
        in_specs += [pl.BlockSpec((tm, LANES), lambda bi, i: (i, 0))] * 3
    widths = [(hl, BF16), (hl, BF16), (kvr, BF16), (kvr, F32), (LANES, F32), (hyw, F32), (d, F32), (d, F32)]
    return pl.pallas_call(
        functools.partial(_inproj_kernel, rope=rope),
        grid=(b, l // tm),
        in_specs=in_specs,
        out_specs=[pl.BlockSpec((None, tm, w), tok) for w, _ in widths],
        out_shape=[jax.ShapeDtypeStruct((b, l, w), dt) for w, dt in widths],
        compiler_params=_cp("parallel", "parallel"),
        name="inproj_rope" if rope else "inproj",
    )(*ins)


def _kvcache_kernel(ckv_ref, kr_ref, wuk_ref, e_ref, k_out, cb_out):
    ckv_b = ckv_ref[...].astype(BF16)
    k_out[...] = (_dot(ckv_b, wuk_ref[...]) + _dot(kr_ref[...].astype(BF16), e_ref[...])).astype(BF16)
    cb_out[...] = ckv_b


def _kvcache(cache_ckv, cache_kr, layer, wuk, e_mat):
    b, _, p, kvr = cache_ckv.shape
    hl = N_HEADS * LANES
    return pl.pallas_call(
        _kvcache_kernel,
        grid=(b,),
        in_specs=[
            pl.BlockSpec((None, None, p, kvr), lambda bi: (bi, layer, 0, 0)),
            pl.BlockSpec((None, None, p, QK_ROPE), lambda bi: (bi, layer, 0, 0)),
            _full(wuk.shape),
            _full(e_mat.shape),
        ],
        out_specs=[pl.BlockSpec((None, p, hl), lambda bi: (bi, 0, 0)),
                   pl.BlockSpec((None, p, kvr), lambda bi: (bi, 0, 0))],
        out_shape=[jax.ShapeDtypeStruct((b, p, hl), BF16), jax.ShapeDtypeStruct((b, p, kvr), BF16)],
        compiler_params=_cp("parallel"),
        name="kvcache",
    )(cache_ckv, cache_kr, wuk, e_mat)


def _attn_kernel(*refs, ctx):
    if ctx:
        q_ref, k_ref, c_ref, wuv_ref, kc_ref, cc_ref, o_ref = refs
    else:
        q_ref, k_ref, c_ref, wuv_ref, o_ref = refs
    chunks = [(kc_ref, cc_ref, 0, kc_ref.shape[0])] if ctx else []
    n_keys = k_ref.shape[0]
    ck = min(n_keys, ATTN_KEY_CHUNK)
    chunks += [(k_ref, c_ref, c * ck, ck) for c in range(n_keys // ck)]
    heads = [slice(i * LANES, (i + 1) * LANES) for i in range(ATTN_HEADS_PER_STEP)]
    qs = [q_ref[:, h] for h in heads]
    m = [None] * len(heads)
    acc = [None] * len(heads)
    den = [None] * len(heads)
    for kr, cr, start, size in chunks:
        for i, h in enumerate(heads):
            s = _dot_nt(qs[i], kr[pl.ds(start, size), h])
            cmax = jnp.max(s, axis=1, keepdims=True)
            if m[i] is None:
                m[i] = cmax
                p = jnp.exp(s - cmax)
                den[i] = jnp.sum(p, axis=1, keepdims=True)
                acc[i] = _dot(p.astype(BF16), cr[pl.ds(start, size), :])
            else:
                m_new = jnp.maximum(m[i], cmax)
                alpha = jnp.exp(m[i] - m_new)
                p = jnp.exp(s - m_new)
                den[i] = den[i] * alpha + jnp.sum(p, axis=1, keepdims=True)
                acc[i] = acc[i] * alpha + _dot(p.astype(BF16), cr[pl.ds(start, size), :])
                m[i] = m_new
    for i, h in enumerate(heads):
        o_ref[:, h] = _dot((acc[i] / den[i]).astype(BF16), wuv_ref[:, h]).astype(BF16)


def _attention(q, k, cb, wuv, kc=None, cc=None):
    b, l, hl = q.shape
    kvr = cb.shape[2]
    tq = min(l, 512)
    ctx = kc is not None
    hw = ATTN_HEADS_PER_STEP * LANES
    ins = [q, k, cb, wuv]
    in_specs = [
        pl.BlockSpec((None, tq, hw), lambda bi, h, i: (bi, i, h)),
        pl.BlockSpec((None, l, hw), lambda bi, h, i: (bi, 0, h)),
        pl.BlockSpec((None, l, kvr), lambda bi, h, i: (bi, 0, 0)),
        pl.BlockSpec((kvr, hw), lambda bi, h, i: (0, h)),
    ]
    if ctx:
        p = kc.shape[1]
        ins += [kc, cc]
        in_specs += [pl.BlockSpec((None, p, hw), lambda bi, h, i: (bi, 0, h)),
                     pl.BlockSpec((None, p, kvr), lambda bi, h, i: (bi, 0, 0))]
    return pl.pallas_call(
        functools.partial(_attn_kernel, ctx=ctx),
        grid=(b, N_HEADS // ATTN_HEADS_PER_STEP, l // tq),
        in_specs=in_specs,
        out_specs=pl.BlockSpec((None, tq, hw), lambda bi, h, i: (bi, i, h)),
        out_shape=jax.ShapeDtypeStruct((b, l, hl), BF16),
        compiler_params=_cp("parallel", "parallel", "parallel"),
        name="attn_ctx" if ctx else "attn",
    )(*ins)


def _shortconv_kernel(x_ref, w_ref, b_ref, o_ref):
    x = x_ref[...]
    n = x.shape[0]
    row = lax.broadcasted_iota(jnp.int32, x.shape, 0)
    xm = jnp.where(row == 0, 0.0, pltpu.roll(x, 1, 0))
    xp = jnp.where(row == n - 1, 0.0, pltpu.roll(x, n - 1, 0))
    w = w_ref[...]
    o_ref[...] = (xm * w[0:1] + x * w[1:2] + xp * w[2:3] + b_ref[...]).astype(BF16)


def _shortconv(x, w, bias):
    b, l, c = x.shape
    ct = _tile(c, 256)
    return pl.pallas_call(
        _shortconv_kernel,
        grid=(b, c // ct),
        in_specs=[
            pl.BlockSpec((None, l, ct), lambda bi, j: (bi, 0, j)),
            pl.BlockSpec((SHORT_K, ct), lambda bi, j: (0, j)),
            pl.BlockSpec((1, ct), lambda bi, j: (0, j)),
        ],
        out_specs=pl.BlockSpec((None, l, ct), lambda bi, j: (bi, 0, j)),
        out_shape=jax.ShapeDtypeStruct((b, l, c), BF16),
        compiler_params=_cp("parallel", "parallel"),
        name="shortconv",
    )(x, w, bias.reshape(1, c))


def _filter_kernel(w1_ref, b1_ref, w2_ref, b2_ref, w3_ref, dec_ref, o_ref, *, seq, width):
    tr = o_ref.shape[0]
    n = 2 * seq
    d = lax.broadcasted_iota(jnp.int32, (tr, LANES), 0) + pl.program_id(0) * tr
    pos = jnp.where(d <= seq, d, n - d)
    t = pos.astype(F32) / seq
    lane = lax.broadcasted_iota(jnp.int32, (tr, LANES), 1)
    band = jnp.where(lane <= HY_BANDS, lane, lane - HY_BANDS).astype(F32)
    ang = (2.0 * math.pi * t) * band
    z = jnp.where(lane == 0, t,
                  jnp.where(lane <= HY_BANDS, jnp.sin(ang),
                            jnp.where(lane <= 2 * HY_BANDS, jnp.cos(ang), 0.0)))
    f = jnp.sin(_dot_hi(z, w1_ref[...]) + b1_ref[...])
    f = jnp.sin(_dot_hi(f, w2_ref[...]) + b2_ref[...])
    hw = _dot_hi(f, w3_ref[...]) * jnp.exp(-dec_ref[...] * t[:, :1])
    dcol = d[:, :1]
    mf = jnp.where(dcol < seq, 1.0, 0.0)
    mb = jnp.where((dcol == 0) | (dcol > seq), 1.0, 0.0)
    for o in range(HY_ORDER):
        fwd = hw[:, (2 * o) * width:(2 * o + 1) * width]
        bwd = hw[:, (2 * o + 1) * width:(2 * o + 2) * width]
        o_ref[:, o * width:(o + 1) * width] = mf * fwd + mb * bwd


def _hyena_filter(seq, w1, b1, w2, b2, w3, decay):
    width = decay.shape[-1]
    n = 2 * seq
    tr = min(n, 512)
    hid = w2.shape[0]
    w1p = jnp.zeros((LANES, hid), F32).at[:w1.shape[0]].set(w1)
    ins = [w1p, b1.reshape(1, hid), w2, b2.reshape(1, hid), w3, decay.reshape(1, -1)]
    return pl.pallas_call(
        functools.partial(_filter_kernel, seq=seq, width=width),
        grid=(n // tr,),
        in_specs=[_full(a.shape) for a in ins],
        out_specs=pl.BlockSpec((tr, HY_ORDER * width), lambda i: (i, 0)),
        out_shape=jax.ShapeDtypeStruct((n, HY_ORDER * width), F32),
        compiler_params=_cp("parallel"),
        name="hyena_filter",
    )(*ins)


def _direct_consts(seq):
    n = 2 * seq
    k = np.arange(n, dtype=np.float64)[:, None]
    pos = np.arange(n, dtype=np.float64)[None, :]
    ang = 2.0 * np.pi * k * pos / n
    fwd = np.concatenate([np.cos(ang), -np.sin(ang)], axis=0)
    inv = np.concatenate([np.cos(ang.T), -np.sin(ang.T)], axis=1)[:seq] / n
    return (jnp.asarray(fwd, F32), jnp.asarray(fwd[:, :seq], BF16), jnp.asarray(inv, BF16))


def _dfilt_kernel(kc_ref, f_ref, o_ref):
    o_ref[...] = _dot_hi(f_ref[...], kc_ref[...])


def _direct_filter_spectrum(kc, fwd_full):
    n, ch = kc.shape
    ct = _tile(ch, 256)
    return pl.pallas_call(
        _dfilt_kernel,
        grid=(ch // ct,),
        in_specs=[pl.BlockSpec((n, ct), lambda j: (0, j)), _full(fwd_full.shape)],
        out_specs=pl.BlockSpec((2 * n, ct), lambda j: (0, j)),
        out_shape=jax.ShapeDtypeStruct((2 * n, ch), F32),
        compiler_params=_cp("parallel"),
        name="filter_spectrum_direct",
    )(kc, fwd_full)


def _dconv_kernel(v_ref, x1_ref, x2_ref, kf_ref, bias_ref, f_ref, fi_ref, o_ref):
    n = f_ref.shape[0] // 2
    ct = o_ref.shape[1]
    gates = (x1_ref, x2_ref)
    y = v_ref[...].astype(F32)
    for o in range(HY_ORDER):
        z = _dot(f_ref[...], y.astype(BF16))
        zr, zi = z[:n], z[n:]
        kr = kf_ref[:n, o * ct:(o + 1) * ct]
        ki = kf_ref[n:, o * ct:(o + 1) * ct]
        prod = jnp.concatenate([zr * kr - zi * ki, zr * ki + zi * kr], axis=0).astype(BF16)
        y = gates[o][...].astype(F32) * (_dot(fi_ref[...], prod) + y * bias_ref[o:o + 1, :])
    o_ref[...] = y.astype(BF16)


def _hyena_direct(hs, kf, bias, fwd, inv):
    b, l, c3 = hs.shape
    w = c3 // (HY_ORDER + 1)
    ct = _tile(w, 256)
    nt = w // ct
    return pl.pallas_call(
        _dconv_kernel,
        grid=(nt, b),
        in_specs=[
            pl.BlockSpec((None, l, ct), lambda j, bi: (bi, 0, j)),
            pl.BlockSpec((None, l, ct), lambda j, bi: (bi, 0, nt + j)),
            pl.BlockSpec((None, l, ct), lambda j, bi: (bi, 0, 2 * nt + j)),
            pl.BlockSpec((kf.shape[0], HY_ORDER * ct), lambda j, bi: (0, j)),
            pl.BlockSpec((HY_ORDER, ct), lambda j, bi: (0, j)),
            _full(fwd.shape),
            _full(inv.shape),
        ],
        out_specs=pl.BlockSpec((None, l, ct), lambda j, bi: (bi, 0, j)),
        out_shape=jax.ShapeDtypeStruct((b, l, w), BF16),
        compiler_params=_cp("parallel", "parallel"),
        name="hyena_conv_direct",
    )(hs, hs, hs, kf, bias, fwd, inv)


def _two_stage_consts(seq):
    n = 2 * seq
    n1 = n // FFT_B
    a_in = n1 // 2
    k1 = np.arange(n1, dtype=np.float64)[:, None]
    a = np.arange(n1, dtype=np.float64)[None, :]
    ang1 = 2.0 * np.pi * k1 * a / n1
    f1_full = np.concatenate([np.cos(ang1), -np.sin(ang1)], axis=0)
    f1_inv = np.concatenate([np.cos(ang1.T), -np.sin(ang1.T)], axis=1)[:a_in] / n
    bb = np.arange(FFT_B, dtype=np.float64)[None, :]
    angt = 2.0 * np.pi * k1 * bb / n
    rep = LANES // FFT_B
    tw = np.stack([np.tile(np.cos(angt), (1, rep)), np.tile(-np.sin(angt), (1, rep))])
    b_ = np.arange(FFT_B, dtype=np.float64)[:, None]
    k2 = np.arange(FFT_B, dtype=np.float64)[None, :]
    ang2 = 2.0 * np.pi * b_ * k2 / FFT_B
    bd = lambda m: np.kron(np.eye(rep), m)
    cr, ci = np.cos(ang2), -np.sin(ang2)
    w2 = np.block([[bd(cr), bd(ci)], [bd(-ci), bd(cr)]])
    er, ei = np.cos(ang2), np.sin(ang2)
    w2i = np.block([[bd(er), bd(ei)], [bd(-ei), bd(er)]])
    return dict(
        f1_full=jnp.asarray(f1_full, F32), f1=jnp.asarray(f1_full[:, :a_in], BF16),
        f1_inv=jnp.asarray(f1_inv, BF16), tw=jnp.asarray(tw, F32),
        w2=jnp.asarray(w2, BF16), w2_f32=jnp.asarray(w2, F32), w2i=jnp.asarray(w2i, BF16))


def _fft_fwd(x, f1, tw_r, tw_i, w2, dot, cast):
    n1 = tw_r.shape[0]
    z = dot(f1, cast(x))
    zr, zi = z[:n1], z[n1:]
    rows = []
    for g in range(x.shape[1] // LANES):
        a = zr[:, g * LANES:(g + 1) * LANES]
        b = zi[:, g * LANES:(g + 1) * LANES]
        rows.append(jnp.concatenate([a * tw_r - b * tw_i, a * tw_i + b * tw_r], axis=1))
    return dot(cast(jnp.concatenate(rows, axis=0)), w2)


def _fft_inv(yf, w2i, tw_r, tw_i, f1_inv):
    n1 = tw_r.shape[0]
    gm = _dot(yf.astype(BF16), w2i)
    cols = []
    for g in range(yf.shape[0] // n1):
        blk = gm[g * n1:(g + 1) * n1]
        gr, gi = blk[:, :LANES], blk[:, LANES:]
        cols.append(jnp.concatenate([gr * tw_r + gi * tw_i, gi * tw_r - gr * tw_i], axis=0))
    return _dot(f1_inv, jnp.concatenate(cols, axis=1).astype(BF16))


def _ffilt_kernel(kc_ref, f1_ref, tw_ref, w2_ref, o_ref):
    x = _fft_fwd(kc_ref[...], f1_ref[...], tw_ref[0], tw_ref[1], w2_ref[...], _dot_hi, lambda v: v)
    o_ref[...] = x.reshape(o_ref.shape)


def _two_stage_filter_spectrum(kc_t, cs, groups_per_tile):
    n1, lanes = kc_t.shape
    wt = groups_per_tile * LANES
    ng = lanes // LANES
    return pl.pallas_call(
        _ffilt_kernel,
        grid=(lanes // wt,),
        in_specs=[pl.BlockSpec((n1, wt), lambda j: (0, j)), _full(cs["f1_full"].shape),
                  _full(cs["tw"].shape), _full(cs["w2_f32"].shape)],
        out_specs=pl.BlockSpec((groups_per_tile, n1, 2 * LANES), lambda j: (j, 0, 0)),
        out_shape=jax.ShapeDtypeStruct((ng, n1, 2 * LANES), F32),
        compiler_params=_cp("parallel"),
        name="filter_spectrum_two_stage",
    )(kc_t, cs["f1_full"], cs["tw"], cs["w2_f32"])


def _fconv_kernel(v_ref, x1_ref, x2_ref, kf_ref, bias_ref, f1_ref, tw_ref, w2_ref, w2i_ref, f1i_ref, o_ref):
    tw_r, tw_i = tw_ref[0], tw_ref[1]
    gates = (x1_ref, x2_ref)
    y = v_ref[...].astype(F32)
    for o in range(HY_ORDER):
        x = _fft_fwd(y, f1_ref[...], tw_r, tw_i, w2_ref[...], _dot, lambda v: v.astype(BF16))
        kf = kf_ref[o]
        kf = kf.reshape(kf.shape[0] * kf.shape[1], kf.shape[2])
        xr, xi = x[:, :LANES], x[:, LANES:]
        kr, ki = kf[:, :LANES], kf[:, LANES:]
        prod = jnp.concatenate([xr * kr - xi * ki, xr * ki + xi * kr], axis=1)
        conv = _fft_inv(prod, w2i_ref[...], tw_r, tw_i, f1i_ref[...])
        y = gates[o][...].astype(F32) * (conv + y * bias_ref[o:o + 1, :])
    o_ref[...] = y.astype(BF16)


def _hyena_two_stage(hs_t, kf, bias_t, cs, width):
    b, a_in, _ = hs_t.shape
    cg = min(width, 32)
    wt = cg * FFT_B
    gpt = wt // LANES
    nt = width // cg
    n1 = kf.shape[2]
    consts = [cs["f1"], cs["tw"], cs["w2"], cs["w2i"], cs["f1_inv"]]
    return pl.pallas_call(
        _fconv_kernel,
        grid=(nt, b),
        in_specs=[
            pl.BlockSpec((None, a_in, wt), lambda j, bi: (bi, 0, j)),
            pl.BlockSpec((None, a_in, wt), lambda j, bi: (bi, 0, nt + j)),
            pl.BlockSpec((None, a_in, wt), lambda j, bi: (bi, 0, 2 * nt + j)),
            pl.BlockSpec((HY_ORDER, gpt, n1, 2 * LANES), lambda j, bi: (0, j, 0, 0)),
            pl.BlockSpec((HY_ORDER, wt), lambda j, bi: (0, j)),
        ] + [_full(c.shape) for c in consts],
        out_specs=pl.BlockSpec((None, a_in, wt), lambda j, bi: (bi, 0, j)),
        out_shape=jax.ShapeDtypeStruct((b, a_in, width * FFT_B), BF16),
        compiler_params=_cp("parallel", "parallel"),
        name="hyena_conv_two_stage",
    )(hs_t, hs_t, hs_t, kf, bias_t, *consts)


def _hyena(hy_in, lp):
    b, l, c3 = hy_in.shape
    width = c3 // (HY_ORDER + 1)
    hs = _shortconv(hy_in, lp["hy_conv_w"], lp["hy_conv_b"])
    kc = _hyena_filter(l, lp["hy_w1"], lp["hy_b1"], lp["hy_w2"], lp["hy_b2"], lp["hy_w3"], lp["hy_decay"])
    n = 2 * l
    if l <= DIRECT_DFT_MAX_L:
        fwd_full, fwd, inv = _direct_consts(l)
        ct = _tile(width, 256)
        kf = _direct_filter_spectrum(kc, fwd_full)
        kf = kf.reshape(2 * n, HY_ORDER, width // ct, ct).transpose(0, 2, 1, 3).reshape(2 * n, HY_ORDER * width)
        return _hyena_direct(hs, kf, lp["hy_bias"], fwd, inv)
    cs = _two_stage_consts(l)
    n1 = n // FFT_B
    a_in = l // FFT_B
    ow = HY_ORDER * width
    kc_t = kc.reshape(n1, FFT_B, ow).transpose(0, 2, 1).reshape(n1, ow * FFT_B)
    kf = _two_stage_filter_spectrum(kc_t, cs, min(ow * FFT_B // LANES, 16))
    kf = kf.reshape(HY_ORDER, width * FFT_B // LANES, n1, 2 * LANES)
    hs_t = hs.reshape(b, a_in, FFT_B, c3).transpose(0, 1, 3, 2).reshape(b, a_in, c3 * FFT_B)
    bias_t = jnp.repeat(lp["hy_bias"], FFT_B, axis=1)
    hy_t = _hyena_two_stage(hs_t, kf, bias_t, cs, width)
    return hy_t.reshape(b, a_in, width, FFT_B).transpose(0, 1, 3, 2).reshape(b, l, width)


def _merge_kernel(x_ref, o_ref, hy_ref, ga_ref, gb_ref, mod_ref, g2_ref, woa_ref, wob_ref, wout_ref,
                  wpq_ref, keys_ref, xn_out, h2_out, s_out):
    mod = mod_ref[...]
    merged = ga_ref[...] * _dot(o_ref[...], woa_ref[...]) + gb_ref[...] * _dot(hy_ref[...], wob_ref[...])
    xn = x_ref[...] + mod[2:3] * _dot(merged.astype(BF16), wout_ref[...])
    xn_out[...] = xn
    h2 = (_rms(xn, g2_ref[...]) * (1.0 + mod[4:5]) + mod[3:4]).astype(BF16)
    h2_out[...] = h2
    q = _dot(h2, wpq_ref[...]).astype(BF16)
    half = keys_ref.shape[2]
    for hp in range(keys_ref.shape[0]):
        s_out[hp] = _dot_nt(keys_ref[hp], q[:, hp * half:(hp + 1) * half])


def _merge(x, attn, hy, ga, gb, mod, lw):
    b, l, d = x.shape
    tm = min(l, 256)
    pb = 1 if mod.shape[0] > 1 else 0
    tok = lambda bi, i: (bi, i, 0)
    w_names = ["g2", "woa", "wob", "wout", "wpq", "keys"]
    nhp = lw["keys"].shape[0]
    return pl.pallas_call(
        _merge_kernel,
        grid=(b, l // tm),
        in_specs=[
            pl.BlockSpec((None, tm, d), tok),
            pl.BlockSpec((None, tm, attn.shape[2]), tok),
            pl.BlockSpec((None, tm, hy.shape[2]), tok),
            pl.BlockSpec((None, tm, d), tok),
            pl.BlockSpec((None, tm, d), tok),
            pl.BlockSpec((None, 6, d), lambda bi, i: (bi * pb, 0, 0)),
        ] + [_full(lw[n].shape) for n in w_names],
        out_specs=[
            pl.BlockSpec((None, tm, d), tok),
            pl.BlockSpec((None, tm, d), tok),
            pl.BlockSpec((None, nhp, N_KEYS, tm), lambda bi, i: (bi, 0, 0, i)),
        ],
        out_shape=[
            jax.ShapeDtypeStruct((b, l, d), F32),
            jax.ShapeDtypeStruct((b, l, d), BF16),
            jax.ShapeDtypeStruct((b, nhp, N_KEYS, l), F32),
        ],
        compiler_params=_cp("parallel", "parallel"),
        name="merge_peer_query",
    )(x, attn, hy, ga, gb, mod, *[lw[n] for n in w_names])


def _top_rows(s, k, payloads=()):
    r = s.shape[0]
    rid = lax.broadcasted_iota(jnp.int32, s.shape, 0).astype(F32)
    vals, idxs, picked = [], [], [[] for _ in payloads]
    for _ in range(k):
        m = jnp.max(s, axis=0, keepdims=True)
        ix = jnp.min(jnp.where(s == m, rid, float(r)), axis=0, keepdims=True)
        hit = rid == ix
        vals.append(m)
        idxs.append(ix)
        for p, acc in zip(payloads, picked):
            acc.append(jnp.sum(jnp.where(hit, p, 0.0), axis=0, keepdims=True))
        s = jnp.where(hit, -jnp.inf, s)
    cat = lambda rows: jnp.concatenate(rows, axis=0)
    return cat(vals), cat(idxs), [cat(a) for a in picked]


def _staircase(v1, r1, v2, r2):
    k = v1.shape[0]
    r1 = r1 * float(N_KEYS)
    cand, expert = [], []
    for a in range(k // 2):
        nb = k // (a + 1)
        rows = -(-nb // SUBLANES) * SUBLANES
        c = v1[a:a + 1] + v2[:rows]
        if nb < rows:
            c = jnp.where(lax.broadcasted_iota(jnp.int32, c.shape, 0) < nb, c, -jnp.inf)
        cand.append(c)
        expert.append(r1[a:a + 1] + r2[:rows])
    cand.append(v1[k // 2:] + v2[0:1])
    expert.append(r1[k // 2:] + r2[0:1])
    return jnp.concatenate(cand, axis=0), jnp.concatenate(expert, axis=0)


def _topk_kernel(s_ref, i1_out, i2_out, g_out, i1_sc, i2_sc, g_sc):
    def head(h, carry):
        v1, r1, _ = _top_rows(s_ref[2 * h], P_TOPK)
        v2, r2, _ = _top_rows(s_ref[2 * h + 1], P_TOPK)
        cand, expert = _staircase(v1, r1, v2, r2)
        top, _, (ex,) = _top_rows(cand, P_TOPK, (expert,))
        e = jnp.exp(top - top[0:1])
        k1 = jnp.floor(ex * (1.0 / N_KEYS))
        rows = pl.ds(pl.multiple_of(h * P_TOPK, P_TOPK), P_TOPK)
        i1_sc[rows, :] = k1
        i2_sc[rows, :] = ex - k1 * float(N_KEYS)
        g_sc[rows, :] = e / jnp.sum(e, axis=0, keepdims=True)
        return carry

    lax.fori_loop(0, P_HEADS, head, 0)
    i1_out[...] = i1_sc[...].T
    i2_out[...] = i2_sc[...].T
    g_out[...] = g_sc[...].T


def _peer_topk(s_t):
    b, nhp, nk, l = s_t.shape
    tt = min(l, 256)
    hk = P_HEADS * P_TOPK
    return pl.pallas_call(
        _topk_kernel,
        grid=(b, l // tt),
        in_specs=[pl.BlockSpec((None, nhp, nk, tt), lambda bi, i: (bi, 0, 0, i))],
        out_specs=[pl.BlockSpec((None, tt, hk), lambda bi, i: (bi, i, 0))] * 3,
        out_shape=[jax.ShapeDtypeStruct((b, l, hk), F32)] * 3,
        scratch_shapes=[pltpu.VMEM((hk, tt), F32)] * 3,
        compiler_params=_cp("parallel", "parallel"),
        name="peer_topk",
    )(s_t)


BF16_BITS = 0xFFFF0000
BF16_ROUND = 0x00008000
DENSE_TOKEN_UNROLL = 64


def _dense_kernel(h_ref, i1_ref, i2_ref, g_ref, u_ref, v_ref, x_ref, mod_ref, o_ref, ws_ref, acc_ref, *, pitch):
    j = pl.program_id(2)
    tm = h_ref.shape[0]
    r = u_ref.shape[0] // N_KEYS
    half = N_KEYS // 2

    @pl.when(j == 0)
    def _():
        acc_ref[...] = jnp.zeros_like(acc_ref)
        sub = lax.broadcasted_iota(jnp.int32, (N_KEYS, i1_ref.shape[1]), 0).astype(F32).astype(BF16)
        one, zero = jnp.ones((), BF16), jnp.zeros((), BF16)

        def toks(tb, carry):
            for u in range(0, DENSE_TOKEN_UNROLL, 2):
                t = tb * DENSE_TOKEN_UNROLL + u
                pts, qts = [], []
                for k in range(2):
                    row = pl.ds(t + k, 1)
                    pts.append(jnp.where(i1_ref[row, :].astype(BF16) == sub, g_ref[row, :].astype(BF16), zero))
                    qts.append(jnp.where(i2_ref[row, :].astype(BF16) == sub, one, zero))
                z = jnp.zeros_like(qts[0])
                rhs = jnp.concatenate([jnp.concatenate([qts[0], z], axis=1),
                                       jnp.concatenate([z, qts[1]], axis=1)], axis=0)
                bits = lax.bitcast_convert_type(_dot_nt(jnp.concatenate(pts, axis=1), rhs), jnp.uint32)
                bits = bits + jnp.uint32(BF16_ROUND)
                for k in range(2):
                    bk = bits[:, k * N_KEYS:(k + 1) * N_KEYS]
                    word = (bk[:half] & jnp.uint32(BF16_BITS)) | (bk[half:] >> 16)
                    ws_ref[pl.ds(t + k, half, stride=pitch), :] = word
            return carry

        lax.fori_loop(0, tm // DENSE_TOKEN_UNROLL, toks, 0)

    a = jax.nn.gelu(_dot_nt(h_ref[...], u_ref[...]))
    first = j * r
    shift = jnp.where(first >= half, 0, 16).astype(jnp.uint32)
    base = lax.rem(first, half)
    cols = []
    for q in range(r):
        word = ws_ref[pl.ds(pl.multiple_of((base + q) * pitch, SUBLANES), tm), :]
        cols.append(lax.bitcast_convert_type(((word >> shift) << 16), F32))
    acc_ref[...] += _dot((a * jnp.concatenate(cols, axis=1)).astype(BF16), v_ref[...])

    @pl.when(j == pl.num_programs(2) - 1)
    def _():
        o_ref[...] = x_ref[...] + mod_ref[5:6, :] * acc_ref[...]


def _peer_dense(h2, i1, i2, g, u_tab, v_tab, xn, mod):
    if mod.shape[0] == 1 and h2.shape[0] > 1:
        b, l, d = h2.shape
        flat = lambda a: a.reshape(1, b * l, a.shape[2])
        return _peer_dense(flat(h2), flat(i1), flat(i2), flat(g), u_tab, v_tab, flat(xn), mod).reshape(b, l, d)
    b, l, d = h2.shape
    tm = min(l, 512)
    te = 1024
    ne = u_tab.shape[0]
    hk = i1.shape[2]
    pitch = tm + SUBLANES
    pb = 1 if mod.shape[0] > 1 else 0
    assert tm % DENSE_TOKEN_UNROLL == 0 and (N_KEYS // 2) % (te // N_KEYS) == 0
    tok = lambda bi, i, j: (bi, i, 0)
    return pl.pallas_call(
        functools.partial(_dense_kernel, pitch=pitch),
        grid=(b, l // tm, ne // te),
        in_specs=[
            pl.BlockSpec((None, tm, d), tok),
            pl.BlockSpec((None, tm, hk), tok),
            pl.BlockSpec((None, tm, hk), tok),
            pl.BlockSpec((None, tm, hk), tok),
            pl.BlockSpec((te, d), lambda bi, i, j: (j, 0)),
            pl.BlockSpec((te, d), lambda bi, i, j: (j, 0)),
            pl.BlockSpec((None, tm, d), tok, pipeline_mode=pl.Buffered(1)),
            pl.BlockSpec((None, 6, d), lambda bi, i, j: (bi * pb, 0, 0)),
        ],
        out_specs=pl.BlockSpec((None, tm, d), tok),
        out_shape=jax.ShapeDtypeStruct((b, l, d), F32),
        scratch_shapes=[pltpu.VMEM((N_KEYS // 2 * pitch, N_KEYS), jnp.uint32), pltpu.VMEM((tm, d), F32)],
        compiler_params=pltpu.CompilerParams(dimension_semantics=("parallel", "parallel", "arbitrary"),
                                             vmem_limit_bytes=DENSE_VMEM_LIMIT_BYTES),
        name="peer_dense",
    )(h2, i1, i2, g, u_tab, v_tab, xn, mod)


def _final_kernel(x_ref, g_ref, o_ref):
    o_ref[...] = _rms(x_ref[...], g_ref[...])


def _final_norm(x, g):
    b, l, d = x.shape
    tm = min(l, 512)
    return pl.pallas_call(
        _final_kernel,
        grid=(b, l // tm),
        in_specs=[pl.BlockSpec((None, tm, d), lambda bi, i: (bi, i, 0)), _full((1, d))],
        out_specs=pl.BlockSpec((None, tm, d), lambda bi, i: (bi, i, 0)),
        out_shape=jax.ShapeDtypeStruct((b, l, d), F32),
        compiler_params=_cp("parallel", "parallel"),
        name="final_norm",
    )(x, g.reshape(1, d))


def _rope_tables(n_tok):
    rows = n_tok // GRID_W
    row = jnp.repeat(jnp.arange(rows, dtype=F32), GRID_W)
    col = jnp.tile(jnp.arange(GRID_W, dtype=F32), rows)
    n_freq = QK_ROPE // 4
    inv = 1.0 / (ROPE_BASE ** (jnp.arange(n_freq, dtype=F32) / n_freq))
    ang = jnp.concatenate([row[:, None] * inv, col[:, None] * inv], axis=-1)
    cos, sin = jnp.cos(ang), jnp.sin(ang)
    half = QK_ROPE // 2
    pad = LANES - QK_NOPE - QK_ROPE
    ones = jnp.ones((n_tok, QK_NOPE), F32)
    z = lambda w: jnp.zeros((n_tok, w), F32)
    c = jnp.concatenate([ones, cos, cos, z(pad)], axis=1)
    s1 = jnp.concatenate([z(QK_NOPE), -sin, z(half), z(pad)], axis=1)
    s2 = jnp.concatenate([z(QK_NOPE), z(half), sin, z(pad)], axis=1)
    return c, s1, s2


def _pad_heads(w, per_head):
    rows = w.shape[0]
    w = w.reshape(rows, N_HEADS, per_head)
    return jnp.pad(w, ((0, 0), (0, 0), (0, LANES - per_head))).reshape(rows, N_HEADS * LANES)


def _layer_weights(p, l):
    d = p["w_in"].shape[1]
    q_rank = p["q_norm"].shape[1]
    kv_rank = p["kv_norm"].shape[1]
    hy3 = p["hy_conv_w"].shape[2]
    w_in = p["w_in"][l]
    c0, c1, c2, c3 = q_rank, q_rank + kv_rank, q_rank + kv_rank + QK_ROPE, q_rank + kv_rank + QK_ROPE + hy3
    wkr = jnp.zeros((d, LANES), F32).at[:, QK_NOPE:QK_NOPE + QK_ROPE].set(w_in[:, c1:c2])
    ukv = p["w_ukv"][l].reshape(kv_rank, N_HEADS, QK_NOPE + V_DIM)
    wuk = _pad_heads(ukv[:, :, :QK_NOPE].reshape(kv_rank, -1), QK_NOPE)
    wuv = _pad_heads(ukv[:, :, QK_NOPE:].reshape(kv_rank, -1), V_DIM)
    woa = p["w_oa"][l].reshape(N_HEADS, V_DIM, d)
    woa = jnp.pad(woa, ((0, 0), (0, LANES - V_DIM), (0, 0))).reshape(N_HEADS * LANES, d)
    keys = p["peer_keys"][l]
    row = lambda v: v.reshape(1, -1)
    return dict(
        gn=row(p["norm_mix"][l]), wq=w_in[:, :c0].astype(BF16), wckv=w_in[:, c0:c1].astype(BF16),
        wkr=wkr.astype(BF16), why=w_in[:, c2:c3].astype(BF16), wg=w_in[:, c3:].astype(BF16),
        qn=row(p["q_norm"][l]), kvn=row(p["kv_norm"][l]),
        wuq=_pad_heads(p["w_uq"][l], QK_NOPE + QK_ROPE).astype(BF16),
        wuk=wuk.astype(BF16), wuv=wuv.astype(BF16),
        g2=row(p["norm_ffn"][l]), woa=woa.astype(BF16), wob=p["w_ob"][l].astype(BF16),
        wout=p["w_out"][l].astype(BF16), wpq=p["peer_wq"][l].astype(BF16),
        keys=keys.reshape(keys.shape[0] * keys.shape[1], keys.shape[2], keys.shape[3]).astype(BF16),
        u=p["peer_u"][l].astype(BF16), v=p["peer_v"][l].astype(BF16),
    )


def _layer(x, mod, lw, lp, rope_tabs, ctx_kv):
    q, k, cb, ckv, kr, hy_in, ga, gb = _inproj(x, mod, lw, rope_tabs)
    if ctx_kv is None:
        attn = _attention(q, k, cb, lw["wuv"])
    else:
        attn = _attention(q, k, cb, lw["wuv"], *ctx_kv)
    hy = _hyena(hy_in, lp)
    xn, h2, s_t = _merge(x, attn, hy, ga, gb, mod, lw)
    i1, i2, g = _peer_topk(s_t)
    x = _peer_dense(h2, i1, i2, g, lw["u"], lw["v"], xn, mod)
    return x, ckv, kr[:, :, QK_NOPE:QK_NOPE + QK_ROPE]


def kernel(x_prompt, x_sample, cache_ckv, cache_krope, c, c_ctx, w_ada, b_ada, norm_mix, norm_ffn,
           w_in, q_norm, kv_norm, w_uq, w_ukv, w_oa, w_ob, w_out, hy_conv_w, hy_conv_b,
           hy_w1, hy_b1, hy_w2, hy_b2, hy_w3, hy_decay, hy_bias,
           peer_wq, peer_keys, peer_u, peer_v, final_norm):
    p = dict(norm_mix=norm_mix, norm_ffn=norm_ffn, w_in=w_in, q_norm=q_norm, kv_norm=kv_norm, w_uq=w_uq,
             w_ukv=w_ukv, w_oa=w_oa, w_ob=w_ob, w_out=w_out, peer_wq=peer_wq, peer_keys=peer_keys,
             peer_u=peer_u, peer_v=peer_v, hy_conv_w=hy_conv_w)
    depth, d = norm_mix.shape
    bd = x_sample.shape[0]
    rows = -(-(bd + 1) // SUBLANES) * SUBLANES
    cc = jnp.zeros((rows, d), F32).at[:bd].set(c).at[bd].set(c_ctx)
    mods = _adaln(cc, w_ada, b_ada).reshape(depth, rows, 6, d)
    rope_tabs = _rope_tables(x_sample.shape[1])
    e_mat = jnp.zeros((QK_ROPE, N_HEADS, LANES), F32)
    e_mat = e_mat.at[:, :, QK_NOPE:QK_NOPE + QK_ROPE].set(jnp.eye(QK_ROPE, dtype=F32)[:, None, :])
    e_mat = e_mat.reshape(QK_ROPE, N_HEADS * LANES).astype(BF16)
    xp, xs = x_prompt, x_sample
    new_ckv, new_kr = [], []
    for l in range(depth):
        lw = _layer_weights(p, l)
        lp = dict(hy_conv_w=hy_conv_w[l], hy_conv_b=hy_conv_b[l], hy_w1=hy_w1[l], hy_b1=hy_b1[l],
                  hy_w2=hy_w2[l], hy_b2=hy_b2[l], hy_w3=hy_w3[l], hy_decay=hy_decay[l], hy_bias=hy_bias[l])
        xp, ckv_p, kr_p = _layer(xp, mods[l, bd:bd + 1], lw, lp, None, None)
        new_ckv.append(ckv_p)
        new_kr.append(kr_p)
        ctx_kv = _kvcache(cache_ckv, cache_krope, l, lw["wuk"], e_mat)
        xs, _, _ = _layer(xs, mods[l, :bd], lw, lp, rope_tabs, ctx_kv)
    return (_final_norm(xp, final_norm), _final_norm(xs, final_norm),
            jnp.stack(new_ckv, axis=1), jnp.stack(new_kr, axis=1))
```

```python
import functools
import math

import numpy as np
import jax
import jax.numpy as jnp
from jax import lax
from jax.experimental import pallas as pl
from jax.experimental.pallas import tpu as pltpu

F32 = jnp.float32
BF16 = jnp.bfloat16

GRID_W = 64
N_HEADS = 8
QK_NOPE = 64
QK_ROPE = 32
V_DIM = 64
ROPE_BASE = 10000.0
HY_ORDER = 2
HY_BANDS = 16
SHORT_K = 3
P_HEADS = 8
N_KEYS = 128
P_TOPK = 16
EPS = 1e-6

LANES = 128
SUBLANES = 8
FFT_B = 64
DIRECT_DFT_MAX_L = 256
VMEM_LIMIT_BYTES = 48 * 1024 * 1024
DENSE_VMEM_LIMIT_BYTES = 56 * 1024 * 1024


def _cp(*sem):
    return pltpu.CompilerParams(dimension_semantics=sem, vmem_limit_bytes=VMEM_LIMIT_BYTES)


def _full(shape):
    n = len(shape)
    return pl.BlockSpec(shape, lambda *_: (0,) * n)


def _tile(n, pref, unit=LANES):
    if n <= pref:
        return n
    return max(t for t in range(unit, pref + 1, unit) if n % t == 0)


def _dot(a, b):
    return jnp.dot(a, b, preferred_element_type=F32)


def _dot_hi(a, b):
    return jnp.dot(a, b, precision=lax.Precision.HIGHEST, preferred_element_type=F32)


def _dot_nt(a, b):
    return lax.dot_general(a, b, (((1,), (1,)), ((), ())), preferred_element_type=F32)


def _rms(x, g):
    return x * lax.rsqrt(jnp.mean(x * x, axis=-1, keepdims=True) + EPS) * g


def _adaln_kernel(c_ref, w_ref, b_ref, o_ref):
    c = c_ref[...]
    s = (c * jax.nn.sigmoid(c)).astype(BF16)
    o_ref[...] = _dot(s, w_ref[...].astype(BF16)) + b_ref[...]


def _adaln(cc, w_ada, b_ada):
    depth, d, d6 = w_ada.shape
    rows = cc.shape[0]
    tn = d6 // 4
    return pl.pallas_call(
        _adaln_kernel,
        grid=(depth, d6 // tn),
        in_specs=[
            _full((rows, d)),
            pl.BlockSpec((None, d, tn), lambda l, j: (l, 0, j)),
            pl.BlockSpec((None, 1, tn), lambda l, j: (l, 0, j)),
        ],
        out_specs=pl.BlockSpec((None, rows, tn), lambda l, j: (l, 0, j)),
        out_shape=jax.ShapeDtypeStruct((depth, rows, d6), F32),
        compiler_params=_cp("parallel", "parallel"),
        name="adaln",
    )(cc, w_ada, b_ada.reshape(depth, 1, d6))


ATTN_SCALE = 1.0 / math.sqrt(QK_NOPE + QK_ROPE)
ATTN_KEY_CHUNK = 1024
ATTN_HEADS_PER_STEP = 4


def _rope_slab(x, c, s1, s2):
    half = QK_ROPE // 2
    return x * c + pltpu.roll(x, LANES - half, 1) * s1 + pltpu.roll(x, half, 1) * s2


def _inproj_kernel(*refs, rope):
    if rope:
        (x_ref, mod_ref, gn_ref, wq_ref, wckv_ref, wkr_ref, why_ref, wg_ref, qn_ref, kvn_ref,
         wuq_ref, wuk_ref, c_ref, s1_ref, s2_ref,
         q_out, k_out, cb_out, ckv_out, kr_out, hy_out, ga_out, gb_out) = refs
    else:
        (x_ref, mod_ref, gn_ref, wq_ref, wckv_ref, wkr_ref, why_ref, wg_ref, qn_ref, kvn_ref,
         wuq_ref, wuk_ref,
         q_out, k_out, cb_out, ckv_out, kr_out, hy_out, ga_out, gb_out) = refs
    x = x_ref[...]
    mod = mod_ref[...]
    d = x.shape[1]
    h = (_rms(x, gn_ref[...]) * (1.0 + mod[1:2]) + mod[0:1]).astype(BF16)
    q = _dot(_rms(_dot(h, wq_ref[...]), qn_ref[...]).astype(BF16), wuq_ref[...])
    ckv = _rms(_dot(h, wckv_ref[...]), kvn_ref[...])
    kr = _dot(h, wkr_ref[...])
    ckv_b = ckv.astype(BF16)
    kn = _dot(ckv_b, wuk_ref[...])
    if rope:
        c, s1, s2 = c_ref[...], s1_ref[...], s2_ref[...]
        kr_k = _rope_slab(kr, c, s1, s2)
        q = jnp.concatenate(
            [_rope_slab(q[:, i * LANES:(i + 1) * LANES], c, s1, s2) for i in range(N_HEADS)], axis=1)
    else:
        kr_k = kr
    q_out[...] = (q * ATTN_SCALE).astype(BF16)
    k_out[...] = (kn + jnp.concatenate([kr_k] * N_HEADS, axis=1)).astype(BF16)
    cb_out[...] = ckv_b
    ckv_out[...] = ckv
    kr_out[...] = kr
    hy_out[...] = _dot(h, why_ref[...])
    g = _dot(h, wg_ref[...])
    ga_out[...] = jax.nn.sigmoid(g[:, :d])
    gb_out[...] = jax.nn.sigmoid(g[:, d:])


def _inproj(x, mod, lw, rope_tabs):
    b, l, d = x.shape
    tm = min(l, 256)
    pb = 1 if mod.shape[0] > 1 else 0
    hl = N_HEADS * LANES
    kvr = lw["wckv"].shape[1]
    hyw = lw["why"].shape[1]
    rope = rope_tabs is not None
    tok = lambda bi, i: (bi, i, 0)
    w_names = ["gn", "wq", "wckv", "wkr", "why", "wg", "qn", "kvn", "wuq", "wuk"]
    ins = [x, mod] + [lw[n] for n in w_names]
    in_specs = [pl.BlockSpec((None, tm, d), tok), pl.BlockSpec((None, 6, d), lambda bi, i: (bi * pb, 0, 0))]
    in_specs += [_full(lw[n].shape) for n in w_names]
    if rope:
        ins += list(rope_tabs)
        in_specs += [pl.BlockSpec((tm, LANES), lambda bi, i: (i, 0))] * 3
    widths = [(hl, BF16), (hl, BF16), (kvr, BF16), (kvr, F32), (LANES, F32), (hyw, F32), (d, F32), (d, F32)]
    return pl.pallas_call(
        functools.partial(_inproj_kernel, rope=rope),
        grid=(b, l // tm),
        in_specs=in_specs,
        out_specs=[pl.BlockSpec((None, tm, w), tok) for w, _ in widths],
        out_shape=[jax.ShapeDtypeStruct((b, l, w), dt) for w, dt in widths],
        compiler_params=_cp("parallel", "parallel"),
        name="inproj_rope" if rope else "inproj",
    )(*ins)


def _kvcache_kernel(ckv_ref, kr_ref, wuk_ref, e_ref, k_out, cb_out):
    ckv_b = ckv_ref[...].astype(BF16)
    k_out[...] = (_dot(ckv_b, wuk_ref[...]) + _dot(kr_ref[...].astype(BF16), e_ref[...])).astype(BF16)
    cb_out[...] = ckv_b


def _kvcache(cache_ckv, cache_kr, layer, wuk, e_mat):
    b, _, p, kvr = cache_ckv.shape
    hl = N_HEADS * LANES
    return pl.pallas_call(
        _kvcache_kernel,
        grid=(b,),
        in_specs=[
            pl.BlockSpec((None, None, p, kvr), lambda bi: (bi, layer, 0, 0)),
            pl.BlockSpec((None, None, p, QK_ROPE), lambda bi: (bi, layer, 0, 0)),
            _full(wuk.shape),
            _full(e_mat.shape),
        ],
        out_specs=[pl.BlockSpec((None, p, hl), lambda bi: (bi, 0, 0)),
                   pl.BlockSpec((None, p, kvr), lambda bi: (bi, 0, 0))],
        out_shape=[jax.ShapeDtypeStruct((b, p, hl), BF16), jax.ShapeDtypeStruct((b, p, kvr), BF16)],
        compiler_params=_cp("parallel"),
        name="kvcache",
    )(cache_ckv, cache_kr, wuk, e_mat)


def _attn_kernel(*refs, ctx):
    if ctx:
        q_ref, k_ref, c_ref, wuv_ref, kc_ref, cc_ref, o_ref = refs
    else:
        q_ref, k_ref, c_ref, wuv_ref, o_ref = refs
    chunks = [(kc_ref, cc_ref, 0, kc_ref.shape[0])] if ctx else []
    n_keys = k_ref.shape[0]
    ck = min(n_keys, ATTN_KEY_CHUNK)
    chunks += [(k_ref, c_ref, c * ck, ck) for c in range(n_keys // ck)]
    heads = [slice(i * LANES, (i + 1) * LANES) for i in range(ATTN_HEADS_PER_STEP)]
    qs = [q_ref[:, h] for h in heads]
    m = [None] * len(heads)
    acc = [None] * len(heads)
    den = [None] * len(heads)
    for kr, cr, start, size in chunks:
        for i, h in enumerate(heads):
            s = _dot_nt(qs[i], kr[pl.ds(start, size), h])
            cmax = jnp.max(s, axis=1, keepdims=True)
            if m[i] is None:
                m[i] = cmax
                p = jnp.exp(s - cmax)
                den[i] = jnp.sum(p, axis=1, keepdims=True)
                acc[i] = _dot(p.astype(BF16), cr[pl.ds(start, size), :])
            else:
                m_new = jnp.maximum(m[i], cmax)
                alpha = jnp.exp(m[i] - m_new)
                p = jnp.exp(s - m_new)
                den[i] = den[i] * alpha + jnp.sum(p, axis=1, keepdims=True)
                acc[i] = acc[i] * alpha + _dot(p.astype(BF16), cr[pl.ds(start, size), :])
                m[i] = m_new
    for i, h in enumerate(heads):
        o_ref[:, h] = _dot((acc[i] / den[i]).astype(BF16), wuv_ref[:, h]).astype(BF16)


def _attention(q, k, cb, wuv, kc=None, cc=None):
    b, l, hl = q.shape
    kvr = cb.shape[2]
    tq = min(l, 512)
    ctx = kc is not None
    hw = ATTN_HEADS_PER_STEP * LANES
    ins = [q, k, cb, wuv]
    in_specs = [
        pl.BlockSpec((None, tq, hw), lambda bi, h, i: (bi, i, h)),
        pl.BlockSpec((None, l, hw), lambda bi, h, i: (bi, 0, h)),
        pl.BlockSpec((None, l, kvr), lambda bi, h, i: (bi, 0, 0)),
        pl.BlockSpec((kvr, hw), lambda bi, h, i: (0, h)),
    ]
    if ctx:
        p = kc.shape[1]
        ins += [kc, cc]
        in_specs += [pl.BlockSpec((None, p, hw), lambda bi, h, i: (bi, 0, h)),
                     pl.BlockSpec((None, p, kvr), lambda bi, h, i: (bi, 0, 0))]
    return pl.pallas_call(
        functools.partial(_attn_kernel, ctx=ctx),
        grid=(b, N_HEADS // ATTN_HEADS_PER_STEP, l // tq),
        in_specs=in_specs,
        out_specs=pl.BlockSpec((None, tq, hw), lambda bi, h, i: (bi, i, h)),
        out_shape=jax.ShapeDtypeStruct((b, l, hl), BF16),
        compiler_params=_cp("parallel", "parallel", "parallel"),
        name="attn_ctx" if ctx else "attn",
    )(*ins)


def _shortconv_kernel(x_ref, w_ref, b_ref, o_ref):
    x = x_ref[...]
    n = x.shape[0]
    row = lax.broadcasted_iota(jnp.int32, x.shape, 0)
    xm = jnp.where(row == 0, 0.0, pltpu.roll(x, 1, 0))
    xp = jnp.where(row == n - 1, 0.0, pltpu.roll(x, n - 1, 0))
    w = w_ref[...]
    o_ref[...] = (xm * w[0:1] + x * w[1:2] + xp * w[2:3] + b_ref[...]).astype(BF16)


def _shortconv(x, w, bias):
    b, l, c = x.shape
    ct = _tile(c, 256)
    return pl.pallas_call(
        _shortconv_kernel,
        grid=(b, c // ct),
        in_specs=[
            pl.BlockSpec((None, l, ct), lambda bi, j: (bi, 0, j)),
            pl.BlockSpec((SHORT_K, ct), lambda bi, j: (0, j)),
            pl.BlockSpec((1, ct), lambda bi, j: (0, j)),
        ],
        out_specs=pl.BlockSpec((None, l, ct), lambda bi, j: (bi, 0, j)),
        out_shape=jax.ShapeDtypeStruct((b, l, c), BF16),
        compiler_params=_cp("parallel", "parallel"),
        name="shortconv",
    )(x, w, bias.reshape(1, c))


def _filter_kernel(w1_ref, b1_ref, w2_ref, b2_ref, w3_ref, dec_ref, o_ref, *, seq, width):
    tr = o_ref.shape[0]
    d = lax.broadcasted_iota(jnp.int32, (tr, LANES), 0) + pl.program_id(0) * tr
    t = d.astype(F32) / seq
    lane = lax.broadcasted_iota(jnp.int32, (tr, LANES), 1)
    band = jnp.where(lane <= HY_BANDS, lane, lane - HY_BANDS).astype(F32)
    ang = (2.0 * math.pi * t) * band
    z = jnp.where(lane == 0, t,
                  jnp.where(lane <= HY_BANDS, jnp.sin(ang),
                            jnp.where(lane <= 2 * HY_BANDS, jnp.cos(ang), 0.0)))
    f = jnp.sin(_dot_hi(z, w1_ref[...]) + b1_ref[...])
    f = jnp.sin(_dot_hi(f, w2_ref[...]) + b2_ref[...])
    hw = _dot_hi(f, w3_ref[...]) * jnp.exp(-dec_ref[...] * t[:, :1])
    lag0 = jnp.where(d[:, :1] == 0, 1.0, 0.0)
    for o in range(HY_ORDER):
        fwd = hw[:, (2 * o) * width:(2 * o + 1) * width]
        bwd = hw[:, (2 * o + 1) * width:(2 * o + 2) * width]
        o_ref[:, (2 * o) * width:(2 * o + 1) * width] = fwd + lag0 * bwd
        o_ref[:, (2 * o + 1) * width:(2 * o + 2) * width] = bwd


def _hyena_filter(seq, w1, b1, w2, b2, w3, decay):
    width = decay.shape[-1]
    tr = min(seq, 512)
    hid = w2.shape[0]
    w1p = jnp.zeros((LANES, hid), F32).at[:w1.shape[0]].set(w1)
    ins = [w1p, b1.reshape(1, hid), w2, b2.reshape(1, hid), w3, decay.reshape(1, -1)]
    taps = pl.pallas_call(
        functools.partial(_filter_kernel, seq=seq, width=width),
        grid=(seq // tr,),
        in_specs=[_full(a.shape) for a in ins],
        out_specs=pl.BlockSpec((tr, 2 * HY_ORDER * width), lambda i: (i, 0)),
        out_shape=jax.ShapeDtypeStruct((seq, 2 * HY_ORDER * width), F32),
        compiler_params=_cp("parallel"),
        name="hyena_filter",
    )(*ins).reshape(seq, HY_ORDER, 2, width)
    fwd = taps[:, :, 0].reshape(seq, HY_ORDER * width)
    bwd = taps[:, :, 1].reshape(seq, HY_ORDER * width)
    return jnp.concatenate([fwd, jnp.zeros_like(fwd[:1]), jnp.flip(bwd[1:], axis=0)], axis=0)


def _direct_consts(seq):
    n = 2 * seq
    k = np.arange(n, dtype=np.float64)[:, None]
    pos = np.arange(n, dtype=np.float64)[None, :]
    ang = 2.0 * np.pi * k * pos / n
    fwd = np.concatenate([np.cos(ang), -np.sin(ang)], axis=0)
    inv = np.concatenate([np.cos(ang.T), -np.sin(ang.T)], axis=1)[:seq] / n
    return (jnp.asarray(fwd, F32), jnp.asarray(fwd[:, :seq], BF16), jnp.asarray(inv, BF16))


def _dfilt_kernel(kc_ref, f_ref, o_ref):
    o_ref[...] = _dot_hi(f_ref[...], kc_ref[...])


def _direct_filter_spectrum(kc, fwd_full):
    n, ch = kc.shape
    ct = _tile(ch, 256)
    return pl.pallas_call(
        _dfilt_kernel,
        grid=(ch // ct,),
        in_specs=[pl.BlockSpec((n, ct), lambda j: (0, j)), _full(fwd_full.shape)],
        out_specs=pl.BlockSpec((2 * n, ct), lambda j: (0, j)),
        out_shape=jax.ShapeDtypeStruct((2 * n, ch), F32),
        compiler_params=_cp("parallel"),
        name="filter_spectrum_direct",
    )(kc, fwd_full)


def _dconv_kernel(v_ref, x1_ref, x2_ref, kf_ref, bias_ref, f_ref, fi_ref, o_ref):
    n = f_ref.shape[0] // 2
    ct = o_ref.shape[1]
    gates = (x1_ref, x2_ref)
    y = v_ref[...].astype(F32)
    for o in range(HY_ORDER):
        z = _dot(f_ref[...], y.astype(BF16))
        zr, zi = z[:n], z[n:]
        kr = kf_ref[:n, o * ct:(o + 1) * ct]
        ki = kf_ref[n:, o * ct:(o + 1) * ct]
        prod = jnp.concatenate([zr * kr - zi * ki, zr * ki + zi * kr], axis=0).astype(BF16)
        y = gates[o][...].astype(F32) * (_dot(fi_ref[...], prod) + y * bias_ref[o:o + 1, :])
    o_ref[...] = y.astype(BF16)


def _hyena_direct(hs, kf, bias, fwd, inv):
    b, l, c3 = hs.shape
    w = c3 // (HY_ORDER + 1)
    ct = _tile(w, 256)
    nt = w // ct
    return pl.pallas_call(
        _dconv_kernel,
        grid=(nt, b),
        in_specs=[
            pl.BlockSpec((None, l, ct), lambda j, bi: (bi, 0, j)),
            pl.BlockSpec((None, l, ct), lambda j, bi: (bi, 0, nt + j)),
            pl.BlockSpec((None, l, ct), lambda j, bi: (bi, 0, 2 * nt + j)),
            pl.BlockSpec((kf.shape[0], HY_ORDER * ct), lambda j, bi: (0, j)),
            pl.BlockSpec((HY_ORDER, ct), lambda j, bi: (0, j)),
            _full(fwd.shape),
            _full(inv.shape),
        ],
        out_specs=pl.BlockSpec((None, l, ct), lambda j, bi: (bi, 0, j)),
        out_shape=jax.ShapeDtypeStruct((b, l, w), BF16),
        compiler_params=_cp("parallel", "parallel"),
        name="hyena_conv_direct",
    )(hs, hs, hs, kf, bias, fwd, inv)


def _pairs_to_lanes(x):
    *lead, nb, c = x.shape
    if x.dtype.itemsize == 2:
        w = lax.bitcast_convert_type(x.reshape(*lead, nb, c // 2, 2), jnp.uint32)
        return lax.bitcast_convert_type(jnp.swapaxes(w, -1, -2), x.dtype).reshape(*lead, c * nb)
    return jnp.swapaxes(x.reshape(*lead, nb, c // 2, 2), -2, -3).reshape(*lead, c * nb)


def _lanes_to_pairs(x, nb):
    *lead, cn = x.shape
    c = cn // nb
    if x.dtype.itemsize == 2:
        w = lax.bitcast_convert_type(x.reshape(*lead, c // 2, nb, 2), jnp.uint32)
        return lax.bitcast_convert_type(jnp.swapaxes(w, -1, -2), x.dtype).reshape(*lead, nb, c)
    return jnp.swapaxes(x.reshape(*lead, c // 2, nb, 2), -2, -3).reshape(*lead, nb, c)


def _two_stage_consts(seq):
    n = 2 * seq
    n1 = n // FFT_B
    a_in = n1 // 2
    k1 = np.arange(n1, dtype=np.float64)[:, None]
    a = np.arange(n1, dtype=np.float64)[None, :]
    ang1 = 2.0 * np.pi * k1 * a / n1
    f1_full = np.concatenate([np.cos(ang1), -np.sin(ang1)], axis=0)
    f1_inv = np.concatenate([np.cos(ang1.T), -np.sin(ang1.T)], axis=1)[:a_in] / n
    bb = np.arange(FFT_B, dtype=np.float64)[None, :]
    angt = 2.0 * np.pi * k1 * bb / n
    rep = LANES // FFT_B
    tw = np.stack([np.repeat(np.cos(angt), rep, axis=1), np.repeat(-np.sin(angt), rep, axis=1)])
    b_ = np.arange(FFT_B, dtype=np.float64)[:, None]
    k2 = np.arange(FFT_B, dtype=np.float64)[None, :]
    ang2 = 2.0 * np.pi * b_ * k2 / FFT_B
    bd = lambda m: np.kron(m, np.eye(rep))
    cr, ci = np.cos(ang2), -np.sin(ang2)
    w2 = np.block([[bd(cr), bd(ci)], [bd(-ci), bd(cr)]])
    er, ei = np.cos(ang2), np.sin(ang2)
    w2i = np.block([[bd(er), bd(ei)], [bd(-ei), bd(er)]])
    return dict(
        f1_full=jnp.asarray(f1_full, F32), f1=jnp.asarray(f1_full[:, :a_in], BF16),
        f1_inv=jnp.asarray(f1_inv, BF16), tw=jnp.asarray(tw, F32),
        w2=jnp.asarray(w2, BF16), w2_f32=jnp.asarray(w2, F32), w2i=jnp.asarray(w2i, BF16))


def _fft_fwd(x, f1, tw_r, tw_i, w2, dot, cast):
    n1 = tw_r.shape[0]
    z = dot(f1, cast(x))
    zr, zi = z[:n1], z[n1:]
    rows = []
    for g in range(x.shape[1] // LANES):
        a = zr[:, g * LANES:(g + 1) * LANES]
        b = zi[:, g * LANES:(g + 1) * LANES]
        rows.append(jnp.concatenate([a * tw_r - b * tw_i, a * tw_i + b * tw_r], axis=1))
    return dot(cast(jnp.concatenate(rows, axis=0)), w2)


def _fft_inv(yf, w2i, tw_r, tw_i, f1_inv):
    n1 = tw_r.shape[0]
    gm = _dot(yf.astype(BF16), w2i)
    cols = []
    for g in range(yf.shape[0] // n1):
        blk = gm[g * n1:(g + 1) * n1]
        gr, gi = blk[:, :LANES], blk[:, LANES:]
        cols.append(jnp.concatenate([gr * tw_r + gi * tw_i, gi * tw_r - gr * tw_i], axis=0))
    return _dot(f1_inv, jnp.concatenate(cols, axis=1).astype(BF16))


def _ffilt_kernel(kc_ref, f1_ref, tw_ref, w2_ref, o_ref):
    x = _fft_fwd(kc_ref[...], f1_ref[...], tw_ref[0], tw_ref[1], w2_ref[...], _dot_hi, lambda v: v)
    o_ref[...] = x.reshape(o_ref.shape)


def _two_stage_filter_spectrum(kc_t, cs, groups_per_tile):
    n1, lanes = kc_t.shape
    wt = groups_per_tile * LANES
    ng = lanes // LANES
    return pl.pallas_call(
        _ffilt_kernel,
        grid=(lanes // wt,),
        in_specs=[pl.BlockSpec((n1, wt), lambda j: (0, j)), _full(cs["f1_full"].shape),
                  _full(cs["tw"].shape), _full(cs["w2_f32"].shape)],
        out_specs=pl.BlockSpec((groups_per_tile, n1, 2 * LANES), lambda j: (j, 0, 0)),
        out_shape=jax.ShapeDtypeStruct((ng, n1, 2 * LANES), F32),
        compiler_params=_cp("parallel"),
        name="filter_spectrum_two_stage",
    )(kc_t, cs["f1_full"], cs["tw"], cs["w2_f32"])


def _fconv_kernel(v_ref, x1_ref, x2_ref, kf_ref, bias_ref, f1_ref, tw_ref, w2_ref, w2i_ref, f1i_ref, o_ref):
    tw_r, tw_i = tw_ref[0], tw_ref[1]
    gates = (x1_ref, x2_ref)
    y = v_ref[...].astype(F32)
    for o in range(HY_ORDER):
        x = _fft_fwd(y, f1_ref[...], tw_r, tw_i, w2_ref[...], _dot, lambda v: v.astype(BF16))
        kf = kf_ref[o]
        kf = kf.reshape(kf.shape[0] * kf.shape[1], kf.shape[2])
        xr, xi = x[:, :LANES], x[:, LANES:]
        kr, ki = kf[:, :LANES], kf[:, LANES:]
        prod = jnp.concatenate([xr * kr - xi * ki, xr * ki + xi * kr], axis=1)
        conv = _fft_inv(prod, w2i_ref[...], tw_r, tw_i, f1i_ref[...])
        y = gates[o][...].astype(F32) * (conv + y * bias_ref[o:o + 1, :])
    o_ref[...] = y.astype(BF16)


def _hyena_two_stage(hs_t, kf, bias_t, cs, width):
    b, a_in, _ = hs_t.shape
    cg = min(width, 32)
    wt = cg * FFT_B
    gpt = wt // LANES
    nt = width // cg
    n1 = kf.shape[2]
    consts = [cs["f1"], cs["tw"], cs["w2"], cs["w2i"], cs["f1_inv"]]
    return pl.pallas_call(
        _fconv_kernel,
        grid=(nt, b),
        in_specs=[
            pl.BlockSpec((None, a_in, wt), lambda j, bi: (bi, 0, j)),
            pl.BlockSpec((None, a_in, wt), lambda j, bi: (bi, 0, nt + j)),
            pl.BlockSpec((None, a_in, wt), lambda j, bi: (bi, 0, 2 * nt + j)),
            pl.BlockSpec((HY_ORDER, gpt, n1, 2 * LANES), lambda j, bi: (0, j, 0, 0)),
            pl.BlockSpec((HY_ORDER, wt), lambda j, bi: (0, j)),
        ] + [_full(c.shape) for c in consts],
        out_specs=pl.BlockSpec((None, a_in, wt), lambda j, bi: (bi, 0, j)),
        out_shape=jax.ShapeDtypeStruct((b, a_in, width * FFT_B), BF16),
        compiler_params=_cp("parallel", "parallel"),
        name="hyena_conv_two_stage",
    )(hs_t, hs_t, hs_t, kf, bias_t, *consts)


def _hyena(hy_in, lp):
    b, l, c3 = hy_in.shape
    width = c3 // (HY_ORDER + 1)
    hs = _shortconv(hy_in, lp["hy_conv_w"], lp["hy_conv_b"])
    kc = _hyena_filter(l, lp["hy_w1"], lp["hy_b1"], lp["hy_w2"], lp["hy_b2"], lp["hy_w3"], lp["hy_decay"])
    n = 2 * l
    if l <= DIRECT_DFT_MAX_L:
        fwd_full, fwd, inv = _direct_consts(l)
        ct = _tile(width, 256)
        kf = _direct_filter_spectrum(kc, fwd_full)
        kf = kf.reshape(2 * n, HY_ORDER, width // ct, ct).transpose(0, 2, 1, 3).reshape(2 * n, HY_ORDER * width)
        return _hyena_direct(hs, kf, lp["hy_bias"], fwd, inv)
    cs = _two_stage_consts(l)
    n1 = n // FFT_B
    a_in = l // FFT_B
    ow = HY_ORDER * width
    kc_t = _pairs_to_lanes(kc.reshape(n1, FFT_B, ow))
    kf = _two_stage_filter_spectrum(kc_t, cs, min(ow * FFT_B // LANES, 16))
    kf = kf.reshape(HY_ORDER, width * FFT_B // LANES, n1, 2 * LANES)
    hs_t = _pairs_to_lanes(hs.reshape(b, a_in, FFT_B, c3))
    bias_t = _pairs_to_lanes(jnp.broadcast_to(lp["hy_bias"][:, None, :], (HY_ORDER, FFT_B, width)))
    hy_t = _hyena_two_stage(hs_t, kf, bias_t, cs, width)
    return _lanes_to_pairs(hy_t, FFT_B).reshape(b, l, width)


def _merge_kernel(x_ref, o_ref, hy_ref, ga_ref, gb_ref, mod_ref, g2_ref, woa_ref, wob_ref, wout_ref,
                  wpq_ref, keys_ref, xn_out, h2_out, s_out):
    mod = mod_ref[...]
    merged = ga_ref[...] * _dot(o_ref[...], woa_ref[...]) + gb_ref[...] * _dot(hy_ref[...], wob_ref[...])
    xn = x_ref[...] + mod[2:3] * _dot(merged.astype(BF16), wout_ref[...])
    xn_out[...] = xn
    h2 = (_rms(xn, g2_ref[...]) * (1.0 + mod[4:5]) + mod[3:4]).astype(BF16)
    h2_out[...] = h2
    q = _dot(h2, wpq_ref[...]).astype(BF16)
    half = keys_ref.shape[2]
    for hp in range(keys_ref.shape[0]):
        s_out[hp] = _dot_nt(keys_ref[hp], q[:, hp * half:(hp + 1) * half])


def _merge(x, attn, hy, ga, gb, mod, lw):
    b, l, d = x.shape
    tm = min(l, 256)
    pb = 1 if mod.shape[0] > 1 else 0
    tok = lambda bi, i: (bi, i, 0)
    w_names = ["g2", "woa", "wob", "wout", "wpq", "keys"]
    nhp = lw["keys"].shape[0]
    return pl.pallas_call(
        _merge_kernel,
        grid=(b, l // tm),
        in_specs=[
            pl.BlockSpec((None, tm, d), tok),
            pl.BlockSpec((None, tm, attn.shape[2]), tok),
            pl.BlockSpec((None, tm, hy.shape[2]), tok),
            pl.BlockSpec((None, tm, d), tok),
            pl.BlockSpec((None, tm, d), tok),
            pl.BlockSpec((None, 6, d), lambda bi, i: (bi * pb, 0, 0)),
        ] + [_full(lw[n].shape) for n in w_names],
        out_specs=[
            pl.BlockSpec((None, tm, d), tok),
            pl.BlockSpec((None, tm, d), tok),
            pl.BlockSpec((None, nhp, N_KEYS, tm), lambda bi, i: (bi, 0, 0, i)),
        ],
        out_shape=[
            jax.ShapeDtypeStruct((b, l, d), F32),
            jax.ShapeDtypeStruct((b, l, d), BF16),
            jax.ShapeDtypeStruct((b, nhp, N_KEYS, l), F32),
        ],
        compiler_params=_cp("parallel", "parallel"),
        name="merge_peer_query",
    )(x, attn, hy, ga, gb, mod, *[lw[n] for n in w_names])


def _top_rows(s, k, payloads=()):
    r = s.shape[0]
    rid = lax.broadcasted_iota(jnp.int32, s.shape, 0).astype(F32)
    vals, idxs, picked = [], [], [[] for _ in payloads]
    for _ in range(k):
        m = jnp.max(s, axis=0, keepdims=True)
        ix = jnp.min(jnp.where(s == m, rid, float(r)), axis=0, keepdims=True)
        hit = rid == ix
        vals.append(m)
        idxs.append(ix)
        for p, acc in zip(payloads, picked):
            acc.append(jnp.sum(jnp.where(hit, p, 0.0), axis=0, keepdims=True))
        s = jnp.where(hit, -jnp.inf, s)
    cat = lambda rows: jnp.concatenate(rows, axis=0)
    return cat(vals), cat(idxs), [cat(a) for a in picked]


def _staircase(v1, r1, v2, r2):
    k = v1.shape[0]
    r1 = r1 * float(N_KEYS)
    cand, expert = [], []
    for a in range(k // 2):
        nb = k // (a + 1)
        rows = -(-nb // SUBLANES) * SUBLANES
        c = v1[a:a + 1] + v2[:rows]
        if nb < rows:
            c = jnp.where(lax.broadcasted_iota(jnp.int32, c.shape, 0) < nb, c, -jnp.inf)
        cand.append(c)
        expert.append(r1[a:a + 1] + r2[:rows])
    cand.append(v1[k // 2:] + v2[0:1])
    expert.append(r1[k // 2:] + r2[0:1])
    return jnp.concatenate(cand, axis=0), jnp.concatenate(expert, axis=0)


def _topk_kernel(s_ref, i1_out, i2_out, g_out, i1_sc, i2_sc, g_sc):
    def head(h, carry):
        v1, r1, _ = _top_rows(s_ref[2 * h], P_TOPK)
        v2, r2, _ = _top_rows(s_ref[2 * h + 1], P_TOPK)
        cand, expert = _staircase(v1, r1, v2, r2)
        top, _, (ex,) = _top_rows(cand, P_TOPK, (expert,))
        e = jnp.exp(top - top[0:1])
        k1 = jnp.floor(ex * (1.0 / N_KEYS))
        rows = pl.ds(pl.multiple_of(h * P_TOPK, P_TOPK), P_TOPK)
        i1_sc[rows, :] = k1
        i2_sc[rows, :] = ex - k1 * float(N_KEYS)
        g_sc[rows, :] = e / jnp.sum(e, axis=0, keepdims=True)
        return carry

    lax.fori_loop(0, P_HEADS, head, 0)
    i1_out[...] = i1_sc[...].T
    i2_out[...] = i2_sc[...].T
    g_out[...] = g_sc[...].T


def _peer_topk(s_t):
    b, nhp, nk, l = s_t.shape
    tt = min(l, 256)
    hk = P_HEADS * P_TOPK
    return pl.pallas_call(
        _topk_kernel,
        grid=(b, l // tt),
        in_specs=[pl.BlockSpec((None, nhp, nk, tt), lambda bi, i: (bi, 0, 0, i))],
        out_specs=[pl.BlockSpec((None, tt, hk), lambda bi, i: (bi, i, 0))] * 3,
        out_shape=[jax.ShapeDtypeStruct((b, l, hk), F32)] * 3,
        scratch_shapes=[pltpu.VMEM((hk, tt), F32)] * 3,
        compiler_params=_cp("parallel", "parallel"),
        name="peer_topk",
    )(s_t)


BF16_BITS = 0xFFFF0000
DENSE_TOKEN_UNROLL = 64


def _dense_kernel(h_ref, i1_ref, i2_ref, g_ref, u_ref, v_ref, x_ref, mod_ref, o_ref, ws_ref, acc_ref, *, pitch):
    j = pl.program_id(2)
    tm = h_ref.shape[0]
    r = v_ref.shape[0] // N_KEYS
    half = N_KEYS // 2

    @pl.when(j == 0)
    def _():
        acc_ref[...] = jnp.zeros_like(acc_ref)
        sub = lax.broadcasted_iota(jnp.int32, (N_KEYS, i1_ref.shape[1]), 0).astype(F32).astype(BF16)
        one, zero = jnp.ones((), BF16), jnp.zeros((), BF16)

        def toks(tb, carry):
            for u in range(0, DENSE_TOKEN_UNROLL, 2):
                t = tb * DENSE_TOKEN_UNROLL + u
                pts, qts = [], []
                for k in range(2):
                    row = pl.ds(t + k, 1)
                    pts.append(jnp.where(i1_ref[row, :].astype(BF16) == sub, g_ref[row, :].astype(BF16), zero))
                    qts.append(jnp.where(i2_ref[row, :].astype(BF16) == sub, one, zero))
                z = jnp.zeros_like(qts[0])
                rhs = jnp.concatenate([jnp.concatenate([qts[0], z], axis=1),
                                       jnp.concatenate([z, qts[1]], axis=1)], axis=0)
                gates = _dot_nt(jnp.concatenate(pts, axis=1), rhs)
                for k in range(2):
                    gk = gates[:, k * N_KEYS:(k + 1) * N_KEYS].astype(BF16)
                    ws_ref[pl.ds(t + k, half, stride=pitch), :] = pltpu.bitcast(gk, jnp.uint32)
            return carry

        lax.fori_loop(0, tm // DENSE_TOKEN_UNROLL, toks, 0)

    a = jax.nn.gelu(_dot_nt(h_ref[...], u_ref[...]))
    cols = []
    for q in range(r // 2):
        word = ws_ref[pl.ds(pl.multiple_of((j * (r // 2) + q) * pitch, SUBLANES), tm), :]
        cols.append(lax.bitcast_convert_type(word << 16, F32))
        cols.append(lax.bitcast_convert_type(word & jnp.uint32(BF16_BITS), F32))
    acc_ref[...] += _dot((a * jnp.concatenate(cols, axis=1)).astype(BF16), v_ref[...])

    @pl.when(j == pl.num_programs(2) - 1)
    def _():
        o_ref[...] = x_ref[...] + mod_ref[5:6, :] * acc_ref[...]


def _peer_dense(h2, i1, i2, g, u_tab, v_tab, xn, mod):
    if mod.shape[0] == 1 and h2.shape[0] > 1:
        b, l, d = h2.shape
        flat = lambda a: a.reshape(1, b * l, a.shape[2])
        return _peer_dense(flat(h2), flat(i1), flat(i2), flat(g), u_tab, v_tab, flat(xn), mod).reshape(b, l, d)
    b, l, d = h2.shape
    tm = min(l, 512)
    te = 1024
    ne = v_tab.shape[0]
    hk = i1.shape[2]
    pitch = tm + SUBLANES
    pb = 1 if mod.shape[0] > 1 else 0
    assert tm % DENSE_TOKEN_UNROLL == 0 and (te // N_KEYS) % 2 == 0
    tok = lambda bi, i, j: (bi, i, 0)
    return pl.pallas_call(
        functools.partial(_dense_kernel, pitch=pitch),
        grid=(b, l // tm, ne // te),
        in_specs=[
            pl.BlockSpec((None, tm, d), tok),
            pl.BlockSpec((None, tm, hk), tok),
            pl.BlockSpec((None, tm, hk), tok),
            pl.BlockSpec((None, tm, hk), tok),
            pl.BlockSpec((te, d), lambda bi, i, j: (j, 0)),
            pl.BlockSpec((te, d), lambda bi, i, j: (j, 0)),
            pl.BlockSpec((None, tm, d), tok, pipeline_mode=pl.Buffered(1)),
            pl.BlockSpec((None, 6, d), lambda bi, i, j: (bi * pb, 0, 0)),
        ],
        out_specs=pl.BlockSpec((None, tm, d), tok),
        out_shape=jax.ShapeDtypeStruct((b, l, d), F32),
        scratch_shapes=[pltpu.VMEM((N_KEYS // 2 * pitch, N_KEYS), jnp.uint32), pltpu.VMEM((tm, d), F32)],
        compiler_params=pltpu.CompilerParams(dimension_semantics=("parallel", "parallel", "arbitrary"),
                                             vmem_limit_bytes=DENSE_VMEM_LIMIT_BYTES),
        name="peer_dense",
    )(h2, i1, i2, g, u_tab, v_tab, xn, mod)


def _final_kernel(x_ref, g_ref, o_ref):
    o_ref[...] = _rms(x_ref[...], g_ref[...])


def _final_norm(x, g):
    b, l, d = x.shape
    tm = min(l, 512)
    return pl.pallas_call(
        _final_kernel,
        grid=(b, l // tm),
        in_specs=[pl.BlockSpec((None, tm, d), lambda bi, i: (bi, i, 0)), _full((1, d))],
        out_specs=pl.BlockSpec((None, tm, d), lambda bi, i: (bi, i, 0)),
        out_shape=jax.ShapeDtypeStruct((b, l, d), F32),
        compiler_params=_cp("parallel", "parallel"),
        name="final_norm",
    )(x, g.reshape(1, d))


def _rope_tables(n_tok):
    rows = n_tok // GRID_W
    row = jnp.repeat(jnp.arange(rows, dtype=F32), GRID_W)
    col = jnp.tile(jnp.arange(GRID_W, dtype=F32), rows)
    n_freq = QK_ROPE // 4
    inv = 1.0 / (ROPE_BASE ** (jnp.arange(n_freq, dtype=F32) / n_freq))
    ang = jnp.concatenate([row[:, None] * inv, col[:, None] * inv], axis=-1)
    cos, sin = jnp.cos(ang), jnp.sin(ang)
    half = QK_ROPE // 2
    pad = LANES - QK_NOPE - QK_ROPE
    ones = jnp.ones((n_tok, QK_NOPE), F32)
    z = lambda w: jnp.zeros((n_tok, w), F32)
    c = jnp.concatenate([ones, cos, cos, z(pad)], axis=1)
    s1 = jnp.concatenate([z(QK_NOPE), -sin, z(half), z(pad)], axis=1)
    s2 = jnp.concatenate([z(QK_NOPE), z(half), sin, z(pad)], axis=1)
    return c, s1, s2


def _pad_heads(w, per_head):
    rows = w.shape[0]
    w = w.reshape(rows, N_HEADS, per_head)
    return jnp.pad(w, ((0, 0), (0, 0), (0, LANES - per_head))).reshape(rows, N_HEADS * LANES)


def _layer_weights(p, l):
    d = p["w_in"].shape[1]
    q_rank = p["q_norm"].shape[1]
    kv_rank = p["kv_norm"].shape[1]
    hy3 = p["hy_conv_w"].shape[2]
    w_in = p["w_in"][l]
    c0, c1, c2, c3 = q_rank, q_rank + kv_rank, q_rank + kv_rank + QK_ROPE, q_rank + kv_rank + QK_ROPE + hy3
    wkr = jnp.zeros((d, LANES), F32).at[:, QK_NOPE:QK_NOPE + QK_ROPE].set(w_in[:, c1:c2])
    ukv = p["w_ukv"][l].reshape(kv_rank, N_HEADS, QK_NOPE + V_DIM)
    wuk = _pad_heads(ukv[:, :, :QK_NOPE].reshape(kv_rank, -1), QK_NOPE)
    wuv = _pad_heads(ukv[:, :, QK_NOPE:].reshape(kv_rank, -1), V_DIM)
    woa = p["w_oa"][l].reshape(N_HEADS, V_DIM, d)
    woa = jnp.pad(woa, ((0, 0), (0, LANES - V_DIM), (0, 0))).reshape(N_HEADS * LANES, d)
    keys = p["peer_keys"][l]
    row = lambda v: v.reshape(1, -1)
    return dict(
        gn=row(p["norm_mix"][l]), wq=w_in[:, :c0].astype(BF16), wckv=w_in[:, c0:c1].astype(BF16),
        wkr=wkr.astype(BF16), why=w_in[:, c2:c3].astype(BF16), wg=w_in[:, c3:].astype(BF16),
        qn=row(p["q_norm"][l]), kvn=row(p["kv_norm"][l]),
        wuq=_pad_heads(p["w_uq"][l], QK_NOPE + QK_ROPE).astype(BF16),
        wuk=wuk.astype(BF16), wuv=wuv.astype(BF16),
        g2=row(p["norm_ffn"][l]), woa=woa.astype(BF16), wob=p["w_ob"][l].astype(BF16),
        wout=p["w_out"][l].astype(BF16), wpq=p["peer_wq"][l].astype(BF16),
        keys=keys.reshape(keys.shape[0] * keys.shape[1], keys.shape[2], keys.shape[3]).astype(BF16),
        u=p["peer_u"][l].astype(BF16), v=p["peer_v"][l].astype(BF16),
    )


def _layer(x, mod, lw, lp, rope_tabs, ctx_kv):
    q, k, cb, ckv, kr, hy_in, ga, gb = _inproj(x, mod, lw, rope_tabs)
    if ctx_kv is None:
        attn = _attention(q, k, cb, lw["wuv"])
    else:
        attn = _attention(q, k, cb, lw["wuv"], *ctx_kv)
    hy = _hyena(hy_in, lp)
    xn, h2, s_t = _merge(x, attn, hy, ga, gb, mod, lw)
    i1, i2, g = _peer_topk(s_t)
    x = _peer_dense(h2, i1, i2, g, lw["u"], lw["v"], xn, mod)
    return x, ckv, kr[:, :, QK_NOPE:QK_NOPE + QK_ROPE]


def kernel(x_prompt, x_sample, cache_ckv, cache_krope, c, c_ctx, w_ada, b_ada, norm_mix, norm_ffn,
           w_in, q_norm, kv_norm, w_uq, w_ukv, w_oa, w_ob, w_out, hy_conv_w, hy_conv_b,
           hy_w1, hy_b1, hy_w2, hy_b2, hy_w3, hy_decay, hy_bias,
           peer_wq, peer_keys, peer_u, peer_v, final_norm):
    p = dict(norm_mix=norm_mix, norm_ffn=norm_ffn, w_in=w_in, q_norm=q_norm, kv_norm=kv_norm, w_uq=w_uq,
             w_ukv=w_ukv, w_oa=w_oa, w_ob=w_ob, w_out=w_out, peer_wq=peer_wq, peer_keys=peer_keys,
             peer_u=peer_u, peer_v=peer_v, hy_conv_w=hy_conv_w)
    depth, d = norm_mix.shape
    bd = x_sample.shape[0]
    rows = -(-(bd + 1) // SUBLANES) * SUBLANES
    cc = jnp.zeros((rows, d), F32).at[:bd].set(c).at[bd].set(c_ctx)
    mods = _adaln(cc, w_ada, b_ada).reshape(depth, rows, 6, d)
    rope_tabs = _rope_tables(x_sample.shape[1])
    e_mat = jnp.zeros((QK_ROPE, N_HEADS, LANES), F32)
    e_mat = e_mat.at[:, :, QK_NOPE:QK_NOPE + QK_ROPE].set(jnp.eye(QK_ROPE, dtype=F32)[:, None, :])
    e_mat = e_mat.reshape(QK_ROPE, N_HEADS * LANES).astype(BF16)
    xp, xs = x_prompt, x_sample
    new_ckv, new_kr = [], []
    for l in range(depth):
        lw = _layer_weights(p, l)
        lp = dict(hy_conv_w=hy_conv_w[l], hy_conv_b=hy_conv_b[l], hy_w1=hy_w1[l], hy_b1=hy_b1[l],
                  hy_w2=hy_w2[l], hy_b2=hy_b2[l], hy_w3=hy_w3[l], hy_decay=hy_decay[l], hy_bias=hy_bias[l])
        xp, ckv_p, kr_p = _layer(xp, mods[l, bd:bd + 1], lw, lp, None, None)
        new_ckv.append(ckv_p)
        new_kr.append(kr_p)
        ctx_kv = _kvcache(cache_ckv, cache_krope, l, lw["wuk"], e_mat)
        xs, _, _ = _layer(xs, mods[l, :bd], lw, lp, rope_tabs, ctx_kv)
    return (_final_norm(xp, final_norm), _final_norm(xs, final_norm),
            jnp.stack(new_ckv, axis=1), jnp.stack(new_kr, axis=1))
```

```python
import functools
import math

import numpy as np
import jax
import jax.numpy as jnp
from jax import lax
from jax.experimental import pallas as pl
from jax.experimental.pallas import tpu as pltpu

F32 = jnp.float32
BF16 = jnp.bfloat16

GRID_W = 64
N_HEADS = 8
QK_NOPE = 64
QK_ROPE = 32
V_DIM = 64
ROPE_BASE = 10000.0
HY_ORDER = 2
HY_BANDS = 16
SHORT_K = 3
P_HEADS = 8
N_KEYS = 128
P_TOPK = 16
EPS = 1e-6

LANES = 128
SUBLANES = 8
FFT_B = 64
DIRECT_DFT_MAX_L = 256
VMEM_LIMIT_BYTES = 48 * 1024 * 1024
DENSE_VMEM_LIMIT_BYTES = 56 * 1024 * 1024


def _cp(*sem):
    return pltpu.CompilerParams(dimension_semantics=sem, vmem_limit_bytes=VMEM_LIMIT_BYTES)


def _full(shape):
    n = len(shape)
    return pl.BlockSpec(shape, lambda *_: (0,) * n)


def _tile(n, pref, unit=LANES):
    if n <= pref:
        return n
    return max(t for t in range(unit, pref + 1, unit) if n % t == 0)


def _dot(a, b):
    return jnp.dot(a, b, preferred_element_type=F32)


def _dot_hi(a, b):
    return jnp.dot(a, b, precision=lax.Precision.HIGHEST, preferred_element_type=F32)


def _dot_nt(a, b):
    return lax.dot_general(a, b, (((1,), (1,)), ((), ())), preferred_element_type=F32)


def _rms(x, g):
    return x * lax.rsqrt(jnp.mean(x * x, axis=-1, keepdims=True) + EPS) * g


def _adaln_kernel(c_ref, w_ref, b_ref, o_ref):
    c = c_ref[...]
    s = (c * jax.nn.sigmoid(c)).astype(BF16)
    o_ref[...] = _dot(s, w_ref[...].astype(BF16)) + b_ref[...]


def _adaln(cc, w_ada, b_ada):
    depth, d, d6 = w_ada.shape
    rows = cc.shape[0]
    tn = d6 // 4
    return pl.pallas_call(
        _adaln_kernel,
        grid=(depth, d6 // tn),
        in_specs=[
            _full((rows, d)),
            pl.BlockSpec((None, d, tn), lambda l, j: (l, 0, j)),
            pl.BlockSpec((None, 1, tn), lambda l, j: (l, 0, j)),
        ],
        out_specs=pl.BlockSpec((None, rows, tn), lambda l, j: (l, 0, j)),
        out_shape=jax.ShapeDtypeStruct((depth, rows, d6), F32),
        compiler_params=_cp("parallel", "parallel"),
        name="adaln",
    )(cc, w_ada, b_ada.reshape(depth, 1, d6))


ATTN_SCALE = 1.0 / math.sqrt(QK_NOPE + QK_ROPE)
ATTN_KEY_CHUNK = 1024
ATTN_HEADS_PER_STEP = 4


def _rope_slab(x, c, s1, s2):
    half = QK_ROPE // 2
    return x * c + pltpu.roll(x, LANES - half, 1) * s1 + pltpu.roll(x, half, 1) * s2


def _inproj_kernel(*refs, rope):
    if rope:
        (x_ref, mod_ref, gn_ref, wq_ref, wckv_ref, wkr_ref, why_ref, wg_ref, qn_ref, kvn_ref,
         wuq_ref, wuk_ref, c_ref, s1_ref, s2_ref,
         q_out, k_out, cb_out, ckv_out, kr_out, hy_out, ga_out, gb_out) = refs
    else:
        (x_ref, mod_ref, gn_ref, wq_ref, wckv_ref, wkr_ref, why_ref, wg_ref, qn_ref, kvn_ref,
         wuq_ref, wuk_ref,
         q_out, k_out, cb_out, ckv_out, kr_out, hy_out, ga_out, gb_out) = refs
    x = x_ref[...]
    mod = mod_ref[...]
    d = x.shape[1]
    h = (_rms(x, gn_ref[...]) * (1.0 + mod[1:2]) + mod[0:1]).astype(BF16)
    q = _dot(_rms(_dot(h, wq_ref[...]), qn_ref[...]).astype(BF16), wuq_ref[...])
    ckv = _rms(_dot(h, wckv_ref[...]), kvn_ref[...])
    kr = _dot(h, wkr_ref[...])
    ckv_b = ckv.astype(BF16)
    kn = _dot(ckv_b, wuk_ref[...])
    if rope:
        c, s1, s2 = c_ref[...], s1_ref[...], s2_ref[...]
        kr_k = _rope_slab(kr, c, s1, s2)
        q = jnp.concatenate(
            [_rope_slab(q[:, i * LANES:(i + 1) * LANES], c, s1, s2) for i in range(N_HEADS)], axis=1)
    else:
        kr_k = kr
    q_out[...] = (q * ATTN_SCALE).astype(BF16)
    k_out[...] = (kn + jnp.concatenate([kr_k] * N_HEADS, axis=1)).astype(BF16)
    cb_out[...] = ckv_b
    ckv_out[...] = ckv
    kr_out[...] = kr
    hy_out[...] = _dot(h, why_ref[...])
    g = _dot(h, wg_ref[...])
    ga_out[...] = jax.nn.sigmoid(g[:, :d])
    gb_out[...] = jax.nn.sigmoid(g[:, d:])


def _inproj(x, mod, lw, rope_tabs):
    b, l, d = x.shape
    tm = min(l, 256)
    pb = 1 if mod.shape[0] > 1 else 0
    hl = N_HEADS * LANES
    kvr = lw["wckv"].shape[1]
    hyw = lw["why"].shape[1]
    rope = rope_tabs is not None
    tok = lambda bi, i: (bi, i, 0)
    w_names = ["gn", "wq", "wckv", "wkr", "why", "wg", "qn", "kvn", "wuq", "wuk"]
    ins = [x, mod] + [lw[n] for n in w_names]
    in_specs = [pl.BlockSpec((None, tm, d), tok), pl.BlockSpec((None, 6, d), lambda bi, i: (bi * pb, 0, 0))]
    in_specs += [_full(lw[n].shape) for n in w_names]
    if rope:
        ins += list(rope_tabs)
        in_specs += [pl.BlockSpec((tm, LANES), lambda bi, i: (i, 0))] * 3
    widths = [(hl, BF16), (hl, BF16), (kvr, BF16), (kvr, F32), (LANES, F32), (hyw, F32), (d, F32), (d, F32)]
    return pl.pallas_call(
        functools.partial(_inproj_kernel, rope=rope),
        grid=(b, l // tm),
        in_specs=in_specs,
        out_specs=[pl.BlockSpec((None, tm, w), tok) for w, _ in widths],
        out_shape=[jax.ShapeDtypeStruct((b, l, w), dt) for w, dt in widths],
        compiler_params=_cp("parallel", "parallel"),
        name="inproj_rope" if rope else "inproj",
    )(*ins)


def _kvcache_kernel(ckv_ref, kr_ref, wuk_ref, e_ref, k_out, cb_out):
    ckv_b = ckv_ref[...].astype(BF16)
    k_out[...] = (_dot(ckv_b, wuk_ref[...]) + _dot(kr_ref[...].astype(BF16), e_ref[...])).astype(BF16)
    cb_out[...] = ckv_b


def _kvcache(cache_ckv, cache_kr, layer, wuk, e_mat):
    b, _, p, kvr = cache_ckv.shape
    hl = N_HEADS * LANES
    return pl.pallas_call(
        _kvcache_kernel,
        grid=(b,),
        in_specs=[
            pl.BlockSpec((None, None, p, kvr), lambda bi: (bi, layer, 0, 0)),
            pl.BlockSpec((None, None, p, QK_ROPE), lambda bi: (bi, layer, 0, 0)),
            _full(wuk.shape),
            _full(e_mat.shape),
        ],
        out_specs=[pl.BlockSpec((None, p, hl), lambda bi: (bi, 0, 0)),
                   pl.BlockSpec((None, p, kvr), lambda bi: (bi, 0, 0))],
        out_shape=[jax.ShapeDtypeStruct((b, p, hl), BF16), jax.ShapeDtypeStruct((b, p, kvr), BF16)],
        compiler_params=_cp("parallel"),
        name="kvcache",
    )(cache_ckv, cache_kr, wuk, e_mat)


def _attn_kernel(*refs, ctx):
    if ctx:
        q_ref, k_ref, c_ref, wuv_ref, kc_ref, cc_ref, o_ref = refs
    else:
        q_ref, k_ref, c_ref, wuv_ref, o_ref = refs
    chunks = [(kc_ref, cc_ref, 0, kc_ref.shape[0])] if ctx else []
    n_keys = k_ref.shape[0]
    ck = min(n_keys, ATTN_KEY_CHUNK)
    chunks += [(k_ref, c_ref, c * ck, ck) for c in range(n_keys // ck)]
    heads = [slice(i * LANES, (i + 1) * LANES) for i in range(ATTN_HEADS_PER_STEP)]
    qs = [q_ref[:, h] for h in heads]
    m = [None] * len(heads)
    acc = [None] * len(heads)
    den = [None] * len(heads)
    for kr, cr, start, size in chunks:
        for i, h in enumerate(heads):
            s = _dot_nt(qs[i], kr[pl.ds(start, size), h])
            cmax = jnp.max(s, axis=1, keepdims=True)
            if m[i] is None:
                m[i] = cmax
                p = jnp.exp(s - cmax)
                den[i] = jnp.sum(p, axis=1, keepdims=True)
                acc[i] = _dot(p.astype(BF16), cr[pl.ds(start, size), :])
            else:
                m_new = jnp.maximum(m[i], cmax)
                alpha = jnp.exp(m[i] - m_new)
                p = jnp.exp(s - m_new)
                den[i] = den[i] * alpha + jnp.sum(p, axis=1, keepdims=True)
                acc[i] = acc[i] * alpha + _dot(p.astype(BF16), cr[pl.ds(start, size), :])
                m[i] = m_new
    for i, h in enumerate(heads):
        o_ref[:, h] = _dot((acc[i] / den[i]).astype(BF16), wuv_ref[:, h]).astype(BF16)


def _attention(q, k, cb, wuv, kc=None, cc=None):
    b, l, hl = q.shape
    kvr = cb.shape[2]
    tq = min(l, 512)
    ctx = kc is not None
    hw = ATTN_HEADS_PER_STEP * LANES
    ins = [q, k, cb, wuv]
    in_specs = [
        pl.BlockSpec((None, tq, hw), lambda bi, h, i: (bi, i, h)),
        pl.BlockSpec((None, l, hw), lambda bi, h, i: (bi, 0, h)),
        pl.BlockSpec((None, l, kvr), lambda bi, h, i: (bi, 0, 0)),
        pl.BlockSpec((kvr, hw), lambda bi, h, i: (0, h)),
    ]
    if ctx:
        p = kc.shape[1]
        ins += [kc, cc]
        in_specs += [pl.BlockSpec((None, p, hw), lambda bi, h, i: (bi, 0, h)),
                     pl.BlockSpec((None, p, kvr), lambda bi, h, i: (bi, 0, 0))]
    return pl.pallas_call(
        functools.partial(_attn_kernel, ctx=ctx),
        grid=(b, N_HEADS // ATTN_HEADS_PER_STEP, l // tq),
        in_specs=in_specs,
        out_specs=pl.BlockSpec((None, tq, hw), lambda bi, h, i: (bi, i, h)),
        out_shape=jax.ShapeDtypeStruct((b, l, hl), BF16),
        compiler_params=_cp("parallel", "parallel", "parallel"),
        name="attn_ctx" if ctx else "attn",
    )(*ins)


def _shortconv_kernel(x_ref, w_ref, b_ref, o_ref):
    x = x_ref[...]
    n = x.shape[0]
    row = lax.broadcasted_iota(jnp.int32, x.shape, 0)
    xm = jnp.where(row == 0, 0.0, pltpu.roll(x, 1, 0))
    xp = jnp.where(row == n - 1, 0.0, pltpu.roll(x, n - 1, 0))
    w = w_ref[...]
    o_ref[...] = (xm * w[0:1] + x * w[1:2] + xp * w[2:3] + b_ref[...]).astype(BF16)


def _shortconv(x, w, bias):
    b, l, c = x.shape
    ct = _tile(c, 256)
    return pl.pallas_call(
        _shortconv_kernel,
        grid=(b, c // ct),
        in_specs=[
            pl.BlockSpec((None, l, ct), lambda bi, j: (bi, 0, j)),
            pl.BlockSpec((SHORT_K, ct), lambda bi, j: (0, j)),
            pl.BlockSpec((1, ct), lambda bi, j: (0, j)),
        ],
        out_specs=pl.BlockSpec((None, l, ct), lambda bi, j: (bi, 0, j)),
        out_shape=jax.ShapeDtypeStruct((b, l, c), BF16),
        compiler_params=_cp("parallel", "parallel"),
        name="shortconv",
    )(x, w, bias.reshape(1, c))


def _filter_kernel(w1_ref, b1_ref, w2_ref, b2_ref, w3_ref, dec_ref, o_ref, *, seq, width):
    tr = o_ref.shape[0]
    d = lax.broadcasted_iota(jnp.int32, (tr, LANES), 0) + pl.program_id(0) * tr
    t = d.astype(F32) / seq
    lane = lax.broadcasted_iota(jnp.int32, (tr, LANES), 1)
    band = jnp.where(lane <= HY_BANDS, lane, lane - HY_BANDS).astype(F32)
    ang = (2.0 * math.pi * t) * band
    z = jnp.where(lane == 0, t,
                  jnp.where(lane <= HY_BANDS, jnp.sin(ang),
                            jnp.where(lane <= 2 * HY_BANDS, jnp.cos(ang), 0.0)))
    f = jnp.sin(_dot_hi(z, w1_ref[...]) + b1_ref[...])
    f = jnp.sin(_dot_hi(f, w2_ref[...]) + b2_ref[...])
    o_ref[...] = _dot_hi(f, w3_ref[...]) * jnp.exp(-dec_ref[...] * t[:, :1])


def _hyena_filter(seq, w1, b1, w2, b2, w3, decay):
    width = decay.shape[-1]
    tr = min(seq, 512)
    hid = w2.shape[0]
    w1p = jnp.zeros((LANES, hid), F32).at[:w1.shape[0]].set(w1)
    ins = [w1p, b1.reshape(1, hid), w2, b2.reshape(1, hid), w3, decay.reshape(1, -1)]
    taps = pl.pallas_call(
        functools.partial(_filter_kernel, seq=seq, width=width),
        grid=(seq // tr,),
        in_specs=[_full(a.shape) for a in ins],
        out_specs=pl.BlockSpec((tr, 2 * HY_ORDER * width), lambda i: (i, 0)),
        out_shape=jax.ShapeDtypeStruct((seq, 2 * HY_ORDER * width), F32),
        compiler_params=_cp("parallel"),
        name="hyena_filter",
    )(*ins).reshape(seq, HY_ORDER, 2, width)
    return taps[:, :, 0].reshape(seq, HY_ORDER * width), taps[:, :, 1].reshape(seq, HY_ORDER * width)


def _direct_consts(seq):
    n = 2 * seq
    k = np.arange(n, dtype=np.float64)[:, None]
    pos = np.arange(n, dtype=np.float64)[None, :]
    ang = 2.0 * np.pi * k * pos / n
    fwd = np.concatenate([np.cos(ang), -np.sin(ang)], axis=0)
    inv = np.concatenate([np.cos(ang.T), -np.sin(ang.T)], axis=1)[:seq] / n
    return (jnp.asarray(fwd[:, :seq], F32), jnp.asarray(fwd[:, :seq], BF16), jnp.asarray(inv, BF16))


def _dfilt_kernel(tf_ref, tb_ref, f_ref, o_ref):
    n = f_ref.shape[0] // 2
    zf = _dot_hi(f_ref[...], tf_ref[...])
    zb = _dot_hi(f_ref[...], tb_ref[...])
    o_ref[:n, :] = zf[:n] + zb[:n]
    o_ref[n:, :] = zf[n:] - zb[n:]


def _direct_filter_spectrum(taps_f, taps_b, fwd_f32):
    seq, ch = taps_f.shape
    n2 = fwd_f32.shape[0]
    ct = _tile(ch, 256)
    return pl.pallas_call(
        _dfilt_kernel,
        grid=(ch // ct,),
        in_specs=[pl.BlockSpec((seq, ct), lambda j: (0, j))] * 2 + [_full(fwd_f32.shape)],
        out_specs=pl.BlockSpec((n2, ct), lambda j: (0, j)),
        out_shape=jax.ShapeDtypeStruct((n2, ch), F32),
        compiler_params=_cp("parallel"),
        name="filter_spectrum_direct",
    )(taps_f, taps_b, fwd_f32)


def _dconv_kernel(v_ref, x1_ref, x2_ref, kf_ref, bias_ref, f_ref, fi_ref, o_ref):
    n = f_ref.shape[0] // 2
    ct = o_ref.shape[1]
    gates = (x1_ref, x2_ref)
    y = v_ref[...].astype(F32)
    for o in range(HY_ORDER):
        z = _dot(f_ref[...], y.astype(BF16))
        zr, zi = z[:n], z[n:]
        kr = kf_ref[:n, o * ct:(o + 1) * ct]
        ki = kf_ref[n:, o * ct:(o + 1) * ct]
        prod = jnp.concatenate([zr * kr - zi * ki, zr * ki + zi * kr], axis=0).astype(BF16)
        y = gates[o][...].astype(F32) * (_dot(fi_ref[...], prod) + y * bias_ref[o:o + 1, :])
    o_ref[...] = y.astype(BF16)


def _hyena_direct(hs, kf, bias, fwd, inv):
    b, l, c3 = hs.shape
    w = c3 // (HY_ORDER + 1)
    ct = _tile(w, 256)
    nt = w // ct
    return pl.pallas_call(
        _dconv_kernel,
        grid=(nt, b),
        in_specs=[
            pl.BlockSpec((None, l, ct), lambda j, bi: (bi, 0, j)),
            pl.BlockSpec((None, l, ct), lambda j, bi: (bi, 0, nt + j)),
            pl.BlockSpec((None, l, ct), lambda j, bi: (bi, 0, 2 * nt + j)),
            pl.BlockSpec((kf.shape[0], HY_ORDER * ct), lambda j, bi: (0, j)),
            pl.BlockSpec((HY_ORDER, ct), lambda j, bi: (0, j)),
            _full(fwd.shape),
            _full(inv.shape),
        ],
        out_specs=pl.BlockSpec((None, l, ct), lambda j, bi: (bi, 0, j)),
        out_shape=jax.ShapeDtypeStruct((b, l, w), BF16),
        compiler_params=_cp("parallel", "parallel"),
        name="hyena_conv_direct",
    )(hs, hs, hs, kf, bias, fwd, inv)


def _to_lanes(x):
    return jnp.swapaxes(x, -1, -2).reshape(*x.shape[:-2], x.shape[-1] * x.shape[-2])


def _two_stage_consts(seq):
    n = 2 * seq
    n1 = n // FFT_B
    a_in = n1 // 2
    k1 = np.arange(n1, dtype=np.float64)[:, None]
    a = np.arange(n1, dtype=np.float64)[None, :]
    ang1 = 2.0 * np.pi * k1 * a / n1
    f1_full = np.concatenate([np.cos(ang1), -np.sin(ang1)], axis=0)
    f1_inv = np.concatenate([np.cos(ang1.T), -np.sin(ang1.T)], axis=1)[:a_in] / n
    bb = np.arange(FFT_B, dtype=np.float64)[None, :]
    angt = 2.0 * np.pi * k1 * bb / n
    rep = LANES // FFT_B
    tw = np.stack([np.tile(np.cos(angt), (1, rep)), np.tile(-np.sin(angt), (1, rep))])
    b_ = np.arange(FFT_B, dtype=np.float64)[:, None]
    k2 = np.arange(FFT_B, dtype=np.float64)[None, :]
    ang2 = 2.0 * np.pi * b_ * k2 / FFT_B
    bd = lambda m: np.kron(np.eye(rep), m)
    cr, ci = np.cos(ang2), -np.sin(ang2)
    w2 = np.block([[bd(cr), bd(ci)], [bd(-ci), bd(cr)]])
    er, ei = np.cos(ang2), np.sin(ang2)
    w2i = np.block([[bd(er), bd(ei)], [bd(-ei), bd(er)]])
    return dict(
        f1_f32=jnp.asarray(f1_full[:, :a_in], F32), f1=jnp.asarray(f1_full[:, :a_in], BF16),
        f1_inv=jnp.asarray(f1_inv, BF16), tw=jnp.asarray(tw, F32),
        w2=jnp.asarray(w2, BF16), w2_f32=jnp.asarray(w2, F32), w2i=jnp.asarray(w2i, BF16))


def _fft_fwd(x, f1, tw_r, tw_i, w2, dot, cast):
    n1 = tw_r.shape[0]
    z = dot(f1, cast(x))
    zr, zi = z[:n1], z[n1:]
    rows = []
    for g in range(x.shape[1] // LANES):
        a = zr[:, g * LANES:(g + 1) * LANES]
        b = zi[:, g * LANES:(g + 1) * LANES]
        rows.append(jnp.concatenate([a * tw_r - b * tw_i, a * tw_i + b * tw_r], axis=1))
    return dot(cast(jnp.concatenate(rows, axis=0)), w2)


def _fft_inv(yf, w2i, tw_r, tw_i, f1_inv):
    n1 = tw_r.shape[0]
    gm = _dot(yf.astype(BF16), w2i)
    cols = []
    for g in range(yf.shape[0] // n1):
        blk = gm[g * n1:(g + 1) * n1]
        gr, gi = blk[:, :LANES], blk[:, LANES:]
        cols.append(jnp.concatenate([gr * tw_r + gi * tw_i, gi * tw_r - gr * tw_i], axis=0))
    return _dot(f1_inv, jnp.concatenate(cols, axis=1).astype(BF16))


def _ffilt_kernel(tf_ref, tb_ref, f1_ref, tw_ref, w2_ref, o_ref):
    xf, xb = (_fft_fwd(t[...], f1_ref[...], tw_ref[0], tw_ref[1], w2_ref[...], _dot_hi, lambda v: v)
              for t in (tf_ref, tb_ref))
    x = jnp.concatenate([xf[:, :LANES] + xb[:, :LANES], xf[:, LANES:] - xb[:, LANES:]], axis=1)
    o_ref[...] = x.reshape(o_ref.shape)


def _two_stage_filter_spectrum(taps_f, taps_b, cs, groups_per_tile):
    a_in, lanes = taps_f.shape
    n1 = cs["tw"].shape[1]
    wt = groups_per_tile * LANES
    ng = lanes // LANES
    return pl.pallas_call(
        _ffilt_kernel,
        grid=(lanes // wt,),
        in_specs=[pl.BlockSpec((a_in, wt), lambda j: (0, j))] * 2 + [
            _full(cs["f1_f32"].shape), _full(cs["tw"].shape), _full(cs["w2_f32"].shape)],
        out_specs=pl.BlockSpec((groups_per_tile, n1, 2 * LANES), lambda j: (j, 0, 0)),
        out_shape=jax.ShapeDtypeStruct((ng, n1, 2 * LANES), F32),
        compiler_params=_cp("parallel"),
        name="filter_spectrum_two_stage",
    )(taps_f, taps_b, cs["f1_f32"], cs["tw"], cs["w2_f32"])


def _fconv_kernel(v_ref, x1_ref, x2_ref, kf_ref, bias_ref, f1_ref, tw_ref, w2_ref, w2i_ref, f1i_ref, o_ref):
    tw_r, tw_i = tw_ref[0], tw_ref[1]
    gates = (x1_ref, x2_ref)
    y = v_ref[...].astype(F32)
    for o in range(HY_ORDER):
        x = _fft_fwd(y, f1_ref[...], tw_r, tw_i, w2_ref[...], _dot, lambda v: v.astype(BF16))
        kf = kf_ref[o]
        kf = kf.reshape(kf.shape[0] * kf.shape[1], kf.shape[2])
        xr, xi = x[:, :LANES], x[:, LANES:]
        kr, ki = kf[:, :LANES], kf[:, LANES:]
        prod = jnp.concatenate([xr * kr - xi * ki, xr * ki + xi * kr], axis=1)
        conv = _fft_inv(prod, w2i_ref[...], tw_r, tw_i, f1i_ref[...])
        y = gates[o][...].astype(F32) * (conv + y * bias_ref[o:o + 1, :])
    o_ref[...] = y.astype(BF16)


def _hyena_two_stage(hs_t, kf, bias_t, cs, width):
    b, a_in, _ = hs_t.shape
    cg = min(width, 32)
    wt = cg * FFT_B
    gpt = wt // LANES
    nt = width // cg
    n1 = kf.shape[2]
    consts = [cs["f1"], cs["tw"], cs["w2"], cs["w2i"], cs["f1_inv"]]
    return pl.pallas_call(
        _fconv_kernel,
        grid=(nt, b),
        in_specs=[
            pl.BlockSpec((None, a_in, wt), lambda j, bi: (bi, 0, j)),
            pl.BlockSpec((None, a_in, wt), lambda j, bi: (bi, 0, nt + j)),
            pl.BlockSpec((None, a_in, wt), lambda j, bi: (bi, 0, 2 * nt + j)),
            pl.BlockSpec((HY_ORDER, gpt, n1, 2 * LANES), lambda j, bi: (0, j, 0, 0)),
            pl.BlockSpec((HY_ORDER, wt), lambda j, bi: (0, j)),
        ] + [_full(c.shape) for c in consts],
        out_specs=pl.BlockSpec((None, a_in, wt), lambda j, bi: (bi, 0, j)),
        out_shape=jax.ShapeDtypeStruct((b, a_in, width * FFT_B), BF16),
        compiler_params=_cp("parallel", "parallel"),
        name="hyena_conv_two_stage",
    )(hs_t, hs_t, hs_t, kf, bias_t, *consts)


def _hyena(hy_in, lp):
    b, l, c3 = hy_in.shape
    width = c3 // (HY_ORDER + 1)
    hs = _shortconv(hy_in, lp["hy_conv_w"], lp["hy_conv_b"])
    taps_f, taps_b = _hyena_filter(l, lp["hy_w1"], lp["hy_b1"], lp["hy_w2"], lp["hy_b2"], lp["hy_w3"],
                                   lp["hy_decay"])
    n = 2 * l
    if l <= DIRECT_DFT_MAX_L:
        fwd_f32, fwd, inv = _direct_consts(l)
        ct = _tile(width, 256)
        kf = _direct_filter_spectrum(taps_f, taps_b, fwd_f32)
        kf = kf.reshape(2 * n, HY_ORDER, width // ct, ct).transpose(0, 2, 1, 3).reshape(2 * n, HY_ORDER * width)
        return _hyena_direct(hs, kf, lp["hy_bias"], fwd, inv)
    cs = _two_stage_consts(l)
    n1 = n // FFT_B
    a_in = l // FFT_B
    ow = HY_ORDER * width
    kf = _two_stage_filter_spectrum(_to_lanes(taps_f.reshape(a_in, FFT_B, ow)),
                                    _to_lanes(taps_b.reshape(a_in, FFT_B, ow)), cs, min(ow * FFT_B // LANES, 16))
    kf = kf.reshape(HY_ORDER, width * FFT_B // LANES, n1, 2 * LANES)
    hs_t = _to_lanes(hs.reshape(b, a_in, FFT_B, c3))
    bias_t = jnp.repeat(lp["hy_bias"], FFT_B, axis=1)
    hy_t = _hyena_two_stage(hs_t, kf, bias_t, cs, width)
    return jnp.swapaxes(hy_t.reshape(b, a_in, width, FFT_B), -1, -2).reshape(b, l, width)


def _merge_kernel(x_ref, o_ref, hy_ref, ga_ref, gb_ref, mod_ref, g2_ref, woa_ref, wob_ref, wout_ref,
                  wpq_ref, keys_ref, xn_out, h2_out, s_out):
    mod = mod_ref[...]
    merged = ga_ref[...] * _dot(o_ref[...], woa_ref[...]) + gb_ref[...] * _dot(hy_ref[...], wob_ref[...])
    xn = x_ref[...] + mod[2:3] * _dot(merged.astype(BF16), wout_ref[...])
    xn_out[...] = xn
    h2 = (_rms(xn, g2_ref[...]) * (1.0 + mod[4:5]) + mod[3:4]).astype(BF16)
    h2_out[...] = h2
    q = _dot(h2, wpq_ref[...]).astype(BF16)
    half = keys_ref.shape[2]
    for hp in range(keys_ref.shape[0]):
        s_out[hp] = _dot_nt(keys_ref[hp], q[:, hp * half:(hp + 1) * half])


def _merge(x, attn, hy, ga, gb, mod, lw):
    b, l, d = x.shape
    tm = min(l, 256)
    pb = 1 if mod.shape[0] > 1 else 0
    tok = lambda bi, i: (bi, i, 0)
    w_names = ["g2", "woa", "wob", "wout", "wpq", "keys"]
    nhp = lw["keys"].shape[0]
    return pl.pallas_call(
        _merge_kernel,
        grid=(b, l // tm),
        in_specs=[
            pl.BlockSpec((None, tm, d), tok),
            pl.BlockSpec((None, tm, attn.shape[2]), tok),
            pl.BlockSpec((None, tm, hy.shape[2]), tok),
            pl.BlockSpec((None, tm, d), tok),
            pl.BlockSpec((None, tm, d), tok),
            pl.BlockSpec((None, 6, d), lambda bi, i: (bi * pb, 0, 0)),
        ] + [_full(lw[n].shape) for n in w_names],
        out_specs=[
            pl.BlockSpec((None, tm, d), tok),
            pl.BlockSpec((None, tm, d), tok),
            pl.BlockSpec((None, nhp, N_KEYS, tm), lambda bi, i: (bi, 0, 0, i)),
        ],
        out_shape=[
            jax.ShapeDtypeStruct((b, l, d), F32),
            jax.ShapeDtypeStruct((b, l, d), BF16),
            jax.ShapeDtypeStruct((b, nhp, N_KEYS, l), F32),
        ],
        compiler_params=_cp("parallel", "parallel"),
        name="merge_peer_query",
    )(x, attn, hy, ga, gb, mod, *[lw[n] for n in w_names])


def _top_rows(s, k, payloads=()):
    r = s.shape[0]
    rid = lax.broadcasted_iota(jnp.int32, s.shape, 0).astype(F32)
    vals, idxs, picked = [], [], [[] for _ in payloads]
    for _ in range(k):
        m = jnp.max(s, axis=0, keepdims=True)
        ix = jnp.min(jnp.where(s == m, rid, float(r)), axis=0, keepdims=True)
        hit = rid == ix
        vals.append(m)
        idxs.append(ix)
        for p, acc in zip(payloads, picked):
            acc.append(jnp.sum(jnp.where(hit, p, 0.0), axis=0, keepdims=True))
        s = jnp.where(hit, -jnp.inf, s)
    cat = lambda rows: jnp.concatenate(rows, axis=0)
    return cat(vals), cat(idxs), [cat(a) for a in picked]


def _staircase(v1, r1, v2, r2):
    k = v1.shape[0]
    r1 = r1 * float(N_KEYS)
    cand, expert = [], []
    for a in range(k // 2):
        nb = k // (a + 1)
        rows = -(-nb // SUBLANES) * SUBLANES
        c = v1[a:a + 1] + v2[:rows]
        if nb < rows:
            c = jnp.where(lax.broadcasted_iota(jnp.int32, c.shape, 0) < nb, c, -jnp.inf)
        cand.append(c)
        expert.append(r1[a:a + 1] + r2[:rows])
    cand.append(v1[k // 2:] + v2[0:1])
    expert.append(r1[k // 2:] + r2[0:1])
    return jnp.concatenate(cand, axis=0), jnp.concatenate(expert, axis=0)


def _topk_kernel(s_ref, i1_out, i2_out, g_out, i1_sc, i2_sc, g_sc):
    def head(h, carry):
        v1, r1, _ = _top_rows(s_ref[2 * h], P_TOPK)
        v2, r2, _ = _top_rows(s_ref[2 * h + 1], P_TOPK)
        cand, expert = _staircase(v1, r1, v2, r2)
        top, _, (ex,) = _top_rows(cand, P_TOPK, (expert,))
        e = jnp.exp(top - top[0:1])
        k1 = jnp.floor(ex * (1.0 / N_KEYS))
        rows = pl.ds(pl.multiple_of(h * P_TOPK, P_TOPK), P_TOPK)
        i1_sc[rows, :] = k1
        i2_sc[rows, :] = ex - k1 * float(N_KEYS)
        g_sc[rows, :] = e / jnp.sum(e, axis=0, keepdims=True)
        return carry

    lax.fori_loop(0, P_HEADS, head, 0)
    i1_out[...] = i1_sc[...].T
    i2_out[...] = i2_sc[...].T
    g_out[...] = g_sc[...].T


def _peer_topk(s_t):
    b, nhp, nk, l = s_t.shape
    tt = min(l, 256)
    hk = P_HEADS * P_TOPK
    return pl.pallas_call(
        _topk_kernel,
        grid=(b, l // tt),
        in_specs=[pl.BlockSpec((None, nhp, nk, tt), lambda bi, i: (bi, 0, 0, i))],
        out_specs=[pl.BlockSpec((None, tt, hk), lambda bi, i: (bi, i, 0))] * 3,
        out_shape=[jax.ShapeDtypeStruct((b, l, hk), F32)] * 3,
        scratch_shapes=[pltpu.VMEM((hk, tt), F32)] * 3,
        compiler_params=_cp("parallel", "parallel"),
        name="peer_topk",
    )(s_t)


BF16_BITS = 0xFFFF0000
DENSE_TOKEN_UNROLL = 64


def _dense_kernel(h_ref, i1_ref, i2_ref, g_ref, u_ref, v_ref, x_ref, mod_ref, o_ref, ws_ref, acc_ref, *, pitch):
    j = pl.program_id(2)
    tm = h_ref.shape[0]
    r = v_ref.shape[0] // N_KEYS
    half = N_KEYS // 2

    @pl.when(j == 0)
    def _():
        acc_ref[...] = jnp.zeros_like(acc_ref)
        sub = lax.broadcasted_iota(jnp.int32, (N_KEYS, i1_ref.shape[1]), 0).astype(F32).astype(BF16)
        one, zero = jnp.ones((), BF16), jnp.zeros((), BF16)

        def toks(tb, carry):
            for u in range(0, DENSE_TOKEN_UNROLL, 2):
                t = tb * DENSE_TOKEN_UNROLL + u
                pts, qts = [], []
                for k in range(2):
                    row = pl.ds(t + k, 1)
                    pts.append(jnp.where(i1_ref[row, :].astype(BF16) == sub, g_ref[row, :].astype(BF16), zero))
                    qts.append(jnp.where(i2_ref[row, :].astype(BF16) == sub, one, zero))
                z = jnp.zeros_like(qts[0])
                rhs = jnp.concatenate([jnp.concatenate([qts[0], z], axis=1),
                                       jnp.concatenate([z, qts[1]], axis=1)], axis=0)
                gates = _dot_nt(jnp.concatenate(pts, axis=1), rhs)
                for k in range(2):
                    gk = gates[:, k * N_KEYS:(k + 1) * N_KEYS].astype(jnp.bfloat16)
                    ws_ref[pl.ds(t + k, half, stride=pitch), :] = pltpu.bitcast(gk, jnp.uint32)
            return carry

        lax.fori_loop(0, tm // DENSE_TOKEN_UNROLL, toks, 0)

    a = jax.nn.gelu(_dot_nt(h_ref[...], u_ref[...]))
    cols = []
    for q in range(r // 2):
        word = ws_ref[pl.ds(pl.multiple_of((j * (r // 2) + q) * pitch, SUBLANES), tm), :]
        cols.append(lax.bitcast_convert_type(word << 16, F32))
        cols.append(lax.bitcast_convert_type(word & jnp.uint32(BF16_BITS), F32))
    acc_ref[...] += _dot((a * jnp.concatenate(cols, axis=1)).astype(BF16), v_ref[...])

    @pl.when(j == pl.num_programs(2) - 1)
    def _():
        o_ref[...] = x_ref[...] + mod_ref[5:6, :] * acc_ref[...]


def _peer_dense(h2, i1, i2, g, u_tab, v_tab, xn, mod):
    if mod.shape[0] == 1 and h2.shape[0] > 1:
        b, l, d = h2.shape
        flat = lambda a: a.reshape(1, b * l, a.shape[2])
        return _peer_dense(flat(h2), flat(i1), flat(i2), flat(g), u_tab, v_tab, flat(xn), mod).reshape(b, l, d)
    b, l, d = h2.shape
    tm = min(l, 512)
    te = 1024
    ne = v_tab.shape[0]
    hk = i1.shape[2]
    pitch = tm + SUBLANES
    pb = 1 if mod.shape[0] > 1 else 0
    assert tm % DENSE_TOKEN_UNROLL == 0 and (te // N_KEYS) % 2 == 0
    tok = lambda bi, i, j: (bi, i, 0)
    return pl.pallas_call(
        functools.partial(_dense_kernel, pitch=pitch),
        grid=(b, l // tm, ne // te),
        in_specs=[
            pl.BlockSpec((None, tm, d), tok),
            pl.BlockSpec((None, tm, hk), tok),
            pl.BlockSpec((None, tm, hk), tok),
            pl.BlockSpec((None, tm, hk), tok),
            pl.BlockSpec((te, d), lambda bi, i, j: (j, 0)),
            pl.BlockSpec((te, d), lambda bi, i, j: (j, 0)),
            pl.BlockSpec((None, tm, d), tok, pipeline_mode=pl.Buffered(1)),
            pl.BlockSpec((None, 6, d), lambda bi, i, j: (bi * pb, 0, 0)),
        ],
        out_specs=pl.BlockSpec((None, tm, d), tok),
        out_shape=jax.ShapeDtypeStruct((b, l, d), F32),
        scratch_shapes=[pltpu.VMEM((N_KEYS // 2 * pitch, N_KEYS), jnp.uint32), pltpu.VMEM((tm, d), F32)],
        compiler_params=pltpu.CompilerParams(dimension_semantics=("parallel", "parallel", "arbitrary"),
                                             vmem_limit_bytes=DENSE_VMEM_LIMIT_BYTES),
        name="peer_dense",
    )(h2, i1, i2, g, u_tab, v_tab, xn, mod)


def _final_kernel(x_ref, g_ref, o_ref):
    o_ref[...] = _rms(x_ref[...], g_ref[...])


def _final_norm(x, g):
    b, l, d = x.shape
    tm = min(l, 512)
    return pl.pallas_call(
        _final_kernel,
        grid=(b, l // tm),
        in_specs=[pl.BlockSpec((None, tm, d), lambda bi, i: (bi, i, 0)), _full((1, d))],
        out_specs=pl.BlockSpec((None, tm, d), lambda bi, i: (bi, i, 0)),
        out_shape=jax.ShapeDtypeStruct((b, l, d), F32),
        compiler_params=_cp("parallel", "parallel"),
        name="final_norm",
    )(x, g.reshape(1, d))


def _rope_tables(n_tok):
    rows = n_tok // GRID_W
    row = jnp.repeat(jnp.arange(rows, dtype=F32), GRID_W)
    col = jnp.tile(jnp.arange(GRID_W, dtype=F32), rows)
    n_freq = QK_ROPE // 4
    inv = 1.0 / (ROPE_BASE ** (jnp.arange(n_freq, dtype=F32) / n_freq))
    ang = jnp.concatenate([row[:, None] * inv, col[:, None] * inv], axis=-1)
    cos, sin = jnp.cos(ang), jnp.sin(ang)
    half = QK_ROPE // 2
    pad = LANES - QK_NOPE - QK_ROPE
    ones = jnp.ones((n_tok, QK_NOPE), F32)
    z = lambda w: jnp.zeros((n_tok, w), F32)
    c = jnp.concatenate([ones, cos, cos, z(pad)], axis=1)
    s1 = jnp.concatenate([z(QK_NOPE), -sin, z(half), z(pad)], axis=1)
    s2 = jnp.concatenate([z(QK_NOPE), z(half), sin, z(pad)], axis=1)
    return c, s1, s2


def _pad_heads(w, per_head):
    rows = w.shape[0]
    w = w.reshape(rows, N_HEADS, per_head)
    return jnp.pad(w, ((0, 0), (0, 0), (0, LANES - per_head))).reshape(rows, N_HEADS * LANES)


def _layer_weights(p, l):
    d = p["w_in"].shape[1]
    q_rank = p["q_norm"].shape[1]
    kv_rank = p["kv_norm"].shape[1]
    hy3 = p["hy_conv_w"].shape[2]
    w_in = p["w_in"][l]
    c0, c1, c2, c3 = q_rank, q_rank + kv_rank, q_rank + kv_rank + QK_ROPE, q_rank + kv_rank + QK_ROPE + hy3
    wkr = jnp.zeros((d, LANES), F32).at[:, QK_NOPE:QK_NOPE + QK_ROPE].set(w_in[:, c1:c2])
    ukv = p["w_ukv"][l].reshape(kv_rank, N_HEADS, QK_NOPE + V_DIM)
    wuk = _pad_heads(ukv[:, :, :QK_NOPE].reshape(kv_rank, -1), QK_NOPE)
    wuv = _pad_heads(ukv[:, :, QK_NOPE:].reshape(kv_rank, -1), V_DIM)
    woa = p["w_oa"][l].reshape(N_HEADS, V_DIM, d)
    woa = jnp.pad(woa, ((0, 0), (0, LANES - V_DIM), (0, 0))).reshape(N_HEADS * LANES, d)
    keys = p["peer_keys"][l]
    row = lambda v: v.reshape(1, -1)
    return dict(
        gn=row(p["norm_mix"][l]), wq=w_in[:, :c0].astype(BF16), wckv=w_in[:, c0:c1].astype(BF16),
        wkr=wkr.astype(BF16), why=w_in[:, c2:c3].astype(BF16), wg=w_in[:, c3:].astype(BF16),
        qn=row(p["q_norm"][l]), kvn=row(p["kv_norm"][l]),
        wuq=_pad_heads(p["w_uq"][l], QK_NOPE + QK_ROPE).astype(BF16),
        wuk=wuk.astype(BF16), wuv=wuv.astype(BF16),
        g2=row(p["norm_ffn"][l]), woa=woa.astype(BF16), wob=p["w_ob"][l].astype(BF16),
        wout=p["w_out"][l].astype(BF16), wpq=p["peer_wq"][l].astype(BF16),
        keys=keys.reshape(keys.shape[0] * keys.shape[1], keys.shape[2], keys.shape[3]).astype(BF16),
        u=p["peer_u"][l].astype(BF16), v=p["peer_v"][l].astype(BF16),
    )


def _layer(x, mod, lw, lp, rope_tabs, ctx_kv):
    q, k, cb, ckv, kr, hy_in, ga, gb = _inproj(x, mod, lw, rope_tabs)
    if ctx_kv is None:
        attn = _attention(q, k, cb, lw["wuv"])
    else:
        attn = _attention(q, k, cb, lw["wuv"], *ctx_kv)
    hy = _hyena(hy_in, lp)
    xn, h2, s_t = _merge(x, attn, hy, ga, gb, mod, lw)
    i1, i2, g = _peer_topk(s_t)
    x = _peer_dense(h2, i1, i2, g, lw["u"], lw["v"], xn, mod)
    return x, ckv, kr[:, :, QK_NOPE:QK_NOPE + QK_ROPE]


def kernel(x_prompt, x_sample, cache_ckv, cache_krope, c, c_ctx, w_ada, b_ada, norm_mix, norm_ffn,
           w_in, q_norm, kv_norm, w_uq, w_ukv, w_oa, w_ob, w_out, hy_conv_w, hy_conv_b,
           hy_w1, hy_b1, hy_w2, hy_b2, hy_w3, hy_decay, hy_bias,
           peer_wq, peer_keys, peer_u, peer_v, final_norm):
    p = dict(norm_mix=norm_mix, norm_ffn=norm_ffn, w_in=w_in, q_norm=q_norm, kv_norm=kv_norm, w_uq=w_uq,
             w_ukv=w_ukv, w_oa=w_oa, w_ob=w_ob, w_out=w_out, peer_wq=peer_wq, peer_keys=peer_keys,
             peer_u=peer_u, peer_v=peer_v, hy_conv_w=hy_conv_w)
    depth, d = norm_mix.shape
    bd = x_sample.shape[0]
    rows = -(-(bd + 1) // SUBLANES) * SUBLANES
    cc = jnp.zeros((rows, d), F32).at[:bd].set(c).at[bd].set(c_ctx)
    mods = _adaln(cc, w_ada, b_ada).reshape(depth, rows, 6, d)
    rope_tabs = _rope_tables(x_sample.shape[1])
    e_mat = jnp.zeros((QK_ROPE, N_HEADS, LANES), F32)
    e_mat = e_mat.at[:, :, QK_NOPE:QK_NOPE + QK_ROPE].set(jnp.eye(QK_ROPE, dtype=F32)[:, None, :])
    e_mat = e_mat.reshape(QK_ROPE, N_HEADS * LANES).astype(BF16)
    xp, xs = x_prompt, x_sample
    new_ckv, new_kr = [], []
    for l in range(depth):
        lw = _layer_weights(p, l)
        lp = dict(hy_conv_w=hy_conv_w[l], hy_conv_b=hy_conv_b[l], hy_w1=hy_w1[l], hy_b1=hy_b1[l],
                  hy_w2=hy_w2[l], hy_b2=hy_b2[l], hy_w3=hy_w3[l], hy_decay=hy_decay[l], hy_bias=hy_bias[l])
        xp, ckv_p, kr_p = _layer(xp, mods[l, bd:bd + 1], lw, lp, None, None)
        new_ckv.append(ckv_p)
        new_kr.append(kr_p)
        ctx_kv = _kvcache(cache_ckv, cache_krope, l, lw["wuk"], e_mat)
        xs, _, _ = _layer(xs, mods[l, :bd], lw, lp, rope_tabs, ctx_kv)
    return (_final_norm(xp, final_norm), _final_norm(xs, final_norm),
            jnp.stack(new_ckv, axis=1), jnp.stack(new_kr, axis=1))
```

```python
import functools
import math

import numpy as np
import jax
import jax.numpy as jnp
from jax import lax
from jax.experimental import pallas as pl
from jax.experimental.pallas import tpu as pltpu

F32 = jnp.float32
BF16 = jnp.bfloat16

GRID_W = 64
N_HEADS = 8
QK_NOPE = 64
QK_ROPE = 32
V_DIM = 64
ROPE_BASE = 10000.0
HY_ORDER = 2
HY_BANDS = 16
SHORT_K = 3
P_HEADS = 8
N_KEYS = 128
P_TOPK = 16
EPS = 1e-6

LANES = 128
SUBLANES = 8
FFT_B = 64
DIRECT_DFT_MAX_L = 256
VMEM_LIMIT_BYTES = 48 * 1024 * 1024
DENSE_VMEM_LIMIT_BYTES = 56 * 1024 * 1024


def _cp(*sem):
    return pltpu.CompilerParams(dimension_semantics=sem, vmem_limit_bytes=VMEM_LIMIT_BYTES)


def _full(shape):
    n = len(shape)
    return pl.BlockSpec(shape, lambda *_: (0,) * n)


def _tile(n, pref, unit=LANES):
    if n <= pref:
        return n
    return max(t for t in range(unit, pref + 1, unit) if n % t == 0)


def _dot(a, b):
    return jnp.dot(a, b, preferred_element_type=F32)


def _dot_hi(a, b):
    return jnp.dot(a, b, precision=lax.Precision.HIGHEST, preferred_element_type=F32)


def _dot_nt(a, b):
    return lax.dot_general(a, b, (((1,), (1,)), ((), ())), preferred_element_type=F32)


def _rms(x, g):
    return x * lax.rsqrt(jnp.mean(x * x, axis=-1, keepdims=True) + EPS) * g


def _adaln_kernel(c_ref, w_ref, b_ref, o_ref):
    c = c_ref[...]
    s = (c * jax.nn.sigmoid(c)).astype(BF16)
    o_ref[...] = _dot(s, w_ref[...].astype(BF16)) + b_ref[...]


def _adaln(cc, w_ada, b_ada):
    depth, d, d6 = w_ada.shape
    rows = cc.shape[0]
    tn = d6 // 4
    return pl.pallas_call(
        _adaln_kernel,
        grid=(depth, d6 // tn),
        in_specs=[
            _full((rows, d)),
            pl.BlockSpec((None, d, tn), lambda l, j: (l, 0, j)),
            pl.BlockSpec((None, 1, tn), lambda l, j: (l, 0, j)),
        ],
        out_specs=pl.BlockSpec((None, rows, tn), lambda l, j: (l, 0, j)),
        out_shape=jax.ShapeDtypeStruct((depth, rows, d6), F32),
        compiler_params=_cp("parallel", "parallel"),
        name="adaln",
    )(cc, w_ada, b_ada.reshape(depth, 1, d6))


ATTN_SCALE = 1.0 / math.sqrt(QK_NOPE + QK_ROPE)
ATTN_KEY_CHUNK = 1024
ATTN_HEADS_PER_STEP = 4


def _rope_slab(x, c, s1, s2):
    half = QK_ROPE // 2
    return x * c + pltpu.roll(x, LANES - half, 1) * s1 + pltpu.roll(x, half, 1) * s2


def _inproj_kernel(*refs, rope):
    if rope:
        (x_ref, mod_ref, gn_ref, wq_ref, wckv_ref, wkr_ref, why_ref, wg_ref, qn_ref, kvn_ref,
         wuq_ref, wuk_ref, c_ref, s1_ref, s2_ref,
         q_out, k_out, cb_out, ckv_out, kr_out, hy_out, ga_out, gb_out) = refs
    else:
        (x_ref, mod_ref, gn_ref, wq_ref, wckv_ref, wkr_ref, why_ref, wg_ref, qn_ref, kvn_ref,
         wuq_ref, wuk_ref,
         q_out, k_out, cb_out, ckv_out, kr_out, hy_out, ga_out, gb_out) = refs
    x = x_ref[...]
    mod = mod_ref[...]
    d = x.shape[1]
    h = (_rms(x, gn_ref[...]) * (1.0 + mod[1:2]) + mod[0:1]).astype(BF16)
    q = _dot(_rms(_dot(h, wq_ref[...]), qn_ref[...]).astype(BF16), wuq_ref[...])
    ckv = _rms(_dot(h, wckv_ref[...]), kvn_ref[...])
    kr = _dot(h, wkr_ref[...])
    ckv_b = ckv.astype(BF16)
    kn = _dot(ckv_b, wuk_ref[...])
    if rope:
        c, s1, s2 = c_ref[...], s1_ref[...], s2_ref[...]
        kr_k = _rope_slab(kr, c, s1, s2)
        q = jnp.concatenate(
            [_rope_slab(q[:, i * LANES:(i + 1) * LANES], c, s1, s2) for i in range(N_HEADS)], axis=1)
    else:
        kr_k = kr
    q_out[...] = (q * ATTN_SCALE).astype(BF16)
    k_out[...] = (kn + jnp.concatenate([kr_k] * N_HEADS, axis=1)).astype(BF16)
    cb_out[...] = ckv_b
    ckv_out[...] = ckv
    kr_out[...] = kr
    hy_out[...] = _dot(h, why_ref[...])
    g = _dot(h, wg_ref[...])
    ga_out[...] = jax.nn.sigmoid(g[:, :d])
    gb_out[...] = jax.nn.sigmoid(g[:, d:])


def _inproj(x, mod, lw, rope_tabs):
    b, l, d = x.shape
    tm = min(l, 256)
    pb = 1 if mod.shape[0] > 1 else 0
    hl = N_HEADS * LANES
    kvr = lw["wckv"].shape[1]
    hyw = lw["why"].shape[1]
    rope = rope_tabs is not None
    tok = lambda bi, i: (bi, i, 0)
    w_names = ["gn", "wq", "wckv", "wkr", "why", "wg", "qn", "kvn", "wuq", "wuk"]
    ins = [x, mod] + [lw[n] for n in w_names]
    in_specs = [pl.BlockSpec((None, tm, d), tok), pl.BlockSpec((None, 6, d), lambda bi, i: (bi * pb, 0, 0))]
    in_specs += [_full(lw[n].shape) for n in w_names]
    if rope:
        ins += list(rope_tabs)
        in_specs += [pl.BlockSpec((tm, LANES), lambda bi, i: (i, 0))] * 3
    widths = [(hl, BF16), (hl, BF16), (kvr, BF16), (kvr, F32), (LANES, F32), (hyw, F32), (d, F32), (d, F32)]
    return pl.pallas_call(
        functools.partial(_inproj_kernel, rope=rope),
        grid=(b, l // tm),
        in_specs=in_specs,
        out_specs=[pl.BlockSpec((None, tm, w), tok) for w, _ in widths],
        out_shape=[jax.ShapeDtypeStruct((b, l, w), dt) for w, dt in widths],
        compiler_params=_cp("parallel", "parallel"),
        name="inproj_rope" if rope else "inproj",
    )(*ins)


def _kvcache_kernel(ckv_ref, kr_ref, wuk_ref, e_ref, k_out, cb_out):
    ckv_b = ckv_ref[...].astype(BF16)
    k_out[...] = (_dot(ckv_b, wuk_ref[...]) + _dot(kr_ref[...].astype(BF16), e_ref[...])).astype(BF16)
    cb_out[...] = ckv_b


def _kvcache(cache_ckv, cache_kr, layer, wuk, e_mat):
    b, _, p, kvr = cache_ckv.shape
    hl = N_HEADS * LANES
    return pl.pallas_call(
        _kvcache_kernel,
        grid=(b,),
        in_specs=[
            pl.BlockSpec((None, None, p, kvr), lambda bi: (bi, layer, 0, 0)),
            pl.BlockSpec((None, None, p, QK_ROPE), lambda bi: (bi, layer, 0, 0)),
            _full(wuk.shape),
            _full(e_mat.shape),
        ],
        out_specs=[pl.BlockSpec((None, p, hl), lambda bi: (bi, 0, 0)),
                   pl.BlockSpec((None, p, kvr), lambda bi: (bi, 0, 0))],
        out_shape=[jax.ShapeDtypeStruct((b, p, hl), BF16), jax.ShapeDtypeStruct((b, p, kvr), BF16)],
        compiler_params=_cp("parallel"),
        name="kvcache",
    )(cache_ckv, cache_kr, wuk, e_mat)


def _attn_kernel(*refs, ctx):
    if ctx:
        q_ref, k_ref, c_ref, wuv_ref, kc_ref, cc_ref, o_ref = refs
    else:
        q_ref, k_ref, c_ref, wuv_ref, o_ref = refs
    chunks = [(kc_ref, cc_ref, 0, kc_ref.shape[0])] if ctx else []
    n_keys = k_ref.shape[0]
    ck = min(n_keys, ATTN_KEY_CHUNK)
    chunks += [(k_ref, c_ref, c * ck, ck) for c in range(n_keys // ck)]
    heads = [slice(i * LANES, (i + 1) * LANES) for i in range(ATTN_HEADS_PER_STEP)]
    qs = [q_ref[:, h] for h in heads]
    m = [None] * len(heads)
    acc = [None] * len(heads)
    den = [None] * len(heads)
    for kr, cr, start, size in chunks:
        for i, h in enumerate(heads):
            s = _dot_nt(qs[i], kr[pl.ds(start, size), h])
            cmax = jnp.max(s, axis=1, keepdims=True)
            if m[i] is None:
                m[i] = cmax
                p = jnp.exp(s - cmax)
                den[i] = jnp.sum(p, axis=1, keepdims=True)
                acc[i] = _dot(p.astype(BF16), cr[pl.ds(start, size), :])
            else:
                m_new = jnp.maximum(m[i], cmax)
                alpha = jnp.exp(m[i] - m_new)
                p = jnp.exp(s - m_new)
                den[i] = den[i] * alpha + jnp.sum(p, axis=1, keepdims=True)
                acc[i] = acc[i] * alpha + _dot(p.astype(BF16), cr[pl.ds(start, size), :])
                m[i] = m_new
    for i, h in enumerate(heads):
        o_ref[:, h] = _dot((acc[i] / den[i]).astype(BF16), wuv_ref[:, h]).astype(BF16)


def _attention(q, k, cb, wuv, kc=None, cc=None):
    b, l, hl = q.shape
    kvr = cb.shape[2]
    tq = min(l, 512)
    ctx = kc is not None
    hw = ATTN_HEADS_PER_STEP * LANES
    ins = [q, k, cb, wuv]
    in_specs = [
        pl.BlockSpec((None, tq, hw), lambda bi, h, i: (bi, i, h)),
        pl.BlockSpec((None, l, hw), lambda bi, h, i: (bi, 0, h)),
        pl.BlockSpec((None, l, kvr), lambda bi, h, i: (bi, 0, 0)),
        pl.BlockSpec((kvr, hw), lambda bi, h, i: (0, h)),
    ]
    if ctx:
        p = kc.shape[1]
        ins += [kc, cc]
        in_specs += [pl.BlockSpec((None, p, hw), lambda bi, h, i: (bi, 0, h)),
                     pl.BlockSpec((None, p, kvr), lambda bi, h, i: (bi, 0, 0))]
    return pl.pallas_call(
        functools.partial(_attn_kernel, ctx=ctx),
        grid=(b, N_HEADS // ATTN_HEADS_PER_STEP, l // tq),
        in_specs=in_specs,
        out_specs=pl.BlockSpec((None, tq, hw), lambda bi, h, i: (bi, i, h)),
        out_shape=jax.ShapeDtypeStruct((b, l, hl), BF16),
        compiler_params=_cp("parallel", "parallel", "parallel"),
        name="attn_ctx" if ctx else "attn",
    )(*ins)


def _shortconv_kernel(x_ref, w_ref, b_ref, o_ref):
    x = x_ref[...]
    n = x.shape[0]
    row = lax.broadcasted_iota(jnp.int32, x.shape, 0)
    xm = jnp.where(row == 0, 0.0, pltpu.roll(x, 1, 0))
    xp = jnp.where(row == n - 1, 0.0, pltpu.roll(x, n - 1, 0))
    w = w_ref[...]
    o_ref[...] = (xm * w[0:1] + x * w[1:2] + xp * w[2:3] + b_ref[...]).astype(BF16)


def _shortconv(x, w, bias):
    b, l, c = x.shape
    ct = _tile(c, 256)
    return pl.pallas_call(
        _shortconv_kernel,
        grid=(b, c // ct),
        in_specs=[
            pl.BlockSpec((None, l, ct), lambda bi, j: (bi, 0, j)),
            pl.BlockSpec((SHORT_K, ct), lambda bi, j: (0, j)),
            pl.BlockSpec((1, ct), lambda bi, j: (0, j)),
        ],
        out_specs=pl.BlockSpec((None, l, ct), lambda bi, j: (bi, 0, j)),
        out_shape=jax.ShapeDtypeStruct((b, l, c), BF16),
        compiler_params=_cp("parallel", "parallel"),
        name="shortconv",
    )(x, w, bias.reshape(1, c))


def _filter_kernel(w1_ref, b1_ref, w2_ref, b2_ref, w3_ref, dec_ref, w3b_ref, o_ref, *, seq):
    tr = o_ref.shape[0]
    d = lax.broadcasted_iota(jnp.int32, (tr, LANES), 0) + pl.program_id(0) * tr
    t = jnp.where(d <= seq, d, 2 * seq - d).astype(F32) / seq
    lane = lax.broadcasted_iota(jnp.int32, (tr, LANES), 1)
    band = jnp.where(lane <= HY_BANDS, lane, lane - HY_BANDS).astype(F32)
    ang = (2.0 * math.pi * t) * band
    z = jnp.where(lane == 0, t,
                  jnp.where(lane <= HY_BANDS, jnp.sin(ang),
                            jnp.where(lane <= 2 * HY_BANDS, jnp.cos(ang), 0.0)))
    f = jnp.sin(_dot_hi(z, w1_ref[...]) + b1_ref[...])
    f = jnp.sin(_dot_hi(f, w2_ref[...]) + b2_ref[...])
    taps = _dot_hi(f, w3_ref[...]) * jnp.exp(-dec_ref[...] * t[:, :1])
    o_ref[...] = jnp.where(d[:, :1] == seq, 0.0, taps)

    @pl.when(pl.program_id(0) == 0)
    def _():
        first = slice(0, SUBLANES)
        lag0 = jnp.where(d[first, :1] == 0, _dot_hi(f[first], w3b_ref[...]), 0.0)
        o_ref[first, :] = taps[first] + lag0


def _hyena_filter(seq, w1, b1, w2, b2, w3, decay):
    order, _, width = decay.shape
    n = 2 * seq
    tr = min(seq, 512)
    per_dir = n // tr // 2
    hid = w2.shape[0]
    ow = order * width
    w1p = jnp.zeros((LANES, hid), F32).at[:w1.shape[0]].set(w1)
    w3d = w3.reshape(hid, order, 2, width).transpose(2, 0, 1, 3).reshape(2, hid, ow)
    decd = decay.transpose(1, 0, 2).reshape(2, 1, ow)
    small = [w1p, b1.reshape(1, hid), w2, b2.reshape(1, hid)]
    return pl.pallas_call(
        functools.partial(_filter_kernel, seq=seq),
        grid=(n // tr,),
        in_specs=[_full(a.shape) for a in small] + [
            pl.BlockSpec((None, hid, ow), lambda i: (i // per_dir, 0, 0)),
            pl.BlockSpec((None, 1, ow), lambda i: (i // per_dir, 0, 0)),
            pl.BlockSpec((None, hid, ow), lambda i: (1, 0, 0)),
        ],
        out_specs=pl.BlockSpec((tr, ow), lambda i: (i, 0)),
        out_shape=jax.ShapeDtypeStruct((n, ow), F32),
        compiler_params=_cp("parallel"),
        name="hyena_filter",
    )(*small, w3d, decd, w3d)


def _direct_consts(seq):
    n = 2 * seq
    k = np.arange(n, dtype=np.float64)[:, None]
    pos = np.arange(n, dtype=np.float64)[None, :]
    ang = 2.0 * np.pi * k * pos / n
    fwd = np.concatenate([np.cos(ang), -np.sin(ang)], axis=0)
    inv = np.concatenate([np.cos(ang.T), -np.sin(ang.T)], axis=1)[:seq] / n
    return (jnp.asarray(fwd, F32), jnp.asarray(fwd[:, :seq], BF16), jnp.asarray(inv, BF16))


def _dfilt_kernel(kc_ref, f_ref, o_ref):
    o_ref[...] = _dot_hi(f_ref[...], kc_ref[...])


def _direct_filter_spectrum(kc, fwd_full):
    n, ch = kc.shape
    ct = _tile(ch, 256)
    return pl.pallas_call(
        _dfilt_kernel,
        grid=(ch // ct,),
        in_specs=[pl.BlockSpec((n, ct), lambda j: (0, j)), _full(fwd_full.shape)],
        out_specs=pl.BlockSpec((2 * n, ct), lambda j: (0, j)),
        out_shape=jax.ShapeDtypeStruct((2 * n, ch), F32),
        compiler_params=_cp("parallel"),
        name="filter_spectrum_direct",
    )(kc, fwd_full)


def _dconv_kernel(v_ref, x1_ref, x2_ref, kf_ref, bias_ref, f_ref, fi_ref, o_ref):
    n = f_ref.shape[0] // 2
    ct = o_ref.shape[1]
    gates = (x1_ref, x2_ref)
    y = v_ref[...].astype(F32)
    for o in range(HY_ORDER):
        z = _dot(f_ref[...], y.astype(BF16))
        zr, zi = z[:n], z[n:]
        kr = kf_ref[:n, o * ct:(o + 1) * ct]
        ki = kf_ref[n:, o * ct:(o + 1) * ct]
        prod = jnp.concatenate([zr * kr - zi * ki, zr * ki + zi * kr], axis=0).astype(BF16)
        y = gates[o][...].astype(F32) * (_dot(fi_ref[...], prod) + y * bias_ref[o:o + 1, :])
    o_ref[...] = y.astype(BF16)


def _hyena_direct(hs, kf, bias, fwd, inv):
    b, l, c3 = hs.shape
    w = c3 // (HY_ORDER + 1)
    ct = _tile(w, 256)
    nt = w // ct
    return pl.pallas_call(
        _dconv_kernel,
        grid=(nt, b),
        in_specs=[
            pl.BlockSpec((None, l, ct), lambda j, bi: (bi, 0, j)),
            pl.BlockSpec((None, l, ct), lambda j, bi: (bi, 0, nt + j)),
            pl.BlockSpec((None, l, ct), lambda j, bi: (bi, 0, 2 * nt + j)),
            pl.BlockSpec((kf.shape[0], HY_ORDER * ct), lambda j, bi: (0, j)),
            pl.BlockSpec((HY_ORDER, ct), lambda j, bi: (0, j)),
            _full(fwd.shape),
            _full(inv.shape),
        ],
        out_specs=pl.BlockSpec((None, l, ct), lambda j, bi: (bi, 0, j)),
        out_shape=jax.ShapeDtypeStruct((b, l, w), BF16),
        compiler_params=_cp("parallel", "parallel"),
        name="hyena_conv_direct",
    )(hs, hs, hs, kf, bias, fwd, inv)


def _to_lanes(x):
    return jnp.swapaxes(x, -1, -2).reshape(*x.shape[:-2], x.shape[-1] * x.shape[-2])


def _two_stage_consts(seq):
    n = 2 * seq
    n1 = n // FFT_B
    a_in = n1 // 2
    k1 = np.arange(n1, dtype=np.float64)[:, None]
    a = np.arange(n1, dtype=np.float64)[None, :]
    ang1 = 2.0 * np.pi * k1 * a / n1
    f1_full = np.concatenate([np.cos(ang1), -np.sin(ang1)], axis=0)
    f1_inv = np.concatenate([np.cos(ang1.T), -np.sin(ang1.T)], axis=1)[:a_in] / n
    bb = np.arange(FFT_B, dtype=np.float64)[None, :]
    angt = 2.0 * np.pi * k1 * bb / n
    rep = LANES // FFT_B
    tw = np.stack([np.tile(np.cos(angt), (1, rep)), np.tile(-np.sin(angt), (1, rep))])
    b_ = np.arange(FFT_B, dtype=np.float64)[:, None]
    k2 = np.arange(FFT_B, dtype=np.float64)[None, :]
    ang2 = 2.0 * np.pi * b_ * k2 / FFT_B
    bd = lambda m: np.kron(np.eye(rep), m)
    cr, ci = np.cos(ang2), -np.sin(ang2)
    w2 = np.block([[bd(cr), bd(ci)], [bd(-ci), bd(cr)]])
    er, ei = np.cos(ang2), np.sin(ang2)
    w2i = np.block([[bd(er), bd(ei)], [bd(-ei), bd(er)]])
    return dict(
        f1_full=jnp.asarray(f1_full, F32), f1=jnp.asarray(f1_full[:, :a_in], BF16),
        f1_inv=jnp.asarray(f1_inv, BF16), tw=jnp.asarray(tw, F32),
        w2=jnp.asarray(w2, BF16), w2_f32=jnp.asarray(w2, F32), w2i=jnp.asarray(w2i, BF16))


def _fft_fwd(x, f1, tw_r, tw_i, w2, dot, cast):
    n1 = tw_r.shape[0]
    z = dot(f1, cast(x))
    zr, zi = z[:n1], z[n1:]
    rows = []
    for g in range(x.shape[1] // LANES):
        a = zr[:, g * LANES:(g + 1) * LANES]
        b = zi[:, g * LANES:(g + 1) * LANES]
        rows.append(jnp.concatenate([a * tw_r - b * tw_i, a * tw_i + b * tw_r], axis=1))
    return dot(cast(jnp.concatenate(rows, axis=0)), w2)


def _fft_inv(yf, w2i, tw_r, tw_i, f1_inv):
    n1 = tw_r.shape[0]
    gm = _dot(yf.astype(BF16), w2i)
    cols = []
    for g in range(yf.shape[0] // n1):
        blk = gm[g * n1:(g + 1) * n1]
        gr, gi = blk[:, :LANES], blk[:, LANES:]
        cols.append(jnp.concatenate([gr * tw_r + gi * tw_i, gi * tw_r - gr * tw_i], axis=0))
    return _dot(f1_inv, jnp.concatenate(cols, axis=1).astype(BF16))


def _ffilt_kernel(kc_ref, f1_ref, tw_ref, w2_ref, o_ref):
    x = _fft_fwd(kc_ref[...], f1_ref[...], tw_ref[0], tw_ref[1], w2_ref[...], _dot_hi, lambda v: v)
    o_ref[...] = x.reshape(o_ref.shape)


def _two_stage_filter_spectrum(kc_t, cs, groups_per_tile):
    n1, lanes = kc_t.shape
    wt = groups_per_tile * LANES
    ng = lanes // LANES
    return pl.pallas_call(
        _ffilt_kernel,
        grid=(lanes // wt,),
        in_specs=[pl.BlockSpec((n1, wt), lambda j: (0, j)), _full(cs["f1_full"].shape),
                  _full(cs["tw"].shape), _full(cs["w2_f32"].shape)],
        out_specs=pl.BlockSpec((groups_per_tile, n1, 2 * LANES), lambda j: (j, 0, 0)),
        out_shape=jax.ShapeDtypeStruct((ng, n1, 2 * LANES), F32),
        compiler_params=_cp("parallel"),
        name="filter_spectrum_two_stage",
    )(kc_t, cs["f1_full"], cs["tw"], cs["w2_f32"])


def _fconv_kernel(v_ref, x1_ref, x2_ref, kf_ref, bias_ref, f1_ref, tw_ref, w2_ref, w2i_ref, f1i_ref, o_ref):
    tw_r, tw_i = tw_ref[0], tw_ref[1]
    gates = (x1_ref, x2_ref)
    y = v_ref[...].astype(F32)
    for o in range(HY_ORDER):
        x = _fft_fwd(y, f1_ref[...], tw_r, tw_i, w2_ref[...], _dot, lambda v: v.astype(BF16))
        kf = kf_ref[o]
        kf = kf.reshape(kf.shape[0] * kf.shape[1], kf.shape[2])
        xr, xi = x[:, :LANES], x[:, LANES:]
        kr, ki = kf[:, :LANES], kf[:, LANES:]
        prod = jnp.concatenate([xr * kr - xi * ki, xr * ki + xi * kr], axis=1)
        conv = _fft_inv(prod, w2i_ref[...], tw_r, tw_i, f1i_ref[...])
        y = gates[o][...].astype(F32) * (conv + y * bias_ref[o:o + 1, :])
    o_ref[...] = y.astype(BF16)


def _hyena_two_stage(hs_t, kf, bias_t, cs, width):
    b, a_in, _ = hs_t.shape
    cg = min(width, 32)
    wt = cg * FFT_B
    gpt = wt // LANES
    nt = width // cg
    n1 = kf.shape[2]
    consts = [cs["f1"], cs["tw"], cs["w2"], cs["w2i"], cs["f1_inv"]]
    return pl.pallas_call(
        _fconv_kernel,
        grid=(nt, b),
        in_specs=[
            pl.BlockSpec((None, a_in, wt), lambda j, bi: (bi, 0, j)),
            pl.BlockSpec((None, a_in, wt), lambda j, bi: (bi, 0, nt + j)),
            pl.BlockSpec((None, a_in, wt), lambda j, bi: (bi, 0, 2 * nt + j)),
            pl.BlockSpec((HY_ORDER, gpt, n1, 2 * LANES), lambda j, bi: (0, j, 0, 0)),
            pl.BlockSpec((HY_ORDER, wt), lambda j, bi: (0, j)),
        ] + [_full(c.shape) for c in consts],
        out_specs=pl.BlockSpec((None, a_in, wt), lambda j, bi: (bi, 0, j)),
        out_shape=jax.ShapeDtypeStruct((b, a_in, width * FFT_B), BF16),
        compiler_params=_cp("parallel", "parallel"),
        name="hyena_conv_two_stage",
    )(hs_t, hs_t, hs_t, kf, bias_t, *consts)


def _hyena(hy_in, lp):
    b, l, c3 = hy_in.shape
    width = c3 // (HY_ORDER + 1)
    hs = _shortconv(hy_in, lp["hy_conv_w"], lp["hy_conv_b"])
    kc = _hyena_filter(l, lp["hy_w1"], lp["hy_b1"], lp["hy_w2"], lp["hy_b2"], lp["hy_w3"], lp["hy_decay"])
    n = 2 * l
    if l <= DIRECT_DFT_MAX_L:
        fwd_full, fwd, inv = _direct_consts(l)
        ct = _tile(width, 256)
        kf = _direct_filter_spectrum(kc, fwd_full)
        kf = kf.reshape(2 * n, HY_ORDER, width // ct, ct).transpose(0, 2, 1, 3).reshape(2 * n, HY_ORDER * width)
        return _hyena_direct(hs, kf, lp["hy_bias"], fwd, inv)
    cs = _two_stage_consts(l)
    n1 = n // FFT_B
    a_in = l // FFT_B
    ow = HY_ORDER * width
    kf = _two_stage_filter_spectrum(_to_lanes(kc.reshape(n1, FFT_B, ow)), cs, min(ow * FFT_B // LANES, 16))
    kf = kf.reshape(HY_ORDER, width * FFT_B // LANES, n1, 2 * LANES)
    hs_t = _to_lanes(hs.reshape(b, a_in, FFT_B, c3))
    bias_t = jnp.repeat(lp["hy_bias"], FFT_B, axis=1)
    hy_t = _hyena_two_stage(hs_t, kf, bias_t, cs, width)
    return jnp.swapaxes(hy_t.reshape(b, a_in, width, FFT_B), -1, -2).reshape(b, l, width)


def _merge_kernel(x_ref, o_ref, hy_ref, ga_ref, gb_ref, mod_ref, g2_ref, woa_ref, wob_ref, wout_ref,
                  wpq_ref, keys_ref, xn_out, h2_out, s_out):
    mod = mod_ref[...]
    merged = ga_ref[...] * _dot(o_ref[...], woa_ref[...]) + gb_ref[...] * _dot(hy_ref[...], wob_ref[...])
    xn = x_ref[...] + mod[2:3] * _dot(merged.astype(BF16), wout_ref[...])
    xn_out[...] = xn
    h2 = (_rms(xn, g2_ref[...]) * (1.0 + mod[4:5]) + mod[3:4]).astype(BF16)
    h2_out[...] = h2
    q = _dot(h2, wpq_ref[...]).astype(BF16)
    half = keys_ref.shape[2]
    for hp in range(keys_ref.shape[0]):
        s_out[hp] = _dot_nt(keys_ref[hp], q[:, hp * half:(hp + 1) * half])


def _merge(x, attn, hy, ga, gb, mod, lw):
    b, l, d = x.shape
    tm = min(l, 256)
    pb = 1 if mod.shape[0] > 1 else 0
    tok = lambda bi, i: (bi, i, 0)
    w_names = ["g2", "woa", "wob", "wout", "wpq", "keys"]
    nhp = lw["keys"].shape[0]
    return pl.pallas_call(
        _merge_kernel,
        grid=(b, l // tm),
        in_specs=[
            pl.BlockSpec((None, tm, d), tok),
            pl.BlockSpec((None, tm, attn.shape[2]), tok),
            pl.BlockSpec((None, tm, hy.shape[2]), tok),
            pl.BlockSpec((None, tm, d), tok),
            pl.BlockSpec((None, tm, d), tok),
            pl.BlockSpec((None, 6, d), lambda bi, i: (bi * pb, 0, 0)),
        ] + [_full(lw[n].shape) for n in w_names],
        out_specs=[
            pl.BlockSpec((None, tm, d), tok),
            pl.BlockSpec((None, tm, d), tok),
            pl.BlockSpec((None, nhp, N_KEYS, tm), lambda bi, i: (bi, 0, 0, i)),
        ],
        out_shape=[
            jax.ShapeDtypeStruct((b, l, d), F32),
            jax.ShapeDtypeStruct((b, l, d), BF16),
            jax.ShapeDtypeStruct((b, nhp, N_KEYS, l), F32),
        ],
        compiler_params=_cp("parallel", "parallel"),
        name="merge_peer_query",
    )(x, attn, hy, ga, gb, mod, *[lw[n] for n in w_names])


def _top_rows(s, k, payloads=()):
    r = s.shape[0]
    rid = lax.broadcasted_iota(jnp.int32, s.shape, 0).astype(F32)
    vals, idxs, picked = [], [], [[] for _ in payloads]
    for _ in range(k):
        m = jnp.max(s, axis=0, keepdims=True)
        ix = jnp.min(jnp.where(s == m, rid, float(r)), axis=0, keepdims=True)
        hit = rid == ix
        vals.append(m)
        idxs.append(ix)
        for p, acc in zip(payloads, picked):
            acc.append(jnp.sum(jnp.where(hit, p, 0.0), axis=0, keepdims=True))
        s = jnp.where(hit, -jnp.inf, s)
    cat = lambda rows: jnp.concatenate(rows, axis=0)
    return cat(vals), cat(idxs), [cat(a) for a in picked]


def _staircase(v1, r1, v2, r2):
    k = v1.shape[0]
    r1 = r1 * float(N_KEYS)
    cand, expert = [], []
    for a in range(k // 2):
        nb = k // (a + 1)
        rows = -(-nb // SUBLANES) * SUBLANES
        c = v1[a:a + 1] + v2[:rows]
        if nb < rows:
            c = jnp.where(lax.broadcasted_iota(jnp.int32, c.shape, 0) < nb, c, -jnp.inf)
        cand.append(c)
        expert.append(r1[a:a + 1] + r2[:rows])
    cand.append(v1[k // 2:] + v2[0:1])
    expert.append(r1[k // 2:] + r2[0:1])
    return jnp.concatenate(cand, axis=0), jnp.concatenate(expert, axis=0)


def _topk_kernel(s_ref, i1_out, i2_out, g_out, i1_sc, i2_sc, g_sc):
    def head(h, carry):
        v1, r1, _ = _top_rows(s_ref[2 * h], P_TOPK)
        v2, r2, _ = _top_rows(s_ref[2 * h + 1], P_TOPK)
        cand, expert = _staircase(v1, r1, v2, r2)
        top, _, (ex,) = _top_rows(cand, P_TOPK, (expert,))
        e = jnp.exp(top - top[0:1])
        k1 = jnp.floor(ex * (1.0 / N_KEYS))
        rows = pl.ds(pl.multiple_of(h * P_TOPK, P_TOPK), P_TOPK)
        i1_sc[rows, :] = k1
        i2_sc[rows, :] = ex - k1 * float(N_KEYS)
        g_sc[rows, :] = e / jnp.sum(e, axis=0, keepdims=True)
        return carry

    lax.fori_loop(0, P_HEADS, head, 0)
    i1_out[...] = i1_sc[...].T
    i2_out[...] = i2_sc[...].T
    g_out[...] = g_sc[...].T


def _peer_topk(s_t):
    b, nhp, nk, l = s_t.shape
    tt = min(l, 256)
    hk = P_HEADS * P_TOPK
    return pl.pallas_call(
        _topk_kernel,
        grid=(b, l // tt),
        in_specs=[pl.BlockSpec((None, nhp, nk, tt), lambda bi, i: (bi, 0, 0, i))],
        out_specs=[pl.BlockSpec((None, tt, hk), lambda bi, i: (bi, i, 0))] * 3,
        out_shape=[jax.ShapeDtypeStruct((b, l, hk), F32)] * 3,
        scratch_shapes=[pltpu.VMEM((hk, tt), F32)] * 3,
        compiler_params=_cp("parallel", "parallel"),
        name="peer_topk",
    )(s_t)


BF16_BITS = 0xFFFF0000
DENSE_TOKEN_UNROLL = 64


def _dense_kernel(h_ref, i1_ref, i2_ref, g_ref, u_ref, v_ref, x_ref, mod_ref, o_ref, ws_ref, acc_ref, *, pitch):
    j = pl.program_id(2)
    tm = h_ref.shape[0]
    r = v_ref.shape[0] // N_KEYS
    half = N_KEYS // 2

    @pl.when(j == 0)
    def _():
        acc_ref[...] = jnp.zeros_like(acc_ref)
        sub = lax.broadcasted_iota(jnp.int32, (N_KEYS, i1_ref.shape[1]), 0).astype(F32).astype(BF16)
        one, zero = jnp.ones((), BF16), jnp.zeros((), BF16)

        def toks(tb, carry):
            for u in range(0, DENSE_TOKEN_UNROLL, 2):
                t = tb * DENSE_TOKEN_UNROLL + u
                pts, qts = [], []
                for k in range(2):
                    row = pl.ds(t + k, 1)
                    pts.append(jnp.where(i1_ref[row, :].astype(BF16) == sub, g_ref[row, :].astype(BF16), zero))
                    qts.append(jnp.where(i2_ref[row, :].astype(BF16) == sub, one, zero))
                z = jnp.zeros_like(qts[0])
                rhs = jnp.concatenate([jnp.concatenate([qts[0], z], axis=1),
                                       jnp.concatenate([z, qts[1]], axis=1)], axis=0)
                gates = _dot_nt(jnp.concatenate(pts, axis=1), rhs)
                for k in range(2):
                    gk = gates[:, k * N_KEYS:(k + 1) * N_KEYS].astype(jnp.bfloat16)
                    ws_ref[pl.ds(t + k, half, stride=pitch), :] = pltpu.bitcast(gk, jnp.uint32)
            return carry

        lax.fori_loop(0, tm // DENSE_TOKEN_UNROLL, toks, 0)

    a = jax.nn.gelu(_dot_nt(h_ref[...], u_ref[...]))
    cols = []
    for q in range(r // 2):
        word = ws_ref[pl.ds(pl.multiple_of((j * (r // 2) + q) * pitch, SUBLANES), tm), :]
        cols.append(lax.bitcast_convert_type(word << 16, F32))
        cols.append(lax.bitcast_convert_type(word & jnp.uint32(BF16_BITS), F32))
    acc_ref[...] += _dot((a * jnp.concatenate(cols, axis=1)).astype(BF16), v_ref[...])

    @pl.when(j == pl.num_programs(2) - 1)
    def _():
        o_ref[...] = x_ref[...] + mod_ref[5:6, :] * acc_ref[...]


def _peer_dense(h2, i1, i2, g, u_tab, v_tab, xn, mod):
    if mod.shape[0] == 1 and h2.shape[0] > 1:
        b, l, d = h2.shape
        flat = lambda a: a.reshape(1, b * l, a.shape[2])
        return _peer_dense(flat(h2), flat(i1), flat(i2), flat(g), u_tab, v_tab, flat(xn), mod).reshape(b, l, d)
    b, l, d = h2.shape
    tm = min(l, 512)
    te = 1024
    ne = v_tab.shape[0]
    hk = i1.shape[2]
    pitch = tm + SUBLANES
    pb = 1 if mod.shape[0] > 1 else 0
    assert tm % DENSE_TOKEN_UNROLL == 0 and (te // N_KEYS) % 2 == 0
    tok = lambda bi, i, j: (bi, i, 0)
    return pl.pallas_call(
        functools.partial(_dense_kernel, pitch=pitch),
        grid=(b, l // tm, ne // te),
        in_specs=[
            pl.BlockSpec((None, tm, d), tok),
            pl.BlockSpec((None, tm, hk), tok),
            pl.BlockSpec((None, tm, hk), tok),
            pl.BlockSpec((None, tm, hk), tok),
            pl.BlockSpec((te, d), lambda bi, i, j: (j, 0)),
            pl.BlockSpec((te, d), lambda bi, i, j: (j, 0)),
            pl.BlockSpec((None, tm, d), tok, pipeline_mode=pl.Buffered(1)),
            pl.BlockSpec((None, 6, d), lambda bi, i, j: (bi * pb, 0, 0)),
        ],
        out_specs=pl.BlockSpec((None, tm, d), tok),
        out_shape=jax.ShapeDtypeStruct((b, l, d), F32),
        scratch_shapes=[pltpu.VMEM((N_KEYS // 2 * pitch, N_KEYS), jnp.uint32), pltpu.VMEM((tm, d), F32)],
        compiler_params=pltpu.CompilerParams(dimension_semantics=("parallel", "parallel", "arbitrary"),
                                             vmem_limit_bytes=DENSE_VMEM_LIMIT_BYTES),
        name="peer_dense",
    )(h2, i1, i2, g, u_tab, v_tab, xn, mod)


def _final_kernel(x_ref, g_ref, o_ref):
    o_ref[...] = _rms(x_ref[...], g_ref[...])


def _final_norm(x, g):
    b, l, d = x.shape
    tm = min(l, 512)
    return pl.pallas_call(
        _final_kernel,
        grid=(b, l // tm),
        in_specs=[pl.BlockSpec((None, tm, d), lambda bi, i: (bi, i, 0)), _full((1, d))],
        out_specs=pl.BlockSpec((None, tm, d), lambda bi, i: (bi, i, 0)),
        out_shape=jax.ShapeDtypeStruct((b, l, d), F32),
        compiler_params=_cp("parallel", "parallel"),
        name="final_norm",
    )(x, g.reshape(1, d))


def _rope_tables(n_tok):
    rows = n_tok // GRID_W
    row = jnp.repeat(jnp.arange(rows, dtype=F32), GRID_W)
    col = jnp.tile(jnp.arange(GRID_W, dtype=F32), rows)
    n_freq = QK_ROPE // 4
    inv = 1.0 / (ROPE_BASE ** (jnp.arange(n_freq, dtype=F32) / n_freq))
    ang = jnp.concatenate([row[:, None] * inv, col[:, None] * inv], axis=-1)
    cos, sin = jnp.cos(ang), jnp.sin(ang)
    half = QK_ROPE // 2
    pad = LANES - QK_NOPE - QK_ROPE
    ones = jnp.ones((n_tok, QK_NOPE), F32)
    z = lambda w: jnp.zeros((n_tok, w), F32)
    c = jnp.concatenate([ones, cos, cos, z(pad)], axis=1)
    s1 = jnp.concatenate([z(QK_NOPE), -sin, z(half), z(pad)], axis=1)
    s2 = jnp.concatenate([z(QK_NOPE), z(half), sin, z(pad)], axis=1)
    return c, s1, s2


def _pad_heads(w, per_head):
    rows = w.shape[0]
    w = w.reshape(rows, N_HEADS, per_head)
    return jnp.pad(w, ((0, 0), (0, 0), (0, LANES - per_head))).reshape(rows, N_HEADS * LANES)


def _layer_weights(p, l):
    d = p["w_in"].shape[1]
    q_rank = p["q_norm"].shape[1]
    kv_rank = p["kv_norm"].shape[1]
    hy3 = p["hy_conv_w"].shape[2]
    w_in = p["w_in"][l]
    c0, c1, c2, c3 = q_rank, q_rank + kv_rank, q_rank + kv_rank + QK_ROPE, q_rank + kv_rank + QK_ROPE + hy3
    wkr = jnp.zeros((d, LANES), F32).at[:, QK_NOPE:QK_NOPE + QK_ROPE].set(w_in[:, c1:c2])
    ukv = p["w_ukv"][l].reshape(kv_rank, N_HEADS, QK_NOPE + V_DIM)
    wuk = _pad_heads(ukv[:, :, :QK_NOPE].reshape(kv_rank, -1), QK_NOPE)
    wuv = _pad_heads(ukv[:, :, QK_NOPE:].reshape(kv_rank, -1), V_DIM)
    woa = p["w_oa"][l].reshape(N_HEADS, V_DIM, d)
    woa = jnp.pad(woa, ((0, 0), (0, LANES - V_DIM), (0, 0))).reshape(N_HEADS * LANES, d)
    keys = p["peer_keys"][l]
    row = lambda v: v.reshape(1, -1)
    return dict(
        gn=row(p["norm_mix"][l]), wq=w_in[:, :c0].astype(BF16), wckv=w_in[:, c0:c1].astype(BF16),
        wkr=wkr.astype(BF16), why=w_in[:, c2:c3].astype(BF16), wg=w_in[:, c3:].astype(BF16),
        qn=row(p["q_norm"][l]), kvn=row(p["kv_norm"][l]),
        wuq=_pad_heads(p["w_uq"][l], QK_NOPE + QK_ROPE).astype(BF16),
        wuk=wuk.astype(BF16), wuv=wuv.astype(BF16),
        g2=row(p["norm_ffn"][l]), woa=woa.astype(BF16), wob=p["w_ob"][l].astype(BF16),
        wout=p["w_out"][l].astype(BF16), wpq=p["peer_wq"][l].astype(BF16),
        keys=keys.reshape(keys.shape[0] * keys.shape[1], keys.shape[2], keys.shape[3]).astype(BF16),
        u=p["peer_u"][l].astype(BF16), v=p["peer_v"][l].astype(BF16),
    )


def _layer(x, mod, lw, lp, rope_tabs, ctx_kv):
    q, k, cb, ckv, kr, hy_in, ga, gb = _inproj(x, mod, lw, rope_tabs)
    if ctx_kv is None:
        attn = _attention(q, k, cb, lw["wuv"])
    else:
        attn = _attention(q, k, cb, lw["wuv"], *ctx_kv)
    hy = _hyena(hy_in, lp)
    xn, h2, s_t = _merge(x, attn, hy, ga, gb, mod, lw)
    i1, i2, g = _peer_topk(s_t)
    x = _peer_dense(h2, i1, i2, g, lw["u"], lw["v"], xn, mod)
    return x, ckv, kr[:, :, QK_NOPE:QK_NOPE + QK_ROPE]


def kernel(x_prompt, x_sample, cache_ckv, cache_krope, c, c_ctx, w_ada, b_ada, norm_mix, norm_ffn,
           w_in, q_norm, kv_norm, w_uq, w_ukv, w_oa, w_ob, w_out, hy_conv_w, hy_conv_b,
           hy_w1, hy_b1, hy_w2, hy_b2, hy_w3, hy_decay, hy_bias,
           peer_wq, peer_keys, peer_u, peer_v, final_norm):
    p = dict(norm_mix=norm_mix, norm_ffn=norm_ffn, w_in=w_in, q_norm=q_norm, kv_norm=kv_norm, w_uq=w_uq,
             w_ukv=w_ukv, w_oa=w_oa, w_ob=w_ob, w_out=w_out, peer_wq=peer_wq, peer_keys=peer_keys,
             peer_u=peer_u, peer_v=peer_v, hy_conv_w=hy_conv_w)
    depth, d = norm_mix.shape
    bd = x_sample.shape[0]
    rows = -(-(bd + 1) // SUBLANES) * SUBLANES
    cc = jnp.zeros((rows, d), F32).at[:bd].set(c).at[bd].set(c_ctx)
    mods = _adaln(cc, w_ada, b_ada).reshape(depth, rows, 6, d)
    rope_tabs = _rope_tables(x_sample.shape[1])
    e_mat = jnp.zeros((QK_ROPE, N_HEADS, LANES), F32)
    e_mat = e_mat.at[:, :, QK_NOPE:QK_NOPE + QK_ROPE].set(jnp.eye(QK_ROPE, dtype=F32)[:, None, :])
    e_mat = e_mat.reshape(QK_ROPE, N_HEADS * LANES).astype(BF16)
    xp, xs = x_prompt, x_sample
    new_ckv, new_kr = [], []
    for l in range(depth):
        lw = _layer_weights(p, l)
        lp = dict(hy_conv_w=hy_conv_w[l], hy_conv_b=hy_conv_b[l], hy_w1=hy_w1[l], hy_b1=hy_b1[l],
                  hy_w2=hy_w2[l], hy_b2=hy_b2[l], hy_w3=hy_w3[l], hy_decay=hy_decay[l], hy_bias=hy_bias[l])
        xp, ckv_p, kr_p = _layer(xp, mods[l, bd:bd + 1], lw, lp, None, None)
        new_ckv.append(ckv_p)
        new_kr.append(kr_p)
        ctx_kv = _kvcache(cache_ckv, cache_krope, l, lw["wuk"], e_mat)
        xs, _, _ = _layer(xs, mods[l, :bd], lw, lp, rope_tabs, ctx_kv)
    return (_final_norm(xp, final_norm), _final_norm(xs, final_norm),
            jnp.stack(new_ckv, axis=1), jnp.stack(new_kr, axis=1))
```

```python
import functools
import math

import numpy as np
import jax
import jax.numpy as jnp
from jax import lax
from jax.experimental import pallas as pl
from jax.experimental.pallas import tpu as pltpu

F32 = jnp.float32
BF16 = jnp.bfloat16

GRID_W = 64
N_HEADS = 8
QK_NOPE = 64
QK_ROPE = 32
V_DIM = 64
ROPE_BASE = 10000.0
HY_ORDER = 2
HY_BANDS = 16
SHORT_K = 3
P_HEADS = 8
N_KEYS = 128
P_TOPK = 16
EPS = 1e-6

LANES = 128
SUBLANES = 8
FFT_B = 64
DIRECT_DFT_MAX_L = 256
VMEM_LIMIT_BYTES = 48 * 1024 * 1024
DENSE_VMEM_LIMIT_BYTES = 56 * 1024 * 1024


def _cp(*sem):
    return pltpu.CompilerParams(dimension_semantics=sem, vmem_limit_bytes=VMEM_LIMIT_BYTES)


def _full(shape):
    n = len(shape)
    return pl.BlockSpec(shape, lambda *_: (0,) * n)


def _tile(n, pref, unit=LANES):
    if n <= pref:
        return n
    return max(t for t in range(unit, pref + 1, unit) if n % t == 0)


def _dot(a, b):
    return jnp.dot(a, b, preferred_element_type=F32)


def _dot_hi(a, b):
    return jnp.dot(a, b, precision=lax.Precision.HIGHEST, preferred_element_type=F32)


def _dot_nt(a, b):
    return lax.dot_general(a, b, (((1,), (1,)), ((), ())), preferred_element_type=F32)


def _rms(x, g):
    return x * lax.rsqrt(jnp.mean(x * x, axis=-1, keepdims=True) + EPS) * g


def _adaln_kernel(c_ref, w_ref, b_ref, o_ref):
    c = c_ref[...]
    s = (c * jax.nn.sigmoid(c)).astype(BF16)
    o_ref[...] = _dot(s, w_ref[...].astype(BF16)) + b_ref[...]


def _adaln(cc, w_ada, b_ada):
    depth, d, d6 = w_ada.shape
    rows = cc.shape[0]
    tn = d6 // 4
    return pl.pallas_call(
        _adaln_kernel,
        grid=(depth, d6 // tn),
        in_specs=[
            _full((rows, d)),
            pl.BlockSpec((None, d, tn), lambda l, j: (l, 0, j)),
            pl.BlockSpec((None, 1, tn), lambda l, j: (l, 0, j)),
        ],
        out_specs=pl.BlockSpec((None, rows, tn), lambda l, j: (l, 0, j)),
        out_shape=jax.ShapeDtypeStruct((depth, rows, d6), F32),
        compiler_params=_cp("parallel", "parallel"),
        name="adaln",
    )(cc, w_ada, b_ada.reshape(depth, 1, d6))


ATTN_SCALE = 1.0 / math.sqrt(QK_NOPE + QK_ROPE)
ATTN_KEY_CHUNK = 1024
ATTN_HEADS_PER_STEP = 4


def _rope_slab(x, c, s1, s2):
    half = QK_ROPE // 2
    return x * c + pltpu.roll(x, LANES - half, 1) * s1 + pltpu.roll(x, half, 1) * s2


def _inproj_kernel(*refs, rope):
    if rope:
        (x_ref, mod_ref, gn_ref, wq_ref, wckv_ref, wkr_ref, why_ref, wg_ref, qn_ref, kvn_ref,
         wuq_ref, wuk_ref, c_ref, s1_ref, s2_ref,
         q_out, k_out, cb_out, ckv_out, kr_out, hy_out, ga_out, gb_out) = refs
    else:
        (x_ref, mod_ref, gn_ref, wq_ref, wckv_ref, wkr_ref, why_ref, wg_ref, qn_ref, kvn_ref,
         wuq_ref, wuk_ref,
         q_out, k_out, cb_out, ckv_out, kr_out, hy_out, ga_out, gb_out) = refs
    x = x_ref[...]
    mod = mod_ref[...]
    d = x.shape[1]
    h = (_rms(x, gn_ref[...]) * (1.0 + mod[1:2]) + mod[0:1]).astype(BF16)
    q = _dot(_rms(_dot(h, wq_ref[...]), qn_ref[...]).astype(BF16), wuq_ref[...])
    ckv = _rms(_dot(h, wckv_ref[...]), kvn_ref[...])
    kr = _dot(h, wkr_ref[...])
    ckv_b = ckv.astype(BF16)
    kn = _dot(ckv_b, wuk_ref[...])
    if rope:
        c, s1, s2 = c_ref[...], s1_ref[...], s2_ref[...]
        kr_k = _rope_slab(kr, c, s1, s2)
        q = jnp.concatenate(
            [_rope_slab(q[:, i * LANES:(i + 1) * LANES], c, s1, s2) for i in range(N_HEADS)], axis=1)
    else:
        kr_k = kr
    q_out[...] = (q * ATTN_SCALE).astype(BF16)
    k_out[...] = (kn + jnp.concatenate([kr_k] * N_HEADS, axis=1)).astype(BF16)
    cb_out[...] = ckv_b
    ckv_out[...] = ckv
    kr_out[...] = kr
    hy_out[...] = _dot(h, why_ref[...])
    g = _dot(h, wg_ref[...])
    ga_out[...] = jax.nn.sigmoid(g[:, :d])
    gb_out[...] = jax.nn.sigmoid(g[:, d:])


def _inproj(x, mod, lw, rope_tabs):
    b, l, d = x.shape
    tm = min(l, 256)
    pb = 1 if mod.shape[0] > 1 else 0
    hl = N_HEADS * LANES
    kvr = lw["wckv"].shape[1]
    hyw = lw["why"].shape[1]
    rope = rope_tabs is not None
    tok = lambda bi, i: (bi, i, 0)
    w_names = ["gn", "wq", "wckv", "wkr", "why", "wg", "qn", "kvn", "wuq", "wuk"]
    ins = [x, mod] + [lw[n] for n in w_names]
    in_specs = [pl.BlockSpec((None, tm, d), tok), pl.BlockSpec((None, 6, d), lambda bi, i: (bi * pb, 0, 0))]
    in_specs += [_full(lw[n].shape) for n in w_names]
    if rope:
        ins += list(rope_tabs)
        in_specs += [pl.BlockSpec((tm, LANES), lambda bi, i: (i, 0))] * 3
    widths = [(hl, BF16), (hl, BF16), (kvr, BF16), (kvr, F32), (LANES, F32), (hyw, F32), (d, F32), (d, F32)]
    return pl.pallas_call(
        functools.partial(_inproj_kernel, rope=rope),
        grid=(b, l // tm),
        in_specs=in_specs,
        out_specs=[pl.BlockSpec((None, tm, w), tok) for w, _ in widths],
        out_shape=[jax.ShapeDtypeStruct((b, l, w), dt) for w, dt in widths],
        compiler_params=_cp("parallel", "parallel"),
        name="inproj_rope" if rope else "inproj",
    )(*ins)


def _kvcache_kernel(ckv_ref, kr_ref, wuk_ref, e_ref, k_out, cb_out):
    ckv_b = ckv_ref[...].astype(BF16)
    k_out[...] = (_dot(ckv_b, wuk_ref[...]) + _dot(kr_ref[...].astype(BF16), e_ref[...])).astype(BF16)
    cb_out[...] = ckv_b


def _kvcache(cache_ckv, cache_kr, layer, wuk, e_mat):
    b, _, p, kvr = cache_ckv.shape
    hl = N_HEADS * LANES
    return pl.pallas_call(
        _kvcache_kernel,
        grid=(b,),
        in_specs=[
            pl.BlockSpec((None, None, p, kvr), lambda bi: (bi, layer, 0, 0)),
            pl.BlockSpec((None, None, p, QK_ROPE), lambda bi: (bi, layer, 0, 0)),
            _full(wuk.shape),
            _full(e_mat.shape),
        ],
        out_specs=[pl.BlockSpec((None, p, hl), lambda bi: (bi, 0, 0)),
                   pl.BlockSpec((None, p, kvr), lambda bi: (bi, 0, 0))],
        out_shape=[jax.ShapeDtypeStruct((b, p, hl), BF16), jax.ShapeDtypeStruct((b, p, kvr), BF16)],
        compiler_params=_cp("parallel"),
        name="kvcache",
    )(cache_ckv, cache_kr, wuk, e_mat)


def _attn_kernel(*refs, ctx):
    if ctx:
        q_ref, k_ref, c_ref, wuv_ref, kc_ref, cc_ref, o_ref = refs
    else:
        q_ref, k_ref, c_ref, wuv_ref, o_ref = refs
    chunks = [(kc_ref, cc_ref, 0, kc_ref.shape[0])] if ctx else []
    n_keys = k_ref.shape[0]
    ck = min(n_keys, ATTN_KEY_CHUNK)
    chunks += [(k_ref, c_ref, c * ck, ck) for c in range(n_keys // ck)]
    heads = [slice(i * LANES, (i + 1) * LANES) for i in range(ATTN_HEADS_PER_STEP)]
    qs = [q_ref[:, h] for h in heads]
    m = [None] * len(heads)
    acc = [None] * len(heads)
    den = [None] * len(heads)
    for kr, cr, start, size in chunks:
        for i, h in enumerate(heads):
            s = _dot_nt(qs[i], kr[pl.ds(start, size), h])
            cmax = jnp.max(s, axis=1, keepdims=True)
            if m[i] is None:
                m[i] = cmax
                p = jnp.exp(s - cmax)
                den[i] = jnp.sum(p, axis=1, keepdims=True)
                acc[i] = _dot(p.astype(BF16), cr[pl.ds(start, size), :])
            else:
                m_new = jnp.maximum(m[i], cmax)
                alpha = jnp.exp(m[i] - m_new)
                p = jnp.exp(s - m_new)
                den[i] = den[i] * alpha + jnp.sum(p, axis=1, keepdims=True)
                acc[i] = acc[i] * alpha + _dot(p.astype(BF16), cr[pl.ds(start, size), :])
                m[i] = m_new
    for i, h in enumerate(heads):
        o_ref[:, h] = _dot((acc[i] / den[i]).astype(BF16), wuv_ref[:, h]).astype(BF16)


def _attention(q, k, cb, wuv, kc=None, cc=None):
    b, l, hl = q.shape
    kvr = cb.shape[2]
    tq = min(l, 512)
    ctx = kc is not None
    hw = ATTN_HEADS_PER_STEP * LANES
    ins = [q, k, cb, wuv]
    in_specs = [
        pl.BlockSpec((None, tq, hw), lambda bi, h, i: (bi, i, h)),
        pl.BlockSpec((None, l, hw), lambda bi, h, i: (bi, 0, h)),
        pl.BlockSpec((None, l, kvr), lambda bi, h, i: (bi, 0, 0)),
        pl.BlockSpec((kvr, hw), lambda bi, h, i: (0, h)),
    ]
    if ctx:
        p = kc.shape[1]
        ins += [kc, cc]
        in_specs += [pl.BlockSpec((None, p, hw), lambda bi, h, i: (bi, 0, h)),
                     pl.BlockSpec((None, p, kvr), lambda bi, h, i: (bi, 0, 0))]
    return pl.pallas_call(
        functools.partial(_attn_kernel, ctx=ctx),
        grid=(b, N_HEADS // ATTN_HEADS_PER_STEP, l // tq),
        in_specs=in_specs,
        out_specs=pl.BlockSpec((None, tq, hw), lambda bi, h, i: (bi, i, h)),
        out_shape=jax.ShapeDtypeStruct((b, l, hl), BF16),
        compiler_params=_cp("parallel", "parallel", "parallel"),
        name="attn_ctx" if ctx else "attn",
    )(*ins)


def _shortconv_kernel(x_ref, w_ref, b_ref, o_ref):
    x = x_ref[...]
    n = x.shape[0]
    row = lax.broadcasted_iota(jnp.int32, x.shape, 0)
    xm = jnp.where(row == 0, 0.0, pltpu.roll(x, 1, 0))
    xp = jnp.where(row == n - 1, 0.0, pltpu.roll(x, n - 1, 0))
    w = w_ref[...]
    o_ref[...] = (xm * w[0:1] + x * w[1:2] + xp * w[2:3] + b_ref[...]).astype(BF16)


def _shortconv(x, w, bias):
    b, l, c = x.shape
    ct = _tile(c, 256)
    return pl.pallas_call(
        _shortconv_kernel,
        grid=(b, c // ct),
        in_specs=[
            pl.BlockSpec((None, l, ct), lambda bi, j: (bi, 0, j)),
            pl.BlockSpec((SHORT_K, ct), lambda bi, j: (0, j)),
            pl.BlockSpec((1, ct), lambda bi, j: (0, j)),
        ],
        out_specs=pl.BlockSpec((None, l, ct), lambda bi, j: (bi, 0, j)),
        out_shape=jax.ShapeDtypeStruct((b, l, c), BF16),
        compiler_params=_cp("parallel", "parallel"),
        name="shortconv",
    )(x, w, bias.reshape(1, c))


def _filter_kernel(w1_ref, b1_ref, w2_ref, b2_ref, w3_ref, dec_ref, w3b_ref, o_ref, *, seq):
    tr = o_ref.shape[0]
    d = lax.broadcasted_iota(jnp.int32, (tr, LANES), 0) + pl.program_id(0) * tr
    t = jnp.where(d <= seq, d, 2 * seq - d).astype(F32) / seq
    lane = lax.broadcasted_iota(jnp.int32, (tr, LANES), 1)
    band = jnp.where(lane <= HY_BANDS, lane, lane - HY_BANDS).astype(F32)
    ang = (2.0 * math.pi * t) * band
    z = jnp.where(lane == 0, t,
                  jnp.where(lane <= HY_BANDS, jnp.sin(ang),
                            jnp.where(lane <= 2 * HY_BANDS, jnp.cos(ang), 0.0)))
    f = jnp.sin(_dot_hi(z, w1_ref[...]) + b1_ref[...])
    f = jnp.sin(_dot_hi(f, w2_ref[...]) + b2_ref[...])
    taps = _dot_hi(f, w3_ref[...]) * jnp.exp(-dec_ref[...] * t[:, :1])
    o_ref[...] = jnp.where(d[:, :1] == seq, 0.0, taps)

    @pl.when(pl.program_id(0) == 0)
    def _():
        first = slice(0, SUBLANES)
        lag0 = jnp.where(d[first, :1] == 0, _dot_hi(f[first], w3b_ref[...]), 0.0)
        o_ref[first, :] = taps[first] + lag0


def _hyena_filter(seq, w1, b1, w2, b2, w3, decay):
    order, _, width = decay.shape
    n = 2 * seq
    tr = min(seq, 512)
    per_dir = n // tr // 2
    hid = w2.shape[0]
    ow = order * width
    w1p = jnp.zeros((LANES, hid), F32).at[:w1.shape[0]].set(w1)
    w3d = w3.reshape(hid, order, 2, width).transpose(2, 0, 1, 3).reshape(2, hid, ow)
    decd = decay.transpose(1, 0, 2).reshape(2, 1, ow)
    small = [w1p, b1.reshape(1, hid), w2, b2.reshape(1, hid)]
    return pl.pallas_call(
        functools.partial(_filter_kernel, seq=seq),
        grid=(n // tr,),
        in_specs=[_full(a.shape) for a in small] + [
            pl.BlockSpec((None, hid, ow), lambda i: (i // per_dir, 0, 0)),
            pl.BlockSpec((None, 1, ow), lambda i: (i // per_dir, 0, 0)),
            pl.BlockSpec((None, hid, ow), lambda i: (1, 0, 0)),
        ],
        out_specs=pl.BlockSpec((tr, ow), lambda i: (i, 0)),
        out_shape=jax.ShapeDtypeStruct((n, ow), F32),
        compiler_params=_cp("parallel"),
        name="hyena_filter",
    )(*small, w3d, decd, w3d)


def _direct_consts(seq):
    n = 2 * seq
    k = np.arange(n, dtype=np.float64)[:, None]
    pos = np.arange(n, dtype=np.float64)[None, :]
    ang = 2.0 * np.pi * k * pos / n
    fwd = np.concatenate([np.cos(ang), -np.sin(ang)], axis=0)
    inv = np.concatenate([np.cos(ang.T), -np.sin(ang.T)], axis=1)[:seq] / n
    return (jnp.asarray(fwd, F32), jnp.asarray(fwd[:, :seq], BF16), jnp.asarray(inv, BF16))


def _dfilt_kernel(kc_ref, f_ref, o_ref):
    o_ref[...] = _dot_hi(f_ref[...], kc_ref[...])


def _direct_filter_spectrum(kc, fwd_full):
    n, ch = kc.shape
    ct = _tile(ch, 256)
    return pl.pallas_call(
        _dfilt_kernel,
        grid=(ch // ct,),
        in_specs=[pl.BlockSpec((n, ct), lambda j: (0, j)), _full(fwd_full.shape)],
        out_specs=pl.BlockSpec((2 * n, ct), lambda j: (0, j)),
        out_shape=jax.ShapeDtypeStruct((2 * n, ch), F32),
        compiler_params=_cp("parallel"),
        name="filter_spectrum_direct",
    )(kc, fwd_full)


def _dconv_kernel(v_ref, x1_ref, x2_ref, kf_ref, bias_ref, f_ref, fi_ref, o_ref):
    n = f_ref.shape[0] // 2
    ct = o_ref.shape[1]
    gates = (x1_ref, x2_ref)
    y = v_ref[...].astype(F32)
    for o in range(HY_ORDER):
        z = _dot(f_ref[...], y.astype(BF16))
        zr, zi = z[:n], z[n:]
        kr = kf_ref[:n, o * ct:(o + 1) * ct]
        ki = kf_ref[n:, o * ct:(o + 1) * ct]
        prod = jnp.concatenate([zr * kr - zi * ki, zr * ki + zi * kr], axis=0).astype(BF16)
        y = gates[o][...].astype(F32) * (_dot(fi_ref[...], prod) + y * bias_ref[o:o + 1, :])
    o_ref[...] = y.astype(BF16)


def _hyena_direct(hs, kf, bias, fwd, inv):
    b, l, c3 = hs.shape
    w = c3 // (HY_ORDER + 1)
    ct = _tile(w, 256)
    nt = w // ct
    return pl.pallas_call(
        _dconv_kernel,
        grid=(nt, b),
        in_specs=[
            pl.BlockSpec((None, l, ct), lambda j, bi: (bi, 0, j)),
            pl.BlockSpec((None, l, ct), lambda j, bi: (bi, 0, nt + j)),
            pl.BlockSpec((None, l, ct), lambda j, bi: (bi, 0, 2 * nt + j)),
            pl.BlockSpec((kf.shape[0], HY_ORDER * ct), lambda j, bi: (0, j)),
            pl.BlockSpec((HY_ORDER, ct), lambda j, bi: (0, j)),
            _full(fwd.shape),
            _full(inv.shape),
        ],
        out_specs=pl.BlockSpec((None, l, ct), lambda j, bi: (bi, 0, j)),
        out_shape=jax.ShapeDtypeStruct((b, l, w), BF16),
        compiler_params=_cp("parallel", "parallel"),
        name="hyena_conv_direct",
    )(hs, hs, hs, kf, bias, fwd, inv)


def _to_lanes(x):
    return jnp.swapaxes(x, -1, -2).reshape(*x.shape[:-2], x.shape[-1] * x.shape[-2])


def _two_stage_consts(seq):
    n = 2 * seq
    n1 = n // FFT_B
    a_in = n1 // 2
    k1 = np.arange(n1, dtype=np.float64)[:, None]
    a = np.arange(n1, dtype=np.float64)[None, :]
    ang1 = 2.0 * np.pi * k1 * a / n1
    f1_full = np.concatenate([np.cos(ang1), -np.sin(ang1)], axis=0)
    f1_inv = np.concatenate([np.cos(ang1.T), -np.sin(ang1.T)], axis=1)[:a_in] / n
    bb = np.arange(FFT_B, dtype=np.float64)[None, :]
    angt = 2.0 * np.pi * k1 * bb / n
    rep = LANES // FFT_B
    tw = np.stack([np.tile(np.cos(angt), (1, rep)), np.tile(-np.sin(angt), (1, rep))])
    b_ = np.arange(FFT_B, dtype=np.float64)[:, None]
    k2 = np.arange(FFT_B, dtype=np.float64)[None, :]
    ang2 = 2.0 * np.pi * b_ * k2 / FFT_B
    bd = lambda m: np.kron(np.eye(rep), m)
    cr, ci = np.cos(ang2), -np.sin(ang2)
    w2 = np.block([[bd(cr), bd(ci)], [bd(-ci), bd(cr)]])
    er, ei = np.cos(ang2), np.sin(ang2)
    w2i = np.block([[bd(er), bd(ei)], [bd(-ei), bd(er)]])
    return dict(
        f1_full=jnp.asarray(f1_full, F32), f1=jnp.asarray(f1_full[:, :a_in], BF16),
        f1_inv=jnp.asarray(f1_inv, BF16), tw=jnp.asarray(tw, F32),
        w2=jnp.asarray(w2, BF16), w2_f32=jnp.asarray(w2, F32), w2i=jnp.asarray(w2i, BF16))


def _fft_fwd(x, f1, tw_r, tw_i, w2, dot, cast):
    n1 = tw_r.shape[0]
    z = dot(f1, cast(x))
    zr, zi = z[:n1], z[n1:]
    rows = []
    for g in range(x.shape[1] // LANES):
        a = zr[:, g * LANES:(g + 1) * LANES]
        b = zi[:, g * LANES:(g + 1) * LANES]
        rows.append(jnp.concatenate([a * tw_r - b * tw_i, a * tw_i + b * tw_r], axis=1))
    return dot(cast(jnp.concatenate(rows, axis=0)), w2)


def _fft_inv(yf, w2i, tw_r, tw_i, f1_inv):
    n1 = tw_r.shape[0]
    gm = _dot(yf.astype(BF16), w2i)
    cols = []
    for g in range(yf.shape[0] // n1):
        blk = gm[g * n1:(g + 1) * n1]
        gr, gi = blk[:, :LANES], blk[:, LANES:]
        cols.append(jnp.concatenate([gr * tw_r + gi * tw_i, gi * tw_r - gr * tw_i], axis=0))
    return _dot(f1_inv, jnp.concatenate(cols, axis=1).astype(BF16))


def _ffilt_kernel(kc_ref, f1_ref, tw_ref, w2_ref, o_ref):
    x = _fft_fwd(kc_ref[...], f1_ref[...], tw_ref[0], tw_ref[1], w2_ref[...], _dot_hi, lambda v: v)
    o_ref[...] = x.reshape(o_ref.shape)


def _two_stage_filter_spectrum(kc_t, cs, groups_per_tile):
    n1, lanes = kc_t.shape
    wt = groups_per_tile * LANES
    ng = lanes // LANES
    return pl.pallas_call(
        _ffilt_kernel,
        grid=(lanes // wt,),
        in_specs=[pl.BlockSpec((n1, wt), lambda j: (0, j)), _full(cs["f1_full"].shape),
                  _full(cs["tw"].shape), _full(cs["w2_f32"].shape)],
        out_specs=pl.BlockSpec((groups_per_tile, n1, 2 * LANES), lambda j: (j, 0, 0)),
        out_shape=jax.ShapeDtypeStruct((ng, n1, 2 * LANES), F32),
        compiler_params=_cp("parallel"),
        name="filter_spectrum_two_stage",
    )(kc_t, cs["f1_full"], cs["tw"], cs["w2_f32"])


def _fconv_kernel(v_ref, x1_ref, x2_ref, kf_ref, bias_ref, f1_ref, tw_ref, w2_ref, w2i_ref, f1i_ref, o_ref):
    tw_r, tw_i = tw_ref[0], tw_ref[1]
    gates = (x1_ref, x2_ref)
    y = v_ref[...].astype(F32)
    for o in range(HY_ORDER):
        x = _fft_fwd(y, f1_ref[...], tw_r, tw_i, w2_ref[...], _dot, lambda v: v.astype(BF16))
        kf = kf_ref[o]
        kf = kf.reshape(kf.shape[0] * kf.shape[1], kf.shape[2])
        xr, xi = x[:, :LANES], x[:, LANES:]
        kr, ki = kf[:, :LANES], kf[:, LANES:]
        prod = jnp.concatenate([xr * kr - xi * ki, xr * ki + xi * kr], axis=1)
        conv = _fft_inv(prod, w2i_ref[...], tw_r, tw_i, f1i_ref[...])
        y = gates[o][...].astype(F32) * (conv + y * bias_ref[o:o + 1, :])
    o_ref[...] = y.astype(BF16)


def _hyena_two_stage(hs_t, kf, bias_t, cs, width):
    b, a_in, _ = hs_t.shape
    cg = min(width, 32)
    wt = cg * FFT_B
    gpt = wt // LANES
    nt = width // cg
    n1 = kf.shape[2]
    consts = [cs["f1"], cs["tw"], cs["w2"], cs["w2i"], cs["f1_inv"]]
    return pl.pallas_call(
        _fconv_kernel,
        grid=(nt, b),
        in_specs=[
            pl.BlockSpec((None, a_in, wt), lambda j, bi: (bi, 0, j)),
            pl.BlockSpec((None, a_in, wt), lambda j, bi: (bi, 0, nt + j)),
            pl.BlockSpec((None, a_in, wt), lambda j, bi: (bi, 0, 2 * nt + j)),
            pl.BlockSpec((HY_ORDER, gpt, n1, 2 * LANES), lambda j, bi: (0, j, 0, 0)),
            pl.BlockSpec((HY_ORDER, wt), lambda j, bi: (0, j)),
        ] + [_full(c.shape) for c in consts],
        out_specs=pl.BlockSpec((None, a_in, wt), lambda j, bi: (bi, 0, j)),
        out_shape=jax.ShapeDtypeStruct((b, a_in, width * FFT_B), BF16),
        compiler_params=_cp("parallel", "parallel"),
        name="hyena_conv_two_stage",
    )(hs_t, hs_t, hs_t, kf, bias_t, *consts)


def _hyena(hy_in, lp):
    b, l, c3 = hy_in.shape
    width = c3 // (HY_ORDER + 1)
    hs = _shortconv(hy_in, lp["hy_conv_w"], lp["hy_conv_b"])
    kc = _hyena_filter(l, lp["hy_w1"], lp["hy_b1"], lp["hy_w2"], lp["hy_b2"], lp["hy_w3"], lp["hy_decay"])
    n = 2 * l
    if l <= DIRECT_DFT_MAX_L:
        fwd_full, fwd, inv = _direct_consts(l)
        ct = _tile(width, 256)
        kf = _direct_filter_spectrum(kc, fwd_full)
        kf = kf.reshape(2 * n, HY_ORDER, width // ct, ct).transpose(0, 2, 1, 3).reshape(2 * n, HY_ORDER * width)
        return _hyena_direct(hs, kf, lp["hy_bias"], fwd, inv)
    cs = _two_stage_consts(l)
    n1 = n // FFT_B
    a_in = l // FFT_B
    ow = HY_ORDER * width
    kf = _two_stage_filter_spectrum(_to_lanes(kc.reshape(n1, FFT_B, ow)), cs, min(ow * FFT_B // LANES, 16))
    kf = kf.reshape(HY_ORDER, width * FFT_B // LANES, n1, 2 * LANES)
    hs_t = _to_lanes(hs.reshape(b, a_in, FFT_B, c3))
    bias_t = jnp.repeat(lp["hy_bias"], FFT_B, axis=1)
    hy_t = _hyena_two_stage(hs_t, kf, bias_t, cs, width)
    return jnp.swapaxes(hy_t.reshape(b, a_in, width, FFT_B), -1, -2).reshape(b, l, width)


def _merge_kernel(x_ref, o_ref, hy_ref, ga_ref, gb_ref, mod_ref, g2_ref, woa_ref, wob_ref, wout_ref,
                  wpq_ref, keys_ref, xn_out, h2_out, s_out):
    mod = mod_ref[...]
    merged = ga_ref[...] * _dot(o_ref[...], woa_ref[...]) + gb_ref[...] * _dot(hy_ref[...], wob_ref[...])
    xn = x_ref[...] + mod[2:3] * _dot(merged.astype(BF16), wout_ref[...])
    xn_out[...] = xn
    h2 = (_rms(xn, g2_ref[...]) * (1.0 + mod[4:5]) + mod[3:4]).astype(BF16)
    h2_out[...] = h2
    q = _dot(h2, wpq_ref[...]).astype(BF16)
    half = keys_ref.shape[2]
    for hp in range(keys_ref.shape[0]):
        s_out[hp] = _dot_nt(keys_ref[hp], q[:, hp * half:(hp + 1) * half])


def _merge(x, attn, hy, ga, gb, mod, lw):
    b, l, d = x.shape
    tm = min(l, 256)
    pb = 1 if mod.shape[0] > 1 else 0
    tok = lambda bi, i: (bi, i, 0)
    w_names = ["g2", "woa", "wob", "wout", "wpq", "keys"]
    nhp = lw["keys"].shape[0]
    return pl.pallas_call(
        _merge_kernel,
        grid=(b, l // tm),
        in_specs=[
            pl.BlockSpec((None, tm, d), tok),
            pl.BlockSpec((None, tm, attn.shape[2]), tok),
            pl.BlockSpec((None, tm, hy.shape[2]), tok),
            pl.BlockSpec((None, tm, d), tok),
            pl.BlockSpec((None, tm, d), tok),
            pl.BlockSpec((None, 6, d), lambda bi, i: (bi * pb, 0, 0)),
        ] + [_full(lw[n].shape) for n in w_names],
        out_specs=[
            pl.BlockSpec((None, tm, d), tok),
            pl.BlockSpec((None, tm, d), tok),
            pl.BlockSpec((None, nhp, N_KEYS, tm), lambda bi, i: (bi, 0, 0, i)),
        ],
        out_shape=[
            jax.ShapeDtypeStruct((b, l, d), F32),
            jax.ShapeDtypeStruct((b, l, d), BF16),
            jax.ShapeDtypeStruct((b, nhp, N_KEYS, l), F32),
        ],
        compiler_params=_cp("parallel", "parallel"),
        name="merge_peer_query",
    )(x, attn, hy, ga, gb, mod, *[lw[n] for n in w_names])


def _top_rows(s, k, payloads=()):
    r = s.shape[0]
    rid = lax.broadcasted_iota(jnp.int32, s.shape, 0).astype(F32)
    vals, idxs, picked = [], [], [[] for _ in payloads]
    for _ in range(k):
        m = jnp.max(s, axis=0, keepdims=True)
        ix = jnp.min(jnp.where(s == m, rid, float(r)), axis=0, keepdims=True)
        hit = rid == ix
        vals.append(m)
        idxs.append(ix)
        for p, acc in zip(payloads, picked):
            acc.append(jnp.sum(jnp.where(hit, p, 0.0), axis=0, keepdims=True))
        s = jnp.where(hit, -jnp.inf, s)
    cat = lambda rows: jnp.concatenate(rows, axis=0)
    return cat(vals), cat(idxs), [cat(a) for a in picked]


def _staircase(v1, r1, v2, r2):
    k = v1.shape[0]
    r1 = r1 * float(N_KEYS)
    cand, expert = [], []
    for a in range(k // 2):
        nb = k // (a + 1)
        rows = -(-nb // SUBLANES) * SUBLANES
        c = v1[a:a + 1] + v2[:rows]
        if nb < rows:
            c = jnp.where(lax.broadcasted_iota(jnp.int32, c.shape, 0) < nb, c, -jnp.inf)
        cand.append(c)
        expert.append(r1[a:a + 1] + r2[:rows])
    cand.append(v1[k // 2:] + v2[0:1])
    expert.append(r1[k // 2:] + r2[0:1])
    return jnp.concatenate(cand, axis=0), jnp.concatenate(expert, axis=0)


def _topk_kernel(s_ref, i1_out, i2_out, g_out, i1_sc, i2_sc, g_sc):
    def head(h, carry):
        v1, r1, _ = _top_rows(s_ref[2 * h], P_TOPK)
        v2, r2, _ = _top_rows(s_ref[2 * h + 1], P_TOPK)
        cand, expert = _staircase(v1, r1, v2, r2)
        top, _, (ex,) = _top_rows(cand, P_TOPK, (expert,))
        e = jnp.exp(top - top[0:1])
        k1 = jnp.floor(ex * (1.0 / N_KEYS))
        rows = pl.ds(pl.multiple_of(h * P_TOPK, P_TOPK), P_TOPK)
        i1_sc[rows, :] = k1
        i2_sc[rows, :] = ex - k1 * float(N_KEYS)
        g_sc[rows, :] = e / jnp.sum(e, axis=0, keepdims=True)
        return carry

    lax.fori_loop(0, P_HEADS, head, 0)
    i1_out[...] = i1_sc[...].T
    i2_out[...] = i2_sc[...].T
    g_out[...] = g_sc[...].T


def _peer_topk(s_t):
    b, nhp, nk, l = s_t.shape
    tt = min(l, 256)
    hk = P_HEADS * P_TOPK
    return pl.pallas_call(
        _topk_kernel,
        grid=(b, l // tt),
        in_specs=[pl.BlockSpec((None, nhp, nk, tt), lambda bi, i: (bi, 0, 0, i))],
        out_specs=[pl.BlockSpec((None, tt, hk), lambda bi, i: (bi, i, 0))] * 3,
        out_shape=[jax.ShapeDtypeStruct((b, l, hk), F32)] * 3,
        scratch_shapes=[pltpu.VMEM((hk, tt), F32)] * 3,
        compiler_params=_cp("parallel", "parallel"),
        name="peer_topk",
    )(s_t)


BF16_BITS = 0xFFFF0000
DENSE_SUB_TILE = 1024
DENSE_TOKEN_UNROLL = 64


def _dense_kernel(h_ref, i1_ref, i2_ref, g_ref, u_ref, v_ref, x_ref, mod_ref, o_ref, ws_ref, acc_ref, *, pitch):
    j = pl.program_id(2)
    tm = h_ref.shape[0]
    r = v_ref.shape[0] // N_KEYS
    half = N_KEYS // 2

    @pl.when(j == 0)
    def _():
        acc_ref[...] = jnp.zeros_like(acc_ref)
        sub = lax.broadcasted_iota(jnp.int32, (N_KEYS, i1_ref.shape[1]), 0).astype(F32).astype(BF16)
        one, zero = jnp.ones((), BF16), jnp.zeros((), BF16)

        def toks(tb, carry):
            for u in range(0, DENSE_TOKEN_UNROLL, 2):
                t = tb * DENSE_TOKEN_UNROLL + u
                pts, qts = [], []
                for k in range(2):
                    row = pl.ds(t + k, 1)
                    pts.append(jnp.where(i1_ref[row, :].astype(BF16) == sub, g_ref[row, :].astype(BF16), zero))
                    qts.append(jnp.where(i2_ref[row, :].astype(BF16) == sub, one, zero))
                z = jnp.zeros_like(qts[0])
                rhs = jnp.concatenate([jnp.concatenate([qts[0], z], axis=1),
                                       jnp.concatenate([z, qts[1]], axis=1)], axis=0)
                gates = _dot_nt(jnp.concatenate(pts, axis=1), rhs)
                for k in range(2):
                    gk = gates[:, k * N_KEYS:(k + 1) * N_KEYS].astype(jnp.bfloat16)
                    ws_ref[pl.ds(t + k, half, stride=pitch), :] = pltpu.bitcast(gk, jnp.uint32)
            return carry

        lax.fori_loop(0, tm // DENSE_TOKEN_UNROLL, toks, 0)

    rs = DENSE_SUB_TILE // N_KEYS
    for c in range(r // rs):
        rows = pl.ds(c * DENSE_SUB_TILE, DENSE_SUB_TILE)
        a = jax.nn.gelu(_dot_nt(h_ref[...], u_ref[rows, :]))
        cols = []
        for q in range(rs // 2):
            word_row = (j * (r // 2) + c * (rs // 2) + q) * pitch
            word = ws_ref[pl.ds(pl.multiple_of(word_row, SUBLANES), tm), :]
            cols.append(lax.bitcast_convert_type(word << 16, F32))
            cols.append(lax.bitcast_convert_type(word & jnp.uint32(BF16_BITS), F32))
        acc_ref[...] += _dot((a * jnp.concatenate(cols, axis=1)).astype(BF16), v_ref[rows, :])

    @pl.when(j == pl.num_programs(2) - 1)
    def _():
        o_ref[...] = x_ref[...] + mod_ref[5:6, :] * acc_ref[...]


def _peer_dense(h2, i1, i2, g, u_tab, v_tab, xn, mod):
    if mod.shape[0] == 1 and h2.shape[0] > 1:
        b, l, d = h2.shape
        flat = lambda a: a.reshape(1, b * l, a.shape[2])
        return _peer_dense(flat(h2), flat(i1), flat(i2), flat(g), u_tab, v_tab, flat(xn), mod).reshape(b, l, d)
    b, l, d = h2.shape
    tm = min(l, 512)
    te = 2 * DENSE_SUB_TILE
    ne = v_tab.shape[0]
    hk = i1.shape[2]
    pitch = tm + SUBLANES
    pb = 1 if mod.shape[0] > 1 else 0
    assert tm % DENSE_TOKEN_UNROLL == 0 and (DENSE_SUB_TILE // N_KEYS) % 2 == 0 and ne % te == 0
    tok = lambda bi, i, j: (bi, i, 0)
    return pl.pallas_call(
        functools.partial(_dense_kernel, pitch=pitch),
        grid=(b, l // tm, ne // te),
        in_specs=[
            pl.BlockSpec((None, tm, d), tok),
            pl.BlockSpec((None, tm, hk), tok),
            pl.BlockSpec((None, tm, hk), tok),
            pl.BlockSpec((None, tm, hk), tok),
            pl.BlockSpec((te, d), lambda bi, i, j: (j, 0)),
            pl.BlockSpec((te, d), lambda bi, i, j: (j, 0)),
            pl.BlockSpec((None, tm, d), tok, pipeline_mode=pl.Buffered(1)),
            pl.BlockSpec((None, 6, d), lambda bi, i, j: (bi * pb, 0, 0)),
        ],
        out_specs=pl.BlockSpec((None, tm, d), tok),
        out_shape=jax.ShapeDtypeStruct((b, l, d), F32),
        scratch_shapes=[pltpu.VMEM((N_KEYS // 2 * pitch, N_KEYS), jnp.uint32), pltpu.VMEM((tm, d), F32)],
        compiler_params=pltpu.CompilerParams(dimension_semantics=("parallel", "parallel", "arbitrary"),
                                             vmem_limit_bytes=DENSE_VMEM_LIMIT_BYTES),
        name="peer_dense",
    )(h2, i1, i2, g, u_tab, v_tab, xn, mod)


def _final_kernel(x_ref, g_ref, o_ref):
    o_ref[...] = _rms(x_ref[...], g_ref[...])


def _final_norm(x, g):
    b, l, d = x.shape
    tm = min(l, 512)
    return pl.pallas_call(
        _final_kernel,
        grid=(b, l // tm),
        in_specs=[pl.BlockSpec((None, tm, d), lambda bi, i: (bi, i, 0)), _full((1, d))],
        out_specs=pl.BlockSpec((None, tm, d), lambda bi, i: (bi, i, 0)),
        out_shape=jax.ShapeDtypeStruct((b, l, d), F32),
        compiler_params=_cp("parallel", "parallel"),
        name="final_norm",
    )(x, g.reshape(1, d))


def _rope_tables(n_tok):
    rows = n_tok // GRID_W
    row = jnp.repeat(jnp.arange(rows, dtype=F32), GRID_W)
    col = jnp.tile(jnp.arange(GRID_W, dtype=F32), rows)
    n_freq = QK_ROPE // 4
    inv = 1.0 / (ROPE_BASE ** (jnp.arange(n_freq, dtype=F32) / n_freq))
    ang = jnp.concatenate([row[:, None] * inv, col[:, None] * inv], axis=-1)
    cos, sin = jnp.cos(ang), jnp.sin(ang)
    half = QK_ROPE // 2
    pad = LANES - QK_NOPE - QK_ROPE
    ones = jnp.ones((n_tok, QK_NOPE), F32)
    z = lambda w: jnp.zeros((n_tok, w), F32)
    c = jnp.concatenate([ones, cos, cos, z(pad)], axis=1)
    s1 = jnp.concatenate([z(QK_NOPE), -sin, z(half), z(pad)], axis=1)
    s2 = jnp.concatenate([z(QK_NOPE), z(half), sin, z(pad)], axis=1)
    return c, s1, s2


def _pad_heads(w, per_head):
    rows = w.shape[0]
    w = w.reshape(rows, N_HEADS, per_head)
    return jnp.pad(w, ((0, 0), (0, 0), (0, LANES - per_head))).reshape(rows, N_HEADS * LANES)


def _layer_weights(p, l):
    d = p["w_in"].shape[1]
    q_rank = p["q_norm"].shape[1]
    kv_rank = p["kv_norm"].shape[1]
    hy3 = p["hy_conv_w"].shape[2]
    w_in = p["w_in"][l]
    c0, c1, c2, c3 = q_rank, q_rank + kv_rank, q_rank + kv_rank + QK_ROPE, q_rank + kv_rank + QK_ROPE + hy3
    wkr = jnp.zeros((d, LANES), F32).at[:, QK_NOPE:QK_NOPE + QK_ROPE].set(w_in[:, c1:c2])
    ukv = p["w_ukv"][l].reshape(kv_rank, N_HEADS, QK_NOPE + V_DIM)
    wuk = _pad_heads(ukv[:, :, :QK_NOPE].reshape(kv_rank, -1), QK_NOPE)
    wuv = _pad_heads(ukv[:, :, QK_NOPE:].reshape(kv_rank, -1), V_DIM)
    woa = p["w_oa"][l].reshape(N_HEADS, V_DIM, d)
    woa = jnp.pad(woa, ((0, 0), (0, LANES - V_DIM), (0, 0))).reshape(N_HEADS * LANES, d)
    keys = p["peer_keys"][l]
    row = lambda v: v.reshape(1, -1)
    return dict(
        gn=row(p["norm_mix"][l]), wq=w_in[:, :c0].astype(BF16), wckv=w_in[:, c0:c1].astype(BF16),
        wkr=wkr.astype(BF16), why=w_in[:, c2:c3].astype(BF16), wg=w_in[:, c3:].astype(BF16),
        qn=row(p["q_norm"][l]), kvn=row(p["kv_norm"][l]),
        wuq=_pad_heads(p["w_uq"][l], QK_NOPE + QK_ROPE).astype(BF16),
        wuk=wuk.astype(BF16), wuv=wuv.astype(BF16),
        g2=row(p["norm_ffn"][l]), woa=woa.astype(BF16), wob=p["w_ob"][l].astype(BF16),
        wout=p["w_out"][l].astype(BF16), wpq=p["peer_wq"][l].astype(BF16),
        keys=keys.reshape(keys.shape[0] * keys.shape[1], keys.shape[2], keys.shape[3]).astype(BF16),
        u=p["peer_u"][l].astype(BF16), v=p["peer_v"][l].astype(BF16),
    )


def _layer(x, mod, lw, lp, rope_tabs, ctx_kv):
    q, k, cb, ckv, kr, hy_in, ga, gb = _inproj(x, mod, lw, rope_tabs)
    if ctx_kv is None:
        attn = _attention(q, k, cb, lw["wuv"])
    else:
        attn = _attention(q, k, cb, lw["wuv"], *ctx_kv)
    hy = _hyena(hy_in, lp)
    xn, h2, s_t = _merge(x, attn, hy, ga, gb, mod, lw)
    i1, i2, g = _peer_topk(s_t)
    x = _peer_dense(h2, i1, i2, g, lw["u"], lw["v"], xn, mod)
    return x, ckv, kr[:, :, QK_NOPE:QK_NOPE + QK_ROPE]


def kernel(x_prompt, x_sample, cache_ckv, cache_krope, c, c_ctx, w_ada, b_ada, norm_mix, norm_ffn,
           w_in, q_norm, kv_norm, w_uq, w_ukv, w_oa, w_ob, w_out, hy_conv_w, hy_conv_b,
           hy_w1, hy_b1, hy_w2, hy_b2, hy_w3, hy_decay, hy_bias,
           peer_wq, peer_keys, peer_u, peer_v, final_norm):
    p = dict(norm_mix=norm_mix, norm_ffn=norm_ffn, w_in=w_in, q_norm=q_norm, kv_norm=kv_norm, w_uq=w_uq,
             w_ukv=w_ukv, w_oa=w_oa, w_ob=w_ob, w_out=w_out, peer_wq=peer_wq, peer_keys=peer_keys,
             peer_u=peer_u, peer_v=peer_v, hy_conv_w=hy_conv_w)
    depth, d = norm_mix.shape
    bd = x_sample.shape[0]
    rows = -(-(bd + 1) // SUBLANES) * SUBLANES
    cc = jnp.zeros((rows, d), F32).at[:bd].set(c).at[bd].set(c_ctx)
    mods = _adaln(cc, w_ada, b_ada).reshape(depth, rows, 6, d)
    rope_tabs = _rope_tables(x_sample.shape[1])
    e_mat = jnp.zeros((QK_ROPE, N_HEADS, LANES), F32)
    e_mat = e_mat.at[:, :, QK_NOPE:QK_NOPE + QK_ROPE].set(jnp.eye(QK_ROPE, dtype=F32)[:, None, :])
    e_mat = e_mat.reshape(QK_ROPE, N_HEADS * LANES).astype(BF16)
    xp, xs = x_prompt, x_sample
    new_ckv, new_kr = [], []
    for l in range(depth):
        lw = _layer_weights(p, l)
        lp = dict(hy_conv_w=hy_conv_w[l], hy_conv_b=hy_conv_b[l], hy_w1=hy_w1[l], hy_b1=hy_b1[l],
                  hy_w2=hy_w2[l], hy_b2=hy_b2[l], hy_w3=hy_w3[l], hy_decay=hy_decay[l], hy_bias=hy_bias[l])
        xp, ckv_p, kr_p = _layer(xp, mods[l, bd:bd + 1], lw, lp, None, None)
        new_ckv.append(ckv_p)
        new_kr.append(kr_p)
        ctx_kv = _kvcache(cache_ckv, cache_krope, l, lw["wuk"], e_mat)
        xs, _, _ = _layer(xs, mods[l, :bd], lw, lp, rope_tabs, ctx_kv)
    return (_final_norm(xp, final_norm), _final_norm(xs, final_norm),
            jnp.stack(new_ckv, axis=1), jnp.stack(new_kr, axis=1))
```

```python
import functools
import math

import numpy as np
import jax
import jax.numpy as jnp
from jax import lax
from jax.experimental import pallas as pl
from jax.experimental.pallas import tpu as pltpu

F32 = jnp.float32
BF16 = jnp.bfloat16

GRID_W = 64
N_HEADS = 8
QK_NOPE = 64
QK_ROPE = 32
V_DIM = 64
ROPE_BASE = 10000.0
HY_ORDER = 2
HY_BANDS = 16
SHORT_K = 3
P_HEADS = 8
N_KEYS = 128
P_TOPK = 16
EPS = 1e-6

LANES = 128
SUBLANES = 8
FFT_B = 64
DIRECT_DFT_MAX_L = 256
VMEM_LIMIT_BYTES = 48 * 1024 * 1024
DENSE_VMEM_LIMIT_BYTES = 56 * 1024 * 1024


def _cp(*sem):
    return pltpu.CompilerParams(dimension_semantics=sem, vmem_limit_bytes=VMEM_LIMIT_BYTES)


def _full(shape):
    n = len(shape)
    return pl.BlockSpec(shape, lambda *_: (0,) * n)


def _tile(n, pref, unit=LANES):
    if n <= pref:
        return n
    return max(t for t in range(unit, pref + 1, unit) if n % t == 0)


def _dot(a, b):
    return jnp.dot(a, b, preferred_element_type=F32)


def _dot_hi(a, b):
    return jnp.dot(a, b, precision=lax.Precision.HIGHEST, preferred_element_type=F32)


def _dot_nt(a, b):
    return lax.dot_general(a, b, (((1,), (1,)), ((), ())), preferred_element_type=F32)


def _rms(x, g):
    return x * lax.rsqrt(jnp.mean(x * x, axis=-1, keepdims=True) + EPS) * g


def _adaln_kernel(c_ref, w_ref, b_ref, o_ref):
    c = c_ref[...]
    s = (c * jax.nn.sigmoid(c)).astype(BF16)
    o_ref[...] = _dot(s, w_ref[...].astype(BF16)) + b_ref[...]


def _adaln(cc, w_ada, b_ada):
    depth, d, d6 = w_ada.shape
    rows = cc.shape[0]
    tn = d6 // 4
    return pl.pallas_call(
        _adaln_kernel,
        grid=(depth, d6 // tn),
        in_specs=[
            _full((rows, d)),
            pl.BlockSpec((None, d, tn), lambda l, j: (l, 0, j)),
            pl.BlockSpec((None, 1, tn), lambda l, j: (l, 0, j)),
        ],
        out_specs=pl.BlockSpec((None, rows, tn), lambda l, j: (l, 0, j)),
        out_shape=jax.ShapeDtypeStruct((depth, rows, d6), F32),
        compiler_params=_cp("parallel", "parallel"),
        name="adaln",
    )(cc, w_ada, b_ada.reshape(depth, 1, d6))


ATTN_SCALE = 1.0 / math.sqrt(QK_NOPE + QK_ROPE)
ATTN_KEY_CHUNK = 1024
ATTN_HEADS_PER_STEP = 4


def _rope_slab(x, c, s1, s2):
    half = QK_ROPE // 2
    return x * c + pltpu.roll(x, LANES - half, 1) * s1 + pltpu.roll(x, half, 1) * s2


def _inproj_kernel(*refs, rope):
    if rope:
        (x_ref, mod_ref, gn_ref, wq_ref, wckv_ref, wkr_ref, why_ref, wg_ref, qn_ref, kvn_ref,
         wuq_ref, wuk_ref, c_ref, s1_ref, s2_ref,
         q_out, k_out, cb_out, ckv_out, kr_out, hy_out, ga_out, gb_out) = refs
    else:
        (x_ref, mod_ref, gn_ref, wq_ref, wckv_ref, wkr_ref, why_ref, wg_ref, qn_ref, kvn_ref,
         wuq_ref, wuk_ref,
         q_out, k_out, cb_out, ckv_out, kr_out, hy_out, ga_out, gb_out) = refs
    x = x_ref[...]
    mod = mod_ref[...]
    d = x.shape[1]
    h = (_rms(x, gn_ref[...]) * (1.0 + mod[1:2]) + mod[0:1]).astype(BF16)
    q = _dot(_rms(_dot(h, wq_ref[...]), qn_ref[...]).astype(BF16), wuq_ref[...])
    ckv = _rms(_dot(h, wckv_ref[...]), kvn_ref[...])
    kr = _dot(h, wkr_ref[...])
    ckv_b = ckv.astype(BF16)
    kn = _dot(ckv_b, wuk_ref[...])
    if rope:
        c, s1, s2 = c_ref[...], s1_ref[...], s2_ref[...]
        kr_k = _rope_slab(kr, c, s1, s2)
        q = jnp.concatenate(
            [_rope_slab(q[:, i * LANES:(i + 1) * LANES], c, s1, s2) for i in range(N_HEADS)], axis=1)
    else:
        kr_k = kr
    q_out[...] = (q * ATTN_SCALE).astype(BF16)
    k_out[...] = (kn + jnp.concatenate([kr_k] * N_HEADS, axis=1)).astype(BF16)
    cb_out[...] = ckv_b
    ckv_out[...] = ckv
    kr_out[...] = kr
    hy_out[...] = _dot(h, why_ref[...])
    g = _dot(h, wg_ref[...])
    ga_out[...] = jax.nn.sigmoid(g[:, :d])
    gb_out[...] = jax.nn.sigmoid(g[:, d:])


def _inproj(x, mod, lw, rope_tabs):
    b, l, d = x.shape
    tm = min(l, 256)
    pb = 1 if mod.shape[0] > 1 else 0
    hl = N_HEADS * LANES
    kvr = lw["wckv"].shape[1]
    hyw = lw["why"].shape[1]
    rope = rope_tabs is not None
    tok = lambda bi, i: (bi, i, 0)
    w_names = ["gn", "wq", "wckv", "wkr", "why", "wg", "qn", "kvn", "wuq", "wuk"]
    ins = [x, mod] + [lw[n] for n in w_names]
    in_specs = [pl.BlockSpec((None, tm, d), tok), pl.BlockSpec((None, 6, d), lambda bi, i: (bi * pb, 0, 0))]
    in_specs += [_full(lw[n].shape) for n in w_names]
    if rope:
        ins += list(rope_tabs)
        in_specs += [pl.BlockSpec((tm, LANES), lambda bi, i: (i, 0))] * 3
    widths = [(hl, BF16), (hl, BF16), (kvr, BF16), (kvr, F32), (LANES, F32), (hyw, F32), (d, F32), (d, F32)]
    return pl.pallas_call(
        functools.partial(_inproj_kernel, rope=rope),
        grid=(b, l // tm),
        in_specs=in_specs,
        out_specs=[pl.BlockSpec((None, tm, w), tok) for w, _ in widths],
        out_shape=[jax.ShapeDtypeStruct((b, l, w), dt) for w, dt in widths],
        compiler_params=_cp("parallel", "parallel"),
        name="inproj_rope" if rope else "inproj",
    )(*ins)


def _kvcache_kernel(ckv_ref, kr_ref, wuk_ref, e_ref, k_out, cb_out):
    ckv_b = ckv_ref[...].astype(BF16)
    k_out[...] = (_dot(ckv_b, wuk_ref[...]) + _dot(kr_ref[...].astype(BF16), e_ref[...])).astype(BF16)
    cb_out[...] = ckv_b


def _kvcache(cache_ckv, cache_kr, layer, wuk, e_mat):
    b, _, p, kvr = cache_ckv.shape
    hl = N_HEADS * LANES
    return pl.pallas_call(
        _kvcache_kernel,
        grid=(b,),
        in_specs=[
            pl.BlockSpec((None, None, p, kvr), lambda bi: (bi, layer, 0, 0)),
            pl.BlockSpec((None, None, p, QK_ROPE), lambda bi: (bi, layer, 0, 0)),
            _full(wuk.shape),
            _full(e_mat.shape),
        ],
        out_specs=[pl.BlockSpec((None, p, hl), lambda bi: (bi, 0, 0)),
                   pl.BlockSpec((None, p, kvr), lambda bi: (bi, 0, 0))],
        out_shape=[jax.ShapeDtypeStruct((b, p, hl), BF16), jax.ShapeDtypeStruct((b, p, kvr), BF16)],
        compiler_params=_cp("parallel"),
        name="kvcache",
    )(cache_ckv, cache_kr, wuk, e_mat)


def _attn_kernel(*refs, ctx):
    if ctx:
        q_ref, k_ref, c_ref, wuv_ref, kc_ref, cc_ref, o_ref = refs
    else:
        q_ref, k_ref, c_ref, wuv_ref, o_ref = refs
    chunks = [(kc_ref, cc_ref, 0, kc_ref.shape[0])] if ctx else []
    n_keys = k_ref.shape[0]
    ck = min(n_keys, ATTN_KEY_CHUNK)
    chunks += [(k_ref, c_ref, c * ck, ck) for c in range(n_keys // ck)]
    heads = [slice(i * LANES, (i + 1) * LANES) for i in range(ATTN_HEADS_PER_STEP)]
    qs = [q_ref[:, h] for h in heads]
    m = [None] * len(heads)
    acc = [None] * len(heads)
    den = [None] * len(heads)
    for kr, cr, start, size in chunks:
        for i, h in enumerate(heads):
            s = _dot_nt(qs[i], kr[pl.ds(start, size), h])
            cmax = jnp.max(s, axis=1, keepdims=True)
            if m[i] is None:
                m[i] = cmax
                p = jnp.exp(s - cmax)
                den[i] = jnp.sum(p, axis=1, keepdims=True)
                acc[i] = _dot(p.astype(BF16), cr[pl.ds(start, size), :])
            else:
                m_new = jnp.maximum(m[i], cmax)
                alpha = jnp.exp(m[i] - m_new)
                p = jnp.exp(s - m_new)
                den[i] = den[i] * alpha + jnp.sum(p, axis=1, keepdims=True)
                acc[i] = acc[i] * alpha + _dot(p.astype(BF16), cr[pl.ds(start, size), :])
                m[i] = m_new
    for i, h in enumerate(heads):
        o_ref[:, h] = _dot((acc[i] / den[i]).astype(BF16), wuv_ref[:, h]).astype(BF16)


def _attention(q, k, cb, wuv, kc=None, cc=None):
    b, l, hl = q.shape
    kvr = cb.shape[2]
    tq = min(l, 512)
    ctx = kc is not None
    hw = ATTN_HEADS_PER_STEP * LANES
    ins = [q, k, cb, wuv]
    in_specs = [
        pl.BlockSpec((None, tq, hw), lambda bi, h, i: (bi, i, h)),
        pl.BlockSpec((None, l, hw), lambda bi, h, i: (bi, 0, h)),
        pl.BlockSpec((None, l, kvr), lambda bi, h, i: (bi, 0, 0)),
        pl.BlockSpec((kvr, hw), lambda bi, h, i: (0, h)),
    ]
    if ctx:
        p = kc.shape[1]
        ins += [kc, cc]
        in_specs += [pl.BlockSpec((None, p, hw), lambda bi, h, i: (bi, 0, h)),
                     pl.BlockSpec((None, p, kvr), lambda bi, h, i: (bi, 0, 0))]
    return pl.pallas_call(
        functools.partial(_attn_kernel, ctx=ctx),
        grid=(b, N_HEADS // ATTN_HEADS_PER_STEP, l // tq),
        in_specs=in_specs,
        out_specs=pl.BlockSpec((None, tq, hw), lambda bi, h, i: (bi, i, h)),
        out_shape=jax.ShapeDtypeStruct((b, l, hl), BF16),
        compiler_params=_cp("parallel", "parallel", "parallel"),
        name="attn_ctx" if ctx else "attn",
    )(*ins)


def _shortconv_kernel(x_ref, w_ref, b_ref, o_ref):
    x = x_ref[...]
    n = x.shape[0]
    row = lax.broadcasted_iota(jnp.int32, x.shape, 0)
    xm = jnp.where(row == 0, 0.0, pltpu.roll(x, 1, 0))
    xp = jnp.where(row == n - 1, 0.0, pltpu.roll(x, n - 1, 0))
    w = w_ref[...]
    o_ref[...] = (xm * w[0:1] + x * w[1:2] + xp * w[2:3] + b_ref[...]).astype(BF16)


def _shortconv(x, w, bias):
    b, l, c = x.shape
    ct = _tile(c, 256)
    return pl.pallas_call(
        _shortconv_kernel,
        grid=(b, c // ct),
        in_specs=[
            pl.BlockSpec((None, l, ct), lambda bi, j: (bi, 0, j)),
            pl.BlockSpec((SHORT_K, ct), lambda bi, j: (0, j)),
            pl.BlockSpec((1, ct), lambda bi, j: (0, j)),
        ],
        out_specs=pl.BlockSpec((None, l, ct), lambda bi, j: (bi, 0, j)),
        out_shape=jax.ShapeDtypeStruct((b, l, c), BF16),
        compiler_params=_cp("parallel", "parallel"),
        name="shortconv",
    )(x, w, bias.reshape(1, c))


def _filter_kernel(w1_ref, b1_ref, w2_ref, b2_ref, w3_ref, dec_ref, w3b_ref, o_ref, *, seq):
    tr = o_ref.shape[0]
    d = lax.broadcasted_iota(jnp.int32, (tr, LANES), 0) + pl.program_id(0) * tr
    t = jnp.where(d <= seq, d, 2 * seq - d).astype(F32) / seq
    lane = lax.broadcasted_iota(jnp.int32, (tr, LANES), 1)
    band = jnp.where(lane <= HY_BANDS, lane, lane - HY_BANDS).astype(F32)
    ang = (2.0 * math.pi * t) * band
    z = jnp.where(lane == 0, t,
                  jnp.where(lane <= HY_BANDS, jnp.sin(ang),
                            jnp.where(lane <= 2 * HY_BANDS, jnp.cos(ang), 0.0)))
    f = jnp.sin(_dot_hi(z, w1_ref[...]) + b1_ref[...])
    f = jnp.sin(_dot_hi(f, w2_ref[...]) + b2_ref[...])
    taps = _dot_hi(f, w3_ref[...]) * jnp.exp(-dec_ref[...] * t[:, :1])
    o_ref[...] = jnp.where(d[:, :1] == seq, 0.0, taps)

    @pl.when(pl.program_id(0) == 0)
    def _():
        first = slice(0, SUBLANES)
        lag0 = jnp.where(d[first, :1] == 0, _dot_hi(f[first], w3b_ref[...]), 0.0)
        o_ref[first, :] = taps[first] + lag0


def _hyena_filter(seq, w1, b1, w2, b2, w3, decay):
    order, _, width = decay.shape
    n = 2 * seq
    tr = min(seq, 512)
    per_dir = n // tr // 2
    hid = w2.shape[0]
    ow = order * width
    w1p = jnp.zeros((LANES, hid), F32).at[:w1.shape[0]].set(w1)
    w3d = w3.reshape(hid, order, 2, width).transpose(2, 0, 1, 3).reshape(2, hid, ow)
    decd = decay.transpose(1, 0, 2).reshape(2, 1, ow)
    small = [w1p, b1.reshape(1, hid), w2, b2.reshape(1, hid)]
    return pl.pallas_call(
        functools.partial(_filter_kernel, seq=seq),
        grid=(n // tr,),
        in_specs=[_full(a.shape) for a in small] + [
            pl.BlockSpec((None, hid, ow), lambda i: (i // per_dir, 0, 0)),
            pl.BlockSpec((None, 1, ow), lambda i: (i // per_dir, 0, 0)),
            pl.BlockSpec((None, hid, ow), lambda i: (1, 0, 0)),
        ],
        out_specs=pl.BlockSpec((tr, ow), lambda i: (i, 0)),
        out_shape=jax.ShapeDtypeStruct((n, ow), F32),
        compiler_params=_cp("parallel"),
        name="hyena_filter",
    )(*small, w3d, decd, w3d)


def _direct_consts(seq):
    n = 2 * seq
    k = np.arange(n, dtype=np.float64)[:, None]
    pos = np.arange(n, dtype=np.float64)[None, :]
    ang = 2.0 * np.pi * k * pos / n
    fwd = np.concatenate([np.cos(ang), -np.sin(ang)], axis=0)
    inv = np.concatenate([np.cos(ang.T), -np.sin(ang.T)], axis=1)[:seq] / n
    return (jnp.asarray(fwd, F32), jnp.asarray(fwd[:, :seq], BF16), jnp.asarray(inv, BF16))


def _dfilt_kernel(kc_ref, f_ref, o_ref):
    o_ref[...] = _dot_hi(f_ref[...], kc_ref[...])


def _direct_filter_spectrum(kc, fwd_full):
    n, ch = kc.shape
    ct = _tile(ch, 256)
    return pl.pallas_call(
        _dfilt_kernel,
        grid=(ch // ct,),
        in_specs=[pl.BlockSpec((n, ct), lambda j: (0, j)), _full(fwd_full.shape)],
        out_specs=pl.BlockSpec((2 * n, ct), lambda j: (0, j)),
        out_shape=jax.ShapeDtypeStruct((2 * n, ch), F32),
        compiler_params=_cp("parallel"),
        name="filter_spectrum_direct",
    )(kc, fwd_full)


def _dconv_kernel(v_ref, x1_ref, x2_ref, kf_ref, bias_ref, f_ref, fi_ref, o_ref):
    n = f_ref.shape[0] // 2
    ct = o_ref.shape[1]
    gates = (x1_ref, x2_ref)
    y = v_ref[...].astype(F32)
    for o in range(HY_ORDER):
        z = _dot(f_ref[...], y.astype(BF16))
        zr, zi = z[:n], z[n:]
        kr = kf_ref[:n, o * ct:(o + 1) * ct]
        ki = kf_ref[n:, o * ct:(o + 1) * ct]
        prod = jnp.concatenate([zr * kr - zi * ki, zr * ki + zi * kr], axis=0).astype(BF16)
        y = gates[o][...].astype(F32) * (_dot(fi_ref[...], prod) + y * bias_ref[o:o + 1, :])
    o_ref[...] = y.astype(BF16)


def _hyena_direct(hs, kf, bias, fwd, inv):
    b, l, c3 = hs.shape
    w = c3 // (HY_ORDER + 1)
    ct = _tile(w, 256)
    nt = w // ct
    return pl.pallas_call(
        _dconv_kernel,
        grid=(nt, b),
        in_specs=[
            pl.BlockSpec((None, l, ct), lambda j, bi: (bi, 0, j)),
            pl.BlockSpec((None, l, ct), lambda j, bi: (bi, 0, nt + j)),
            pl.BlockSpec((None, l, ct), lambda j, bi: (bi, 0, 2 * nt + j)),
            pl.BlockSpec((kf.shape[0], HY_ORDER * ct), lambda j, bi: (0, j)),
            pl.BlockSpec((HY_ORDER, ct), lambda j, bi: (0, j)),
            _full(fwd.shape),
            _full(inv.shape),
        ],
        out_specs=pl.BlockSpec((None, l, ct), lambda j, bi: (bi, 0, j)),
        out_shape=jax.ShapeDtypeStruct((b, l, w), BF16),
        compiler_params=_cp("parallel", "parallel"),
        name="hyena_conv_direct",
    )(hs, hs, hs, kf, bias, fwd, inv)


def _to_lanes(x):
    return jnp.swapaxes(x, -1, -2).reshape(*x.shape[:-2], x.shape[-1] * x.shape[-2])


def _two_stage_consts(seq):
    n = 2 * seq
    n1 = n // FFT_B
    a_in = n1 // 2
    k1 = np.arange(n1, dtype=np.float64)[:, None]
    a = np.arange(n1, dtype=np.float64)[None, :]
    ang1 = 2.0 * np.pi * k1 * a / n1
    f1_full = np.concatenate([np.cos(ang1), -np.sin(ang1)], axis=0)
    f1_inv = np.concatenate([np.cos(ang1.T), -np.sin(ang1.T)], axis=1)[:a_in] / n
    bb = np.arange(FFT_B, dtype=np.float64)[None, :]
    angt = 2.0 * np.pi * k1 * bb / n
    rep = LANES // FFT_B
    tw = np.stack([np.tile(np.cos(angt), (1, rep)), np.tile(-np.sin(angt), (1, rep))])
    b_ = np.arange(FFT_B, dtype=np.float64)[:, None]
    k2 = np.arange(FFT_B, dtype=np.float64)[None, :]
    ang2 = 2.0 * np.pi * b_ * k2 / FFT_B
    bd = lambda m: np.kron(np.eye(rep), m)
    cr, ci = np.cos(ang2), -np.sin(ang2)
    w2 = np.block([[bd(cr), bd(ci)], [bd(-ci), bd(cr)]])
    er, ei = np.cos(ang2), np.sin(ang2)
    w2i = np.block([[bd(er), bd(ei)], [bd(-ei), bd(er)]])
    return dict(
        f1_full=jnp.asarray(f1_full, F32), f1=jnp.asarray(f1_full[:, :a_in], BF16),
        f1_inv=jnp.asarray(f1_inv, BF16), tw=jnp.asarray(tw, F32),
        w2=jnp.asarray(w2, BF16), w2_f32=jnp.asarray(w2, F32), w2i=jnp.asarray(w2i, BF16))


def _fft_fwd(x, f1, tw_r, tw_i, w2, dot, cast):
    n1 = tw_r.shape[0]
    z = dot(f1, cast(x))
    zr, zi = z[:n1], z[n1:]
    rows = []
    for g in range(x.shape[1] // LANES):
        a = zr[:, g * LANES:(g + 1) * LANES]
        b = zi[:, g * LANES:(g + 1) * LANES]
        rows.append(jnp.concatenate([a * tw_r - b * tw_i, a * tw_i + b * tw_r], axis=1))
    return dot(cast(jnp.concatenate(rows, axis=0)), w2)


def _fft_inv(yf, w2i, tw_r, tw_i, f1_inv):
    n1 = tw_r.shape[0]
    gm = _dot(yf.astype(BF16), w2i)
    cols = []
    for g in range(yf.shape[0] // n1):
        blk = gm[g * n1:(g + 1) * n1]
        gr, gi = blk[:, :LANES], blk[:, LANES:]
        cols.append(jnp.concatenate([gr * tw_r + gi * tw_i, gi * tw_r - gr * tw_i], axis=0))
    return _dot(f1_inv, jnp.concatenate(cols, axis=1).astype(BF16))


def _ffilt_kernel(kc_ref, f1_ref, tw_ref, w2_ref, o_ref):
    x = _fft_fwd(kc_ref[...], f1_ref[...], tw_ref[0], tw_ref[1], w2_ref[...], _dot_hi, lambda v: v)
    o_ref[...] = x.reshape(o_ref.shape)


def _two_stage_filter_spectrum(kc_t, cs, groups_per_tile):
    n1, lanes = kc_t.shape
    wt = groups_per_tile * LANES
    ng = lanes // LANES
    return pl.pallas_call(
        _ffilt_kernel,
        grid=(lanes // wt,),
        in_specs=[pl.BlockSpec((n1, wt), lambda j: (0, j)), _full(cs["f1_full"].shape),
                  _full(cs["tw"].shape), _full(cs["w2_f32"].shape)],
        out_specs=pl.BlockSpec((groups_per_tile, n1, 2 * LANES), lambda j: (j, 0, 0)),
        out_shape=jax.ShapeDtypeStruct((ng, n1, 2 * LANES), F32),
        compiler_params=_cp("parallel"),
        name="filter_spectrum_two_stage",
    )(kc_t, cs["f1_full"], cs["tw"], cs["w2_f32"])


def _fconv_kernel(v_ref, x1_ref, x2_ref, kf_ref, bias_ref, f1_ref, tw_ref, w2_ref, w2i_ref, f1i_ref, o_ref):
    tw_r, tw_i = tw_ref[0], tw_ref[1]
    gates = (x1_ref, x2_ref)
    y = v_ref[...].astype(F32)
    for o in range(HY_ORDER):
        x = _fft_fwd(y, f1_ref[...], tw_r, tw_i, w2_ref[...], _dot, lambda v: v.astype(BF16))
        kf = kf_ref[o]
        kf = kf.reshape(kf.shape[0] * kf.shape[1], kf.shape[2])
        xr, xi = x[:, :LANES], x[:, LANES:]
        kr, ki = kf[:, :LANES], kf[:, LANES:]
        prod = jnp.concatenate([xr * kr - xi * ki, xr * ki + xi * kr], axis=1)
        conv = _fft_inv(prod, w2i_ref[...], tw_r, tw_i, f1i_ref[...])
        y = gates[o][...].astype(F32) * (conv + y * bias_ref[o:o + 1, :])
    o_ref[...] = y.astype(BF16)


def _hyena_two_stage(hs_t, kf, bias_t, cs, width):
    b, a_in, _ = hs_t.shape
    cg = min(width, 32)
    wt = cg * FFT_B
    gpt = wt // LANES
    nt = width // cg
    n1 = kf.shape[2]
    consts = [cs["f1"], cs["tw"], cs["w2"], cs["w2i"], cs["f1_inv"]]
    return pl.pallas_call(
        _fconv_kernel,
        grid=(nt, b),
        in_specs=[
            pl.BlockSpec((None, a_in, wt), lambda j, bi: (bi, 0, j)),
            pl.BlockSpec((None, a_in, wt), lambda j, bi: (bi, 0, nt + j)),
            pl.BlockSpec((None, a_in, wt), lambda j, bi: (bi, 0, 2 * nt + j)),
            pl.BlockSpec((HY_ORDER, gpt, n1, 2 * LANES), lambda j, bi: (0, j, 0, 0)),
            pl.BlockSpec((HY_ORDER, wt), lambda j, bi: (0, j)),
        ] + [_full(c.shape) for c in consts],
        out_specs=pl.BlockSpec((None, a_in, wt), lambda j, bi: (bi, 0, j)),
        out_shape=jax.ShapeDtypeStruct((b, a_in, width * FFT_B), BF16),
        compiler_params=_cp("parallel", "parallel"),
        name="hyena_conv_two_stage",
    )(hs_t, hs_t, hs_t, kf, bias_t, *consts)


def _hyena(hy_in, lp):
    b, l, c3 = hy_in.shape
    width = c3 // (HY_ORDER + 1)
    hs = _shortconv(hy_in, lp["hy_conv_w"], lp["hy_conv_b"])
    kc = _hyena_filter(l, lp["hy_w1"], lp["hy_b1"], lp["hy_w2"], lp["hy_b2"], lp["hy_w3"], lp["hy_decay"])
    n = 2 * l
    if l <= DIRECT_DFT_MAX_L:
        fwd_full, fwd, inv = _direct_consts(l)
        ct = _tile(width, 256)
        kf = _direct_filter_spectrum(kc, fwd_full)
        kf = kf.reshape(2 * n, HY_ORDER, width // ct, ct).transpose(0, 2, 1, 3).reshape(2 * n, HY_ORDER * width)
        return _hyena_direct(hs, kf, lp["hy_bias"], fwd, inv)
    cs = _two_stage_consts(l)
    n1 = n // FFT_B
    a_in = l // FFT_B
    ow = HY_ORDER * width
    kf = _two_stage_filter_spectrum(_to_lanes(kc.reshape(n1, FFT_B, ow)), cs, min(ow * FFT_B // LANES, 16))
    kf = kf.reshape(HY_ORDER, width * FFT_B // LANES, n1, 2 * LANES)
    hs_t = _to_lanes(hs.reshape(b, a_in, FFT_B, c3))
    bias_t = jnp.repeat(lp["hy_bias"], FFT_B, axis=1)
    hy_t = _hyena_two_stage(hs_t, kf, bias_t, cs, width)
    return jnp.swapaxes(hy_t.reshape(b, a_in, width, FFT_B), -1, -2).reshape(b, l, width)


def _merge_kernel(x_ref, o_ref, hy_ref, ga_ref, gb_ref, mod_ref, g2_ref, woa_ref, wob_ref, wout_ref,
                  wpq_ref, keys_ref, xn_out, h2_out, s_out):
    mod = mod_ref[...]
    merged = ga_ref[...] * _dot(o_ref[...], woa_ref[...]) + gb_ref[...] * _dot(hy_ref[...], wob_ref[...])
    xn = x_ref[...] + mod[2:3] * _dot(merged.astype(BF16), wout_ref[...])
    xn_out[...] = xn
    h2 = (_rms(xn, g2_ref[...]) * (1.0 + mod[4:5]) + mod[3:4]).astype(BF16)
    h2_out[...] = h2
    q = _dot(h2, wpq_ref[...]).astype(BF16)
    half = keys_ref.shape[2]
    for hp in range(keys_ref.shape[0]):
        s_out[hp] = _dot_nt(keys_ref[hp], q[:, hp * half:(hp + 1) * half])


def _merge(x, attn, hy, ga, gb, mod, lw):
    b, l, d = x.shape
    tm = min(l, 256)
    pb = 1 if mod.shape[0] > 1 else 0
    tok = lambda bi, i: (bi, i, 0)
    w_names = ["g2", "woa", "wob", "wout", "wpq", "keys"]
    nhp = lw["keys"].shape[0]
    return pl.pallas_call(
        _merge_kernel,
        grid=(b, l // tm),
        in_specs=[
            pl.BlockSpec((None, tm, d), tok),
            pl.BlockSpec((None, tm, attn.shape[2]), tok),
            pl.BlockSpec((None, tm, hy.shape[2]), tok),
            pl.BlockSpec((None, tm, d), tok),
            pl.BlockSpec((None, tm, d), tok),
            pl.BlockSpec((None, 6, d), lambda bi, i: (bi * pb, 0, 0)),
        ] + [_full(lw[n].shape) for n in w_names],
        out_specs=[
            pl.BlockSpec((None, tm, d), tok),
            pl.BlockSpec((None, tm, d), tok),
            pl.BlockSpec((None, nhp, N_KEYS, tm), lambda bi, i: (bi, 0, 0, i)),
        ],
        out_shape=[
            jax.ShapeDtypeStruct((b, l, d), F32),
            jax.ShapeDtypeStruct((b, l, d), BF16),
            jax.ShapeDtypeStruct((b, nhp, N_KEYS, l), F32),
        ],
        compiler_params=_cp("parallel", "parallel"),
        name="merge_peer_query",
    )(x, attn, hy, ga, gb, mod, *[lw[n] for n in w_names])


def _top_rows(s, k, payloads=()):
    r = s.shape[0]
    rid = lax.broadcasted_iota(jnp.int32, s.shape, 0).astype(F32)
    vals, idxs, picked = [], [], [[] for _ in payloads]
    for _ in range(k):
        m = jnp.max(s, axis=0, keepdims=True)
        ix = jnp.min(jnp.where(s == m, rid, float(r)), axis=0, keepdims=True)
        hit = rid == ix
        vals.append(m)
        idxs.append(ix)
        for p, acc in zip(payloads, picked):
            acc.append(jnp.sum(jnp.where(hit, p, 0.0), axis=0, keepdims=True))
        s = jnp.where(hit, -jnp.inf, s)
    cat = lambda rows: jnp.concatenate(rows, axis=0)
    return cat(vals), cat(idxs), [cat(a) for a in picked]


def _staircase(v1, r1, v2, r2):
    k = v1.shape[0]
    r1 = r1 * float(N_KEYS)
    cand, expert = [], []
    for a in range(k // 2):
        nb = k // (a + 1)
        rows = -(-nb // SUBLANES) * SUBLANES
        c = v1[a:a + 1] + v2[:rows]
        if nb < rows:
            c = jnp.where(lax.broadcasted_iota(jnp.int32, c.shape, 0) < nb, c, -jnp.inf)
        cand.append(c)
        expert.append(r1[a:a + 1] + r2[:rows])
    cand.append(v1[k // 2:] + v2[0:1])
    expert.append(r1[k // 2:] + r2[0:1])
    return jnp.concatenate(cand, axis=0), jnp.concatenate(expert, axis=0)


def _topk_kernel(s_ref, i1_out, i2_out, g_out, i1_sc, i2_sc, g_sc):
    def head(h, carry):
        v1, r1, _ = _top_rows(s_ref[2 * h], P_TOPK)
        v2, r2, _ = _top_rows(s_ref[2 * h + 1], P_TOPK)
        cand, expert = _staircase(v1, r1, v2, r2)
        top, _, (ex,) = _top_rows(cand, P_TOPK, (expert,))
        e = jnp.exp(top - top[0:1])
        k1 = jnp.floor(ex * (1.0 / N_KEYS))
        rows = pl.ds(pl.multiple_of(h * P_TOPK, P_TOPK), P_TOPK)
        i1_sc[rows, :] = k1
        i2_sc[rows, :] = ex - k1 * float(N_KEYS)
        g_sc[rows, :] = e / jnp.sum(e, axis=0, keepdims=True)
        return carry

    lax.fori_loop(0, P_HEADS, head, 0)
    i1_out[...] = i1_sc[...].T
    i2_out[...] = i2_sc[...].T
    g_out[...] = g_sc[...].T


def _peer_topk(s_t):
    b, nhp, nk, l = s_t.shape
    tt = min(l, 512)
    hk = P_HEADS * P_TOPK
    return pl.pallas_call(
        _topk_kernel,
        grid=(b, l // tt),
        in_specs=[pl.BlockSpec((None, nhp, nk, tt), lambda bi, i: (bi, 0, 0, i))],
        out_specs=[pl.BlockSpec((None, tt, hk), lambda bi, i: (bi, i, 0))] * 3,
        out_shape=[jax.ShapeDtypeStruct((b, l, hk), F32)] * 3,
        scratch_shapes=[pltpu.VMEM((hk, tt), F32)] * 3,
        compiler_params=_cp("parallel", "parallel"),
        name="peer_topk",
    )(s_t)


BF16_BITS = 0xFFFF0000
DENSE_SUB_TILE = 1024
DENSE_TOKEN_UNROLL = 128


def _dense_kernel(h_ref, i1_ref, i2_ref, g_ref, u_ref, v_ref, x_ref, mod_ref, o_ref, ws_ref, acc_ref, *, pitch):
    j = pl.program_id(2)
    tm = h_ref.shape[0]
    r = v_ref.shape[0] // N_KEYS
    half = N_KEYS // 2

    @pl.when(j == 0)
    def _():
        acc_ref[...] = jnp.zeros_like(acc_ref)
        sub = lax.broadcasted_iota(jnp.int32, (N_KEYS, i1_ref.shape[1]), 0).astype(F32).astype(BF16)
        one, zero = jnp.ones((), BF16), jnp.zeros((), BF16)

        def toks(tb, carry):
            for u in range(0, DENSE_TOKEN_UNROLL, 2):
                t = tb * DENSE_TOKEN_UNROLL + u
                pts, qts = [], []
                for k in range(2):
                    row = pl.ds(t + k, 1)
                    pts.append(jnp.where(i1_ref[row, :].astype(BF16) == sub, g_ref[row, :].astype(BF16), zero))
                    qts.append(jnp.where(i2_ref[row, :].astype(BF16) == sub, one, zero))
                z = jnp.zeros_like(qts[0])
                rhs = jnp.concatenate([jnp.concatenate([qts[0], z], axis=1),
                                       jnp.concatenate([z, qts[1]], axis=1)], axis=0)
                gates = _dot_nt(jnp.concatenate(pts, axis=1), rhs)
                for k in range(2):
                    gk = gates[:, k * N_KEYS:(k + 1) * N_KEYS].astype(jnp.bfloat16)
                    ws_ref[pl.ds(t + k, half, stride=pitch), :] = pltpu.bitcast(gk, jnp.uint32)
            return carry

        lax.fori_loop(0, tm // DENSE_TOKEN_UNROLL, toks, 0)

    rs = DENSE_SUB_TILE // N_KEYS
    for c in range(r // rs):
        rows = pl.ds(c * DENSE_SUB_TILE, DENSE_SUB_TILE)
        a = jax.nn.gelu(_dot_nt(h_ref[...], u_ref[rows, :]))
        cols = []
        for q in range(rs // 2):
            word_row = (j * (r // 2) + c * (rs // 2) + q) * pitch
            word = ws_ref[pl.ds(pl.multiple_of(word_row, SUBLANES), tm), :]
            cols.append(lax.bitcast_convert_type(word << 16, F32))
            cols.append(lax.bitcast_convert_type(word & jnp.uint32(BF16_BITS), F32))
        acc_ref[...] += _dot((a * jnp.concatenate(cols, axis=1)).astype(BF16), v_ref[rows, :])

    @pl.when(j == pl.num_programs(2) - 1)
    def _():
        o_ref[...] = x_ref[...] + mod_ref[5:6, :] * acc_ref[...]


def _peer_dense(h2, i1, i2, g, u_tab, v_tab, xn, mod):
    if mod.shape[0] == 1 and h2.shape[0] > 1:
        b, l, d = h2.shape
        flat = lambda a: a.reshape(1, b * l, a.shape[2])
        return _peer_dense(flat(h2), flat(i1), flat(i2), flat(g), u_tab, v_tab, flat(xn), mod).reshape(b, l, d)
    b, l, d = h2.shape
    tm = min(l, 512)
    te = 2 * DENSE_SUB_TILE
    ne = v_tab.shape[0]
    hk = i1.shape[2]
    pitch = tm + SUBLANES
    pb = 1 if mod.shape[0] > 1 else 0
    assert tm % DENSE_TOKEN_UNROLL == 0 and (DENSE_SUB_TILE // N_KEYS) % 2 == 0 and ne % te == 0
    tok = lambda bi, i, j: (bi, i, 0)
    return pl.pallas_call(
        functools.partial(_dense_kernel, pitch=pitch),
        grid=(b, l // tm, ne // te),
        in_specs=[
            pl.BlockSpec((None, tm, d), tok),
            pl.BlockSpec((None, tm, hk), tok),
            pl.BlockSpec((None, tm, hk), tok),
            pl.BlockSpec((None, tm, hk), tok),
            pl.BlockSpec((te, d), lambda bi, i, j: (j, 0)),
            pl.BlockSpec((te, d), lambda bi, i, j: (j, 0)),
            pl.BlockSpec((None, tm, d), tok, pipeline_mode=pl.Buffered(1)),
            pl.BlockSpec((None, 6, d), lambda bi, i, j: (bi * pb, 0, 0)),
        ],
        out_specs=pl.BlockSpec((None, tm, d), tok),
        out_shape=jax.ShapeDtypeStruct((b, l, d), F32),
        scratch_shapes=[pltpu.VMEM((N_KEYS // 2 * pitch, N_KEYS), jnp.uint32), pltpu.VMEM((tm, d), F32)],
        compiler_params=pltpu.CompilerParams(dimension_semantics=("parallel", "parallel", "arbitrary"),
                                             vmem_limit_bytes=DENSE_VMEM_LIMIT_BYTES),
        name="peer_dense",
    )(h2, i1, i2, g, u_tab, v_tab, xn, mod)


def _final_kernel(x_ref, g_ref, o_ref):
    o_ref[...] = _rms(x_ref[...], g_ref[...])


def _final_norm(x, g):
    b, l, d = x.shape
    tm = min(l, 512)
    return pl.pallas_call(
        _final_kernel,
        grid=(b, l // tm),
        in_specs=[pl.BlockSpec((None, tm, d), lambda bi, i: (bi, i, 0)), _full((1, d))],
        out_specs=pl.BlockSpec((None, tm, d), lambda bi, i: (bi, i, 0)),
        out_shape=jax.ShapeDtypeStruct((b, l, d), F32),
        compiler_params=_cp("parallel", "parallel"),
        name="final_norm",
    )(x, g.reshape(1, d))


def _rope_tables(n_tok):
    rows = n_tok // GRID_W
    row = jnp.repeat(jnp.arange(rows, dtype=F32), GRID_W)
    col = jnp.tile(jnp.arange(GRID_W, dtype=F32), rows)
    n_freq = QK_ROPE // 4
    inv = 1.0 / (ROPE_BASE ** (jnp.arange(n_freq, dtype=F32) / n_freq))
    ang = jnp.concatenate([row[:, None] * inv, col[:, None] * inv], axis=-1)
    cos, sin = jnp.cos(ang), jnp.sin(ang)
    half = QK_ROPE // 2
    pad = LANES - QK_NOPE - QK_ROPE
    ones = jnp.ones((n_tok, QK_NOPE), F32)
    z = lambda w: jnp.zeros((n_tok, w), F32)
    c = jnp.concatenate([ones, cos, cos, z(pad)], axis=1)
    s1 = jnp.concatenate([z(QK_NOPE), -sin, z(half), z(pad)], axis=1)
    s2 = jnp.concatenate([z(QK_NOPE), z(half), sin, z(pad)], axis=1)
    return c, s1, s2


def _pad_heads(w, per_head):
    rows = w.shape[0]
    w = w.reshape(rows, N_HEADS, per_head)
    return jnp.pad(w, ((0, 0), (0, 0), (0, LANES - per_head))).reshape(rows, N_HEADS * LANES)


def _layer_weights(p, l):
    d = p["w_in"].shape[1]
    q_rank = p["q_norm"].shape[1]
    kv_rank = p["kv_norm"].shape[1]
    hy3 = p["hy_conv_w"].shape[2]
    w_in = p["w_in"][l]
    c0, c1, c2, c3 = q_rank, q_rank + kv_rank, q_rank + kv_rank + QK_ROPE, q_rank + kv_rank + QK_ROPE + hy3
    wkr = jnp.zeros((d, LANES), F32).at[:, QK_NOPE:QK_NOPE + QK_ROPE].set(w_in[:, c1:c2])
    ukv = p["w_ukv"][l].reshape(kv_rank, N_HEADS, QK_NOPE + V_DIM)
    wuk = _pad_heads(ukv[:, :, :QK_NOPE].reshape(kv_rank, -1), QK_NOPE)
    wuv = _pad_heads(ukv[:, :, QK_NOPE:].reshape(kv_rank, -1), V_DIM)
    woa = p["w_oa"][l].reshape(N_HEADS, V_DIM, d)
    woa = jnp.pad(woa, ((0, 0), (0, LANES - V_DIM), (0, 0))).reshape(N_HEADS * LANES, d)
    keys = p["peer_keys"][l]
    row = lambda v: v.reshape(1, -1)
    return dict(
        gn=row(p["norm_mix"][l]), wq=w_in[:, :c0].astype(BF16), wckv=w_in[:, c0:c1].astype(BF16),
        wkr=wkr.astype(BF16), why=w_in[:, c2:c3].astype(BF16), wg=w_in[:, c3:].astype(BF16),
        qn=row(p["q_norm"][l]), kvn=row(p["kv_norm"][l]),
        wuq=_pad_heads(p["w_uq"][l], QK_NOPE + QK_ROPE).astype(BF16),
        wuk=wuk.astype(BF16), wuv=wuv.astype(BF16),
        g2=row(p["norm_ffn"][l]), woa=woa.astype(BF16), wob=p["w_ob"][l].astype(BF16),
        wout=p["w_out"][l].astype(BF16), wpq=p["peer_wq"][l].astype(BF16),
        keys=keys.reshape(keys.shape[0] * keys.shape[1], keys.shape[2], keys.shape[3]).astype(BF16),
        u=p["peer_u"][l].astype(BF16), v=p["peer_v"][l].astype(BF16),
    )


def _layer(x, mod, lw, lp, rope_tabs, ctx_kv):
    q, k, cb, ckv, kr, hy_in, ga, gb = _inproj(x, mod, lw, rope_tabs)
    if ctx_kv is None:
        attn = _attention(q, k, cb, lw["wuv"])
    else:
        attn = _attention(q, k, cb, lw["wuv"], *ctx_kv)
    hy = _hyena(hy_in, lp)
    xn, h2, s_t = _merge(x, attn, hy, ga, gb, mod, lw)
    i1, i2, g = _peer_topk(s_t)
    x = _peer_dense(h2, i1, i2, g, lw["u"], lw["v"], xn, mod)
    return x, ckv, kr[:, :, QK_NOPE:QK_NOPE + QK_ROPE]


def kernel(x_prompt, x_sample, cache_ckv, cache_krope, c, c_ctx, w_ada, b_ada, norm_mix, norm_ffn,
           w_in, q_norm, kv_norm, w_uq, w_ukv, w_oa, w_ob, w_out, hy_conv_w, hy_conv_b,
           hy_w1, hy_b1, hy_w2, hy_b2, hy_w3, hy_decay, hy_bias,
           peer_wq, peer_keys, peer_u, peer_v, final_norm):
    p = dict(norm_mix=norm_mix, norm_ffn=norm_ffn, w_in=w_in, q_norm=q_norm, kv_norm=kv_norm, w_uq=w_uq,
             w_ukv=w_ukv, w_oa=w_oa, w_ob=w_ob, w_out=w_out, peer_wq=peer_wq, peer_keys=peer_keys,
             peer_u=peer_u, peer_v=peer_v, hy_conv_w=hy_conv_w)
    depth, d = norm_mix.shape
    bd = x_sample.shape[0]
    rows = -(-(bd + 1) // SUBLANES) * SUBLANES
    cc = jnp.zeros((rows, d), F32).at[:bd].set(c).at[bd].set(c_ctx)
    mods = _adaln(cc, w_ada, b_ada).reshape(depth, rows, 6, d)
    rope_tabs = _rope_tables(x_sample.shape[1])
    e_mat = jnp.zeros((QK_ROPE, N_HEADS, LANES), F32)
    e_mat = e_mat.at[:, :, QK_NOPE:QK_NOPE + QK_ROPE].set(jnp.eye(QK_ROPE, dtype=F32)[:, None, :])
    e_mat = e_mat.reshape(QK_ROPE, N_HEADS * LANES).astype(BF16)
    xp, xs = x_prompt, x_sample
    new_ckv, new_kr = [], []
    for l in range(depth):
        lw = _layer_weights(p, l)
        lp = dict(hy_conv_w=hy_conv_w[l], hy_conv_b=hy_conv_b[l], hy_w1=hy_w1[l], hy_b1=hy_b1[l],
                  hy_w2=hy_w2[l], hy_b2=hy_b2[l], hy_w3=hy_w3[l], hy_decay=hy_decay[l], hy_bias=hy_bias[l])
        xp, ckv_p, kr_p = _layer(xp, mods[l, bd:bd + 1], lw, lp, None, None)
        new_ckv.append(ckv_p)
        new_kr.append(kr_p)
        ctx_kv = _kvcache(cache_ckv, cache_krope, l, lw["wuk"], e_mat)
        xs, _, _ = _layer(xs, mods[l, :bd], lw, lp, rope_tabs, ctx_kv)
    return (_final_norm(xp, final_norm), _final_norm(xs, final_norm),
            jnp.stack(new_ckv, axis=1), jnp.stack(new_kr, axis=1))
```

```python
import functools
import math

import numpy as np
import jax
import jax.numpy as jnp
from jax import lax
from jax.experimental import pallas as pl
from jax.experimental.pallas import tpu as pltpu

F32 = jnp.float32
BF16 = jnp.bfloat16

GRID_W = 64
N_HEADS = 8
QK_NOPE = 64
QK_ROPE = 32
V_DIM = 64
ROPE_BASE = 10000.0
HY_ORDER = 2
HY_BANDS = 16
SHORT_K = 3
P_HEADS = 8
N_KEYS = 128
P_TOPK = 16
EPS = 1e-6

LANES = 128
SUBLANES = 8
FFT_B = 64
DIRECT_DFT_MAX_L = 256
VMEM_LIMIT_BYTES = 48 * 1024 * 1024
DENSE_VMEM_LIMIT_BYTES = 56 * 1024 * 1024


def _cp(*sem):
    return pltpu.CompilerParams(dimension_semantics=sem, vmem_limit_bytes=VMEM_LIMIT_BYTES)


def _full(shape):
    n = len(shape)
    return pl.BlockSpec(shape, lambda *_: (0,) * n)


def _tile(n, pref, unit=LANES):
    if n <= pref:
        return n
    return max(t for t in range(unit, pref + 1, unit) if n % t == 0)


def _dot(a, b):
    return jnp.dot(a, b, preferred_element_type=F32)


def _dot_hi(a, b):
    return jnp.dot(a, b, precision=lax.Precision.HIGHEST, preferred_element_type=F32)


def _dot_nt(a, b):
    return lax.dot_general(a, b, (((1,), (1,)), ((), ())), preferred_element_type=F32)


def _rms(x, g):
    return x * lax.rsqrt(jnp.mean(x * x, axis=-1, keepdims=True) + EPS) * g


def _adaln_kernel(c_ref, w_ref, b_ref, o_ref):
    c = c_ref[...]
    s = (c * jax.nn.sigmoid(c)).astype(BF16)
    o_ref[...] = _dot(s, w_ref[...].astype(BF16)) + b_ref[...]


def _adaln(cc, w_ada, b_ada):
    depth, d, d6 = w_ada.shape
    rows = cc.shape[0]
    tn = d6 // 4
    return pl.pallas_call(
        _adaln_kernel,
        grid=(depth, d6 // tn),
        in_specs=[
            _full((rows, d)),
            pl.BlockSpec((None, d, tn), lambda l, j: (l, 0, j)),
            pl.BlockSpec((None, 1, tn), lambda l, j: (l, 0, j)),
        ],
        out_specs=pl.BlockSpec((None, rows, tn), lambda l, j: (l, 0, j)),
        out_shape=jax.ShapeDtypeStruct((depth, rows, d6), F32),
        compiler_params=_cp("parallel", "parallel"),
        name="adaln",
    )(cc, w_ada, b_ada.reshape(depth, 1, d6))


ATTN_SCALE = 1.0 / math.sqrt(QK_NOPE + QK_ROPE)
ATTN_KEY_CHUNK = 1024
ATTN_HEADS_PER_STEP = 4


def _rope_slab(x, c, s1, s2):
    half = QK_ROPE // 2
    return x * c + pltpu.roll(x, LANES - half, 1) * s1 + pltpu.roll(x, half, 1) * s2


def _inproj_kernel(*refs, rope):
    if rope:
        (x_ref, mod_ref, gn_ref, wq_ref, wckv_ref, wkr_ref, why_ref, wg_ref, qn_ref, kvn_ref,
         wuq_ref, wuk_ref, c_ref, s1_ref, s2_ref,
         q_out, k_out, cb_out, ckv_out, kr_out, hy_out, ga_out, gb_out) = refs
    else:
        (x_ref, mod_ref, gn_ref, wq_ref, wckv_ref, wkr_ref, why_ref, wg_ref, qn_ref, kvn_ref,
         wuq_ref, wuk_ref,
         q_out, k_out, cb_out, ckv_out, kr_out, hy_out, ga_out, gb_out) = refs
    x = x_ref[...]
    mod = mod_ref[...]
    d = x.shape[1]
    h = (_rms(x, gn_ref[...]) * (1.0 + mod[1:2]) + mod[0:1]).astype(BF16)
    q = _dot(_rms(_dot(h, wq_ref[...]), qn_ref[...]).astype(BF16), wuq_ref[...])
    ckv = _rms(_dot(h, wckv_ref[...]), kvn_ref[...])
    kr = _dot(h, wkr_ref[...])
    ckv_b = ckv.astype(BF16)
    kn = _dot(ckv_b, wuk_ref[...])
    if rope:
        c, s1, s2 = c_ref[...], s1_ref[...], s2_ref[...]
        kr_k = _rope_slab(kr, c, s1, s2)
        q = jnp.concatenate(
            [_rope_slab(q[:, i * LANES:(i + 1) * LANES], c, s1, s2) for i in range(N_HEADS)], axis=1)
    else:
        kr_k = kr
    q_out[...] = (q * ATTN_SCALE).astype(BF16)
    k_out[...] = (kn + jnp.concatenate([kr_k] * N_HEADS, axis=1)).astype(BF16)
    cb_out[...] = ckv_b
    ckv_out[...] = ckv
    kr_out[...] = kr
    hy_out[...] = _dot(h, why_ref[...])
    g = _dot(h, wg_ref[...])
    ga_out[...] = jax.nn.sigmoid(g[:, :d])
    gb_out[...] = jax.nn.sigmoid(g[:, d:])


def _inproj(x, mod, lw, rope_tabs):
    b, l, d = x.shape
    tm = min(l, 512)
    pb = 1 if mod.shape[0] > 1 else 0
    hl = N_HEADS * LANES
    kvr = lw["wckv"].shape[1]
    hyw = lw["why"].shape[1]
    rope = rope_tabs is not None
    tok = lambda bi, i: (bi, i, 0)
    w_names = ["gn", "wq", "wckv", "wkr", "why", "wg", "qn", "kvn", "wuq", "wuk"]
    ins = [x, mod] + [lw[n] for n in w_names]
    in_specs = [pl.BlockSpec((None, tm, d), tok), pl.BlockSpec((None, 6, d), lambda bi, i: (bi * pb, 0, 0))]
    in_specs += [_full(lw[n].shape) for n in w_names]
    if rope:
        ins += list(rope_tabs)
        in_specs += [pl.BlockSpec((tm, LANES), lambda bi, i: (i, 0))] * 3
    widths = [(hl, BF16), (hl, BF16), (kvr, BF16), (kvr, F32), (LANES, F32), (hyw, F32), (d, F32), (d, F32)]
    return pl.pallas_call(
        functools.partial(_inproj_kernel, rope=rope),
        grid=(b, l // tm),
        in_specs=in_specs,
        out_specs=[pl.BlockSpec((None, tm, w), tok) for w, _ in widths],
        out_shape=[jax.ShapeDtypeStruct((b, l, w), dt) for w, dt in widths],
        compiler_params=_cp("parallel", "parallel"),
        name="inproj_rope" if rope else "inproj",
    )(*ins)


def _kvcache_kernel(ckv_ref, kr_ref, wuk_ref, e_ref, k_out, cb_out):
    ckv_b = ckv_ref[...].astype(BF16)
    k_out[...] = (_dot(ckv_b, wuk_ref[...]) + _dot(kr_ref[...].astype(BF16), e_ref[...])).astype(BF16)
    cb_out[...] = ckv_b


def _kvcache(cache_ckv, cache_kr, layer, wuk, e_mat):
    b, _, p, kvr = cache_ckv.shape
    hl = N_HEADS * LANES
    return pl.pallas_call(
        _kvcache_kernel,
        grid=(b,),
        in_specs=[
            pl.BlockSpec((None, None, p, kvr), lambda bi: (bi, layer, 0, 0)),
            pl.BlockSpec((None, None, p, QK_ROPE), lambda bi: (bi, layer, 0, 0)),
            _full(wuk.shape),
            _full(e_mat.shape),
        ],
        out_specs=[pl.BlockSpec((None, p, hl), lambda bi: (bi, 0, 0)),
                   pl.BlockSpec((None, p, kvr), lambda bi: (bi, 0, 0))],
        out_shape=[jax.ShapeDtypeStruct((b, p, hl), BF16), jax.ShapeDtypeStruct((b, p, kvr), BF16)],
        compiler_params=_cp("parallel"),
        name="kvcache",
    )(cache_ckv, cache_kr, wuk, e_mat)


def _attn_kernel(*refs, ctx):
    if ctx:
        q_ref, k_ref, c_ref, wuv_ref, kc_ref, cc_ref, o_ref = refs
    else:
        q_ref, k_ref, c_ref, wuv_ref, o_ref = refs
    chunks = [(kc_ref, cc_ref, 0, kc_ref.shape[0])] if ctx else []
    n_keys = k_ref.shape[0]
    ck = min(n_keys, ATTN_KEY_CHUNK)
    chunks += [(k_ref, c_ref, c * ck, ck) for c in range(n_keys // ck)]
    heads = [slice(i * LANES, (i + 1) * LANES) for i in range(ATTN_HEADS_PER_STEP)]
    qs = [q_ref[:, h] for h in heads]
    m = [None] * len(heads)
    acc = [None] * len(heads)
    den = [None] * len(heads)
    for kr, cr, start, size in chunks:
        for i, h in enumerate(heads):
            s = _dot_nt(qs[i], kr[pl.ds(start, size), h])
            cmax = jnp.max(s, axis=1, keepdims=True)
            if m[i] is None:
                m[i] = cmax
                p = jnp.exp(s - cmax)
                den[i] = jnp.sum(p, axis=1, keepdims=True)
                acc[i] = _dot(p.astype(BF16), cr[pl.ds(start, size), :])
            else:
                m_new = jnp.maximum(m[i], cmax)
                alpha = jnp.exp(m[i] - m_new)
                p = jnp.exp(s - m_new)
                den[i] = den[i] * alpha + jnp.sum(p, axis=1, keepdims=True)
                acc[i] = acc[i] * alpha + _dot(p.astype(BF16), cr[pl.ds(start, size), :])
                m[i] = m_new
    for i, h in enumerate(heads):
        o_ref[:, h] = _dot((acc[i] / den[i]).astype(BF16), wuv_ref[:, h]).astype(BF16)


def _attention(q, k, cb, wuv, kc=None, cc=None):
    b, l, hl = q.shape
    kvr = cb.shape[2]
    tq = min(l, 512)
    ctx = kc is not None
    hw = ATTN_HEADS_PER_STEP * LANES
    ins = [q, k, cb, wuv]
    in_specs = [
        pl.BlockSpec((None, tq, hw), lambda bi, h, i: (bi, i, h)),
        pl.BlockSpec((None, l, hw), lambda bi, h, i: (bi, 0, h)),
        pl.BlockSpec((None, l, kvr), lambda bi, h, i: (bi, 0, 0)),
        pl.BlockSpec((kvr, hw), lambda bi, h, i: (0, h)),
    ]
    if ctx:
        p = kc.shape[1]
        ins += [kc, cc]
        in_specs += [pl.BlockSpec((None, p, hw), lambda bi, h, i: (bi, 0, h)),
                     pl.BlockSpec((None, p, kvr), lambda bi, h, i: (bi, 0, 0))]
    return pl.pallas_call(
        functools.partial(_attn_kernel, ctx=ctx),
        grid=(b, N_HEADS // ATTN_HEADS_PER_STEP, l // tq),
        in_specs=in_specs,
        out_specs=pl.BlockSpec((None, tq, hw), lambda bi, h, i: (bi, i, h)),
        out_shape=jax.ShapeDtypeStruct((b, l, hl), BF16),
        compiler_params=_cp("parallel", "parallel", "parallel"),
        name="attn_ctx" if ctx else "attn",
    )(*ins)


def _shortconv_kernel(x_ref, w_ref, b_ref, o_ref):
    x = x_ref[...]
    n = x.shape[0]
    row = lax.broadcasted_iota(jnp.int32, x.shape, 0)
    xm = jnp.where(row == 0, 0.0, pltpu.roll(x, 1, 0))
    xp = jnp.where(row == n - 1, 0.0, pltpu.roll(x, n - 1, 0))
    w = w_ref[...]
    o_ref[...] = (xm * w[0:1] + x * w[1:2] + xp * w[2:3] + b_ref[...]).astype(BF16)


def _shortconv(x, w, bias):
    b, l, c = x.shape
    ct = _tile(c, 256)
    return pl.pallas_call(
        _shortconv_kernel,
        grid=(b, c // ct),
        in_specs=[
            pl.BlockSpec((None, l, ct), lambda bi, j: (bi, 0, j)),
            pl.BlockSpec((SHORT_K, ct), lambda bi, j: (0, j)),
            pl.BlockSpec((1, ct), lambda bi, j: (0, j)),
        ],
        out_specs=pl.BlockSpec((None, l, ct), lambda bi, j: (bi, 0, j)),
        out_shape=jax.ShapeDtypeStruct((b, l, c), BF16),
        compiler_params=_cp("parallel", "parallel"),
        name="shortconv",
    )(x, w, bias.reshape(1, c))


def _filter_kernel(w1_ref, b1_ref, w2_ref, b2_ref, w3_ref, dec_ref, w3b_ref, o_ref, *, seq):
    tr = o_ref.shape[0]
    d = lax.broadcasted_iota(jnp.int32, (tr, LANES), 0) + pl.program_id(0) * tr
    t = jnp.where(d <= seq, d, 2 * seq - d).astype(F32) / seq
    lane = lax.broadcasted_iota(jnp.int32, (tr, LANES), 1)
    band = jnp.where(lane <= HY_BANDS, lane, lane - HY_BANDS).astype(F32)
    ang = (2.0 * math.pi * t) * band
    z = jnp.where(lane == 0, t,
                  jnp.where(lane <= HY_BANDS, jnp.sin(ang),
                            jnp.where(lane <= 2 * HY_BANDS, jnp.cos(ang), 0.0)))
    f = jnp.sin(_dot_hi(z, w1_ref[...]) + b1_ref[...])
    f = jnp.sin(_dot_hi(f, w2_ref[...]) + b2_ref[...])
    taps = _dot_hi(f, w3_ref[...]) * jnp.exp(-dec_ref[...] * t[:, :1])
    o_ref[...] = jnp.where(d[:, :1] == seq, 0.0, taps)

    @pl.when(pl.program_id(0) == 0)
    def _():
        first = slice(0, SUBLANES)
        lag0 = jnp.where(d[first, :1] == 0, _dot_hi(f[first], w3b_ref[...]), 0.0)
        o_ref[first, :] = taps[first] + lag0


def _hyena_filter(seq, w1, b1, w2, b2, w3, decay):
    order, _, width = decay.shape
    n = 2 * seq
    tr = min(seq, 512)
    per_dir = n // tr // 2
    hid = w2.shape[0]
    ow = order * width
    w1p = jnp.zeros((LANES, hid), F32).at[:w1.shape[0]].set(w1)
    w3d = w3.reshape(hid, order, 2, width).transpose(2, 0, 1, 3).reshape(2, hid, ow)
    decd = decay.transpose(1, 0, 2).reshape(2, 1, ow)
    small = [w1p, b1.reshape(1, hid), w2, b2.reshape(1, hid)]
    return pl.pallas_call(
        functools.partial(_filter_kernel, seq=seq),
        grid=(n // tr,),
        in_specs=[_full(a.shape) for a in small] + [
            pl.BlockSpec((None, hid, ow), lambda i: (i // per_dir, 0, 0)),
            pl.BlockSpec((None, 1, ow), lambda i: (i // per_dir, 0, 0)),
            pl.BlockSpec((None, hid, ow), lambda i: (1, 0, 0)),
        ],
        out_specs=pl.BlockSpec((tr, ow), lambda i: (i, 0)),
        out_shape=jax.ShapeDtypeStruct((n, ow), F32),
        compiler_params=_cp("parallel"),
        name="hyena_filter",
    )(*small, w3d, decd, w3d)


def _direct_consts(seq):
    n = 2 * seq
    k = np.arange(n, dtype=np.float64)[:, None]
    pos = np.arange(n, dtype=np.float64)[None, :]
    ang = 2.0 * np.pi * k * pos / n
    fwd = np.concatenate([np.cos(ang), -np.sin(ang)], axis=0)
    inv = np.concatenate([np.cos(ang.T), -np.sin(ang.T)], axis=1)[:seq] / n
    return (jnp.asarray(fwd, F32), jnp.asarray(fwd[:, :seq], BF16), jnp.asarray(inv, BF16))


def _dfilt_kernel(kc_ref, f_ref, o_ref):
    o_ref[...] = _dot_hi(f_ref[...], kc_ref[...])


def _direct_filter_spectrum(kc, fwd_full):
    n, ch = kc.shape
    ct = _tile(ch, 256)
    return pl.pallas_call(
        _dfilt_kernel,
        grid=(ch // ct,),
        in_specs=[pl.BlockSpec((n, ct), lambda j: (0, j)), _full(fwd_full.shape)],
        out_specs=pl.BlockSpec((2 * n, ct), lambda j: (0, j)),
        out_shape=jax.ShapeDtypeStruct((2 * n, ch), F32),
        compiler_params=_cp("parallel"),
        name="filter_spectrum_direct",
    )(kc, fwd_full)


def _dconv_kernel(v_ref, x1_ref, x2_ref, kf_ref, bias_ref, f_ref, fi_ref, o_ref):
    n = f_ref.shape[0] // 2
    ct = o_ref.shape[1]
    gates = (x1_ref, x2_ref)
    y = v_ref[...].astype(F32)
    for o in range(HY_ORDER):
        z = _dot(f_ref[...], y.astype(BF16))
        zr, zi = z[:n], z[n:]
        kr = kf_ref[:n, o * ct:(o + 1) * ct]
        ki = kf_ref[n:, o * ct:(o + 1) * ct]
        prod = jnp.concatenate([zr * kr - zi * ki, zr * ki + zi * kr], axis=0).astype(BF16)
        y = gates[o][...].astype(F32) * (_dot(fi_ref[...], prod) + y * bias_ref[o:o + 1, :])
    o_ref[...] = y.astype(BF16)


def _hyena_direct(hs, kf, bias, fwd, inv):
    b, l, c3 = hs.shape
    w = c3 // (HY_ORDER + 1)
    ct = _tile(w, 256)
    nt = w // ct
    return pl.pallas_call(
        _dconv_kernel,
        grid=(nt, b),
        in_specs=[
            pl.BlockSpec((None, l, ct), lambda j, bi: (bi, 0, j)),
            pl.BlockSpec((None, l, ct), lambda j, bi: (bi, 0, nt + j)),
            pl.BlockSpec((None, l, ct), lambda j, bi: (bi, 0, 2 * nt + j)),
            pl.BlockSpec((kf.shape[0], HY_ORDER * ct), lambda j, bi: (0, j)),
            pl.BlockSpec((HY_ORDER, ct), lambda j, bi: (0, j)),
            _full(fwd.shape),
            _full(inv.shape),
        ],
        out_specs=pl.BlockSpec((None, l, ct), lambda j, bi: (bi, 0, j)),
        out_shape=jax.ShapeDtypeStruct((b, l, w), BF16),
        compiler_params=_cp("parallel", "parallel"),
        name="hyena_conv_direct",
    )(hs, hs, hs, kf, bias, fwd, inv)


def _to_lanes(x):
    return jnp.swapaxes(x, -1, -2).reshape(*x.shape[:-2], x.shape[-1] * x.shape[-2])


def _two_stage_consts(seq):
    n = 2 * seq
    n1 = n // FFT_B
    a_in = n1 // 2
    k1 = np.arange(n1, dtype=np.float64)[:, None]
    a = np.arange(n1, dtype=np.float64)[None, :]
    ang1 = 2.0 * np.pi * k1 * a / n1
    f1_full = np.concatenate([np.cos(ang1), -np.sin(ang1)], axis=0)
    f1_inv = np.concatenate([np.cos(ang1.T), -np.sin(ang1.T)], axis=1)[:a_in] / n
    bb = np.arange(FFT_B, dtype=np.float64)[None, :]
    angt = 2.0 * np.pi * k1 * bb / n
    rep = LANES // FFT_B
    tw = np.stack([np.tile(np.cos(angt), (1, rep)), np.tile(-np.sin(angt), (1, rep))])
    b_ = np.arange(FFT_B, dtype=np.float64)[:, None]
    k2 = np.arange(FFT_B, dtype=np.float64)[None, :]
    ang2 = 2.0 * np.pi * b_ * k2 / FFT_B
    bd = lambda m: np.kron(np.eye(rep), m)
    cr, ci = np.cos(ang2), -np.sin(ang2)
    w2 = np.block([[bd(cr), bd(ci)], [bd(-ci), bd(cr)]])
    er, ei = np.cos(ang2), np.sin(ang2)
    w2i = np.block([[bd(er), bd(ei)], [bd(-ei), bd(er)]])
    return dict(
        f1_full=jnp.asarray(f1_full, F32), f1=jnp.asarray(f1_full[:, :a_in], BF16),
        f1_inv=jnp.asarray(f1_inv, BF16), tw=jnp.asarray(tw, F32),
        w2=jnp.asarray(w2, BF16), w2_f32=jnp.asarray(w2, F32), w2i=jnp.asarray(w2i, BF16))


def _fft_fwd(x, f1, tw_r, tw_i, w2, dot, cast):
    n1 = tw_r.shape[0]
    z = dot(f1, cast(x))
    zr, zi = z[:n1], z[n1:]
    rows = []
    for g in range(x.shape[1] // LANES):
        a = zr[:, g * LANES:(g + 1) * LANES]
        b = zi[:, g * LANES:(g + 1) * LANES]
        rows.append(jnp.concatenate([a * tw_r - b * tw_i, a * tw_i + b * tw_r], axis=1))
    return dot(cast(jnp.concatenate(rows, axis=0)), w2)


def _fft_inv(yf, w2i, tw_r, tw_i, f1_inv):
    n1 = tw_r.shape[0]
    gm = _dot(yf.astype(BF16), w2i)
    cols = []
    for g in range(yf.shape[0] // n1):
        blk = gm[g * n1:(g + 1) * n1]
        gr, gi = blk[:, :LANES], blk[:, LANES:]
        cols.append(jnp.concatenate([gr * tw_r + gi * tw_i, gi * tw_r - gr * tw_i], axis=0))
    return _dot(f1_inv, jnp.concatenate(cols, axis=1).astype(BF16))


def _ffilt_kernel(kc_ref, f1_ref, tw_ref, w2_ref, o_ref):
    x = _fft_fwd(kc_ref[...], f1_ref[...], tw_ref[0], tw_ref[1], w2_ref[...], _dot_hi, lambda v: v)
    o_ref[...] = x.reshape(o_ref.shape)


def _two_stage_filter_spectrum(kc_t, cs, groups_per_tile):
    n1, lanes = kc_t.shape
    wt = groups_per_tile * LANES
    ng = lanes // LANES
    return pl.pallas_call(
        _ffilt_kernel,
        grid=(lanes // wt,),
        in_specs=[pl.BlockSpec((n1, wt), lambda j: (0, j)), _full(cs["f1_full"].shape),
                  _full(cs["tw"].shape), _full(cs["w2_f32"].shape)],
        out_specs=pl.BlockSpec((groups_per_tile, n1, 2 * LANES), lambda j: (j, 0, 0)),
        out_shape=jax.ShapeDtypeStruct((ng, n1, 2 * LANES), F32),
        compiler_params=_cp("parallel"),
        name="filter_spectrum_two_stage",
    )(kc_t, cs["f1_full"], cs["tw"], cs["w2_f32"])


def _fconv_kernel(v_ref, x1_ref, x2_ref, kf_ref, bias_ref, f1_ref, tw_ref, w2_ref, w2i_ref, f1i_ref, o_ref):
    tw_r, tw_i = tw_ref[0], tw_ref[1]
    gates = (x1_ref, x2_ref)
    y = v_ref[...].astype(F32)
    for o in range(HY_ORDER):
        x = _fft_fwd(y, f1_ref[...], tw_r, tw_i, w2_ref[...], _dot, lambda v: v.astype(BF16))
        kf = kf_ref[o]
        kf = kf.reshape(kf.shape[0] * kf.shape[1], kf.shape[2])
        xr, xi = x[:, :LANES], x[:, LANES:]
        kr, ki = kf[:, :LANES], kf[:, LANES:]
        prod = jnp.concatenate([xr * kr - xi * ki, xr * ki + xi * kr], axis=1)
        conv = _fft_inv(prod, w2i_ref[...], tw_r, tw_i, f1i_ref[...])
        y = gates[o][...].astype(F32) * (conv + y * bias_ref[o:o + 1, :])
    o_ref[...] = y.astype(BF16)


def _hyena_two_stage(hs_t, kf, bias_t, cs, width):
    b, a_in, _ = hs_t.shape
    cg = min(width, 64)
    wt = cg * FFT_B
    gpt = wt // LANES
    nt = width // cg
    n1 = kf.shape[2]
    consts = [cs["f1"], cs["tw"], cs["w2"], cs["w2i"], cs["f1_inv"]]
    return pl.pallas_call(
        _fconv_kernel,
        grid=(nt, b),
        in_specs=[
            pl.BlockSpec((None, a_in, wt), lambda j, bi: (bi, 0, j)),
            pl.BlockSpec((None, a_in, wt), lambda j, bi: (bi, 0, nt + j)),
            pl.BlockSpec((None, a_in, wt), lambda j, bi: (bi, 0, 2 * nt + j)),
            pl.BlockSpec((HY_ORDER, gpt, n1, 2 * LANES), lambda j, bi: (0, j, 0, 0)),
            pl.BlockSpec((HY_ORDER, wt), lambda j, bi: (0, j)),
        ] + [_full(c.shape) for c in consts],
        out_specs=pl.BlockSpec((None, a_in, wt), lambda j, bi: (bi, 0, j)),
        out_shape=jax.ShapeDtypeStruct((b, a_in, width * FFT_B), BF16),
        compiler_params=_cp("parallel", "parallel"),
        name="hyena_conv_two_stage",
    )(hs_t, hs_t, hs_t, kf, bias_t, *consts)


def _hyena(hy_in, lp):
    b, l, c3 = hy_in.shape
    width = c3 // (HY_ORDER + 1)
    hs = _shortconv(hy_in, lp["hy_conv_w"], lp["hy_conv_b"])
    kc = _hyena_filter(l, lp["hy_w1"], lp["hy_b1"], lp["hy_w2"], lp["hy_b2"], lp["hy_w3"], lp["hy_decay"])
    n = 2 * l
    if l <= DIRECT_DFT_MAX_L:
        fwd_full, fwd, inv = _direct_consts(l)
        ct = _tile(width, 256)
        kf = _direct_filter_spectrum(kc, fwd_full)
        kf = kf.reshape(2 * n, HY_ORDER, width // ct, ct).transpose(0, 2, 1, 3).reshape(2 * n, HY_ORDER * width)
        return _hyena_direct(hs, kf, lp["hy_bias"], fwd, inv)
    cs = _two_stage_consts(l)
    n1 = n // FFT_B
    a_in = l // FFT_B
    ow = HY_ORDER * width
    kf = _two_stage_filter_spectrum(_to_lanes(kc.reshape(n1, FFT_B, ow)), cs, min(ow * FFT_B // LANES, 16))
    kf = kf.reshape(HY_ORDER, width * FFT_B // LANES, n1, 2 * LANES)
    hs_t = _to_lanes(hs.reshape(b, a_in, FFT_B, c3))
    bias_t = jnp.repeat(lp["hy_bias"], FFT_B, axis=1)
    hy_t = _hyena_two_stage(hs_t, kf, bias_t, cs, width)
    return jnp.swapaxes(hy_t.reshape(b, a_in, width, FFT_B), -1, -2).reshape(b, l, width)


def _merge_kernel(x_ref, o_ref, hy_ref, ga_ref, gb_ref, mod_ref, g2_ref, woa_ref, wob_ref, wout_ref,
                  wpq_ref, keys_ref, xn_out, h2_out, s_out):
    mod = mod_ref[...]
    merged = ga_ref[...] * _dot(o_ref[...], woa_ref[...]) + gb_ref[...] * _dot(hy_ref[...], wob_ref[...])
    xn = x_ref[...] + mod[2:3] * _dot(merged.astype(BF16), wout_ref[...])
    xn_out[...] = xn
    h2 = (_rms(xn, g2_ref[...]) * (1.0 + mod[4:5]) + mod[3:4]).astype(BF16)
    h2_out[...] = h2
    q = _dot(h2, wpq_ref[...]).astype(BF16)
    half = keys_ref.shape[2]
    for hp in range(keys_ref.shape[0]):
        s_out[hp] = _dot_nt(keys_ref[hp], q[:, hp * half:(hp + 1) * half])


def _merge(x, attn, hy, ga, gb, mod, lw):
    b, l, d = x.shape
    tm = min(l, 512)
    pb = 1 if mod.shape[0] > 1 else 0
    tok = lambda bi, i: (bi, i, 0)
    w_names = ["g2", "woa", "wob", "wout", "wpq", "keys"]
    nhp = lw["keys"].shape[0]
    return pl.pallas_call(
        _merge_kernel,
        grid=(b, l // tm),
        in_specs=[
            pl.BlockSpec((None, tm, d), tok),
            pl.BlockSpec((None, tm, attn.shape[2]), tok),
            pl.BlockSpec((None, tm, hy.shape[2]), tok),
            pl.BlockSpec((None, tm, d), tok),
            pl.BlockSpec((None, tm, d), tok),
            pl.BlockSpec((None, 6, d), lambda bi, i: (bi * pb, 0, 0)),
        ] + [_full(lw[n].shape) for n in w_names],
        out_specs=[
            pl.BlockSpec((None, tm, d), tok),
            pl.BlockSpec((None, tm, d), tok),
            pl.BlockSpec((None, nhp, N_KEYS, tm), lambda bi, i: (bi, 0, 0, i)),
        ],
        out_shape=[
            jax.ShapeDtypeStruct((b, l, d), F32),
            jax.ShapeDtypeStruct((b, l, d), BF16),
            jax.ShapeDtypeStruct((b, nhp, N_KEYS, l), F32),
        ],
        compiler_params=_cp("parallel", "parallel"),
        name="merge_peer_query",
    )(x, attn, hy, ga, gb, mod, *[lw[n] for n in w_names])


TOPK_HEADS_PER_TRIP = 1


def _top_rows(s, k, payloads=()):
    r = s.shape[0]
    rid = lax.broadcasted_iota(jnp.int32, s.shape, 0).astype(F32)
    vals, idxs, picked = [], [], [[] for _ in payloads]
    for _ in range(k):
        m = jnp.max(s, axis=0, keepdims=True)
        ix = jnp.min(jnp.where(s == m, rid, float(r)), axis=0, keepdims=True)
        hit = rid == ix
        vals.append(m)
        idxs.append(ix)
        for p, acc in zip(payloads, picked):
            acc.append(jnp.sum(jnp.where(hit, p, 0.0), axis=0, keepdims=True))
        s = jnp.where(hit, -jnp.inf, s)
    cat = lambda rows: jnp.concatenate(rows, axis=0)
    return cat(vals), cat(idxs), [cat(a) for a in picked]


def _staircase(v1, r1, v2, r2):
    k = v1.shape[0]
    r1 = r1 * float(N_KEYS)
    cand, expert = [], []
    for a in range(k // 2):
        nb = k // (a + 1)
        rows = -(-nb // SUBLANES) * SUBLANES
        c = v1[a:a + 1] + v2[:rows]
        if nb < rows:
            c = jnp.where(lax.broadcasted_iota(jnp.int32, c.shape, 0) < nb, c, -jnp.inf)
        cand.append(c)
        expert.append(r1[a:a + 1] + r2[:rows])
    cand.append(v1[k // 2:] + v2[0:1])
    expert.append(r1[k // 2:] + r2[0:1])
    return jnp.concatenate(cand, axis=0), jnp.concatenate(expert, axis=0)


def _topk_kernel(s_ref, i1_out, i2_out, g_out, i1_sc, i2_sc, g_sc):
    def one_head(h):
        v1, r1, _ = _top_rows(s_ref[2 * h], P_TOPK)
        v2, r2, _ = _top_rows(s_ref[2 * h + 1], P_TOPK)
        cand, expert = _staircase(v1, r1, v2, r2)
        top, _, (ex,) = _top_rows(cand, P_TOPK, (expert,))
        e = jnp.exp(top - top[0:1])
        k1 = jnp.floor(ex * (1.0 / N_KEYS))
        rows = pl.ds(pl.multiple_of(h * P_TOPK, P_TOPK), P_TOPK)
        i1_sc[rows, :] = k1
        i2_sc[rows, :] = ex - k1 * float(N_KEYS)
        g_sc[rows, :] = e / jnp.sum(e, axis=0, keepdims=True)

    def heads(hb, carry):
        for k in range(TOPK_HEADS_PER_TRIP):
            one_head(hb * TOPK_HEADS_PER_TRIP + k)
        return carry

    lax.fori_loop(0, P_HEADS // TOPK_HEADS_PER_TRIP, heads, 0)
    i1_out[...] = i1_sc[...].T
    i2_out[...] = i2_sc[...].T
    g_out[...] = g_sc[...].T


def _peer_topk(s_t):
    b, nhp, nk, l = s_t.shape
    tt = min(l, 512)
    hk = P_HEADS * P_TOPK
    return pl.pallas_call(
        _topk_kernel,
        grid=(b, l // tt),
        in_specs=[pl.BlockSpec((None, nhp, nk, tt), lambda bi, i: (bi, 0, 0, i))],
        out_specs=[pl.BlockSpec((None, tt, hk), lambda bi, i: (bi, i, 0))] * 3,
        out_shape=[jax.ShapeDtypeStruct((b, l, hk), F32)] * 3,
        scratch_shapes=[pltpu.VMEM((hk, tt), F32)] * 3,
        compiler_params=_cp("parallel", "parallel"),
        name="peer_topk",
    )(s_t)


BF16_BITS = 0xFFFF0000
DENSE_SUB_TILE = 1024
DENSE_TOKEN_UNROLL = 128


def _dense_kernel(h_ref, i1_ref, i2_ref, g_ref, u_ref, v_ref, x_ref, mod_ref, o_ref, ws_ref, acc_ref, *, pitch):
    j = pl.program_id(2)
    tm = h_ref.shape[0]
    r = v_ref.shape[0] // N_KEYS
    half = N_KEYS // 2

    @pl.when(j == 0)
    def _():
        acc_ref[...] = jnp.zeros_like(acc_ref)
        sub = lax.broadcasted_iota(jnp.int32, (N_KEYS, i1_ref.shape[1]), 0).astype(F32).astype(BF16)
        one, zero = jnp.ones((), BF16), jnp.zeros((), BF16)

        def toks(tb, carry):
            for u in range(0, DENSE_TOKEN_UNROLL, 2):
                t = tb * DENSE_TOKEN_UNROLL + u
                pts, qts = [], []
                for k in range(2):
                    row = pl.ds(t + k, 1)
                    pts.append(jnp.where(i1_ref[row, :].astype(BF16) == sub, g_ref[row, :].astype(BF16), zero))
                    qts.append(jnp.where(i2_ref[row, :].astype(BF16) == sub, one, zero))
                z = jnp.zeros_like(qts[0])
                rhs = jnp.concatenate([jnp.concatenate([qts[0], z], axis=1),
                                       jnp.concatenate([z, qts[1]], axis=1)], axis=0)
                gates = _dot_nt(jnp.concatenate(pts, axis=1), rhs)
                for k in range(2):
                    gk = gates[:, k * N_KEYS:(k + 1) * N_KEYS].astype(jnp.bfloat16)
                    ws_ref[pl.ds(t + k, half, stride=pitch), :] = pltpu.bitcast(gk, jnp.uint32)
            return carry

        lax.fori_loop(0, tm // DENSE_TOKEN_UNROLL, toks, 0)

    rs = DENSE_SUB_TILE // N_KEYS
    for c in range(r // rs):
        rows = pl.ds(c * DENSE_SUB_TILE, DENSE_SUB_TILE)
        a = jax.nn.gelu(_dot_nt(h_ref[...], u_ref[rows, :]))
        cols = []
        for q in range(rs // 2):
            word_row = (j * (r // 2) + c * (rs // 2) + q) * pitch
            word = ws_ref[pl.ds(pl.multiple_of(word_row, SUBLANES), tm), :]
            cols.append(lax.bitcast_convert_type(word << 16, F32))
            cols.append(lax.bitcast_convert_type(word & jnp.uint32(BF16_BITS), F32))
        acc_ref[...] += _dot((a * jnp.concatenate(cols, axis=1)).astype(BF16), v_ref[rows, :])

    @pl.when(j == pl.num_programs(2) - 1)
    def _():
        o_ref[...] = x_ref[...] + mod_ref[5:6, :] * acc_ref[...]


def _peer_dense(h2, i1, i2, g, u_tab, v_tab, xn, mod):
    if mod.shape[0] == 1 and h2.shape[0] > 1:
        b, l, d = h2.shape
        flat = lambda a: a.reshape(1, b * l, a.shape[2])
        return _peer_dense(flat(h2), flat(i1), flat(i2), flat(g), u_tab, v_tab, flat(xn), mod).reshape(b, l, d)
    b, l, d = h2.shape
    tm = min(l, 512)
    te = 2 * DENSE_SUB_TILE
    ne = v_tab.shape[0]
    hk = i1.shape[2]
    pitch = tm + SUBLANES
    pb = 1 if mod.shape[0] > 1 else 0
    assert tm % DENSE_TOKEN_UNROLL == 0 and (DENSE_SUB_TILE // N_KEYS) % 2 == 0 and ne % te == 0
    tok = lambda bi, i, j: (bi, i, 0)
    return pl.pallas_call(
        functools.partial(_dense_kernel, pitch=pitch),
        grid=(b, l // tm, ne // te),
        in_specs=[
            pl.BlockSpec((None, tm, d), tok),
            pl.BlockSpec((None, tm, hk), tok),
            pl.BlockSpec((None, tm, hk), tok),
            pl.BlockSpec((None, tm, hk), tok),
            pl.BlockSpec((te, d), lambda bi, i, j: (j, 0)),
            pl.BlockSpec((te, d), lambda bi, i, j: (j, 0)),
            pl.BlockSpec((None, tm, d), tok, pipeline_mode=pl.Buffered(1)),
            pl.BlockSpec((None, 6, d), lambda bi, i, j: (bi * pb, 0, 0)),
        ],
        out_specs=pl.BlockSpec((None, tm, d), tok),
        out_shape=jax.ShapeDtypeStruct((b, l, d), F32),
        scratch_shapes=[pltpu.VMEM((N_KEYS // 2 * pitch, N_KEYS), jnp.uint32), pltpu.VMEM((tm, d), F32)],
        compiler_params=pltpu.CompilerParams(dimension_semantics=("parallel", "parallel", "arbitrary"),
                                             vmem_limit_bytes=DENSE_VMEM_LIMIT_BYTES),
        name="peer_dense",
    )(h2, i1, i2, g, u_tab, v_tab, xn, mod)


def _final_kernel(x_ref, g_ref, o_ref):
    o_ref[...] = _rms(x_ref[...], g_ref[...])


def _final_norm(x, g):
    b, l, d = x.shape
    tm = min(l, 512)
    return pl.pallas_call(
        _final_kernel,
        grid=(b, l // tm),
        in_specs=[pl.BlockSpec((None, tm, d), lambda bi, i: (bi, i, 0)), _full((1, d))],
        out_specs=pl.BlockSpec((None, tm, d), lambda bi, i: (bi, i, 0)),
        out_shape=jax.ShapeDtypeStruct((b, l, d), F32),
        compiler_params=_cp("parallel", "parallel"),
        name="final_norm",
    )(x, g.reshape(1, d))


def _rope_tables(n_tok):
    rows = n_tok // GRID_W
    row = jnp.repeat(jnp.arange(rows, dtype=F32), GRID_W)
    col = jnp.tile(jnp.arange(GRID_W, dtype=F32), rows)
    n_freq = QK_ROPE // 4
    inv = 1.0 / (ROPE_BASE ** (jnp.arange(n_freq, dtype=F32) / n_freq))
    ang = jnp.concatenate([row[:, None] * inv, col[:, None] * inv], axis=-1)
    cos, sin = jnp.cos(ang), jnp.sin(ang)
    half = QK_ROPE // 2
    pad = LANES - QK_NOPE - QK_ROPE
    ones = jnp.ones((n_tok, QK_NOPE), F32)
    z = lambda w: jnp.zeros((n_tok, w), F32)
    c = jnp.concatenate([ones, cos, cos, z(pad)], axis=1)
    s1 = jnp.concatenate([z(QK_NOPE), -sin, z(half), z(pad)], axis=1)
    s2 = jnp.concatenate([z(QK_NOPE), z(half), sin, z(pad)], axis=1)
    return c, s1, s2


def _pad_heads(w, per_head):
    rows = w.shape[0]
    w = w.reshape(rows, N_HEADS, per_head)
    return jnp.pad(w, ((0, 0), (0, 0), (0, LANES - per_head))).reshape(rows, N_HEADS * LANES)


def _layer_weights(p, l):
    d = p["w_in"].shape[1]
    q_rank = p["q_norm"].shape[1]
    kv_rank = p["kv_norm"].shape[1]
    hy3 = p["hy_conv_w"].shape[2]
    w_in = p["w_in"][l]
    c0, c1, c2, c3 = q_rank, q_rank + kv_rank, q_rank + kv_rank + QK_ROPE, q_rank + kv_rank + QK_ROPE + hy3
    wkr = jnp.zeros((d, LANES), F32).at[:, QK_NOPE:QK_NOPE + QK_ROPE].set(w_in[:, c1:c2])
    ukv = p["w_ukv"][l].reshape(kv_rank, N_HEADS, QK_NOPE + V_DIM)
    wuk = _pad_heads(ukv[:, :, :QK_NOPE].reshape(kv_rank, -1), QK_NOPE)
    wuv = _pad_heads(ukv[:, :, QK_NOPE:].reshape(kv_rank, -1), V_DIM)
    woa = p["w_oa"][l].reshape(N_HEADS, V_DIM, d)
    woa = jnp.pad(woa, ((0, 0), (0, LANES - V_DIM), (0, 0))).reshape(N_HEADS * LANES, d)
    keys = p["peer_keys"][l]
    row = lambda v: v.reshape(1, -1)
    return dict(
        gn=row(p["norm_mix"][l]), wq=w_in[:, :c0].astype(BF16), wckv=w_in[:, c0:c1].astype(BF16),
        wkr=wkr.astype(BF16), why=w_in[:, c2:c3].astype(BF16), wg=w_in[:, c3:].astype(BF16),
        qn=row(p["q_norm"][l]), kvn=row(p["kv_norm"][l]),
        wuq=_pad_heads(p["w_uq"][l], QK_NOPE + QK_ROPE).astype(BF16),
        wuk=wuk.astype(BF16), wuv=wuv.astype(BF16),
        g2=row(p["norm_ffn"][l]), woa=woa.astype(BF16), wob=p["w_ob"][l].astype(BF16),
        wout=p["w_out"][l].astype(BF16), wpq=p["peer_wq"][l].astype(BF16),
        keys=keys.reshape(keys.shape[0] * keys.shape[1], keys.shape[2], keys.shape[3]).astype(BF16),
        u=p["peer_u"][l].astype(BF16), v=p["peer_v"][l].astype(BF16),
    )


def _layer(x, mod, lw, lp, rope_tabs, ctx_kv):
    q, k, cb, ckv, kr, hy_in, ga, gb = _inproj(x, mod, lw, rope_tabs)
    if ctx_kv is None:
        attn = _attention(q, k, cb, lw["wuv"])
    else:
        attn = _attention(q, k, cb, lw["wuv"], *ctx_kv)
    hy = _hyena(hy_in, lp)
    xn, h2, s_t = _merge(x, attn, hy, ga, gb, mod, lw)
    i1, i2, g = _peer_topk(s_t)
    x = _peer_dense(h2, i1, i2, g, lw["u"], lw["v"], xn, mod)
    return x, ckv, kr[:, :, QK_NOPE:QK_NOPE + QK_ROPE]


def kernel(x_prompt, x_sample, cache_ckv, cache_krope, c, c_ctx, w_ada, b_ada, norm_mix, norm_ffn,
           w_in, q_norm, kv_norm, w_uq, w_ukv, w_oa, w_ob, w_out, hy_conv_w, hy_conv_b,
           hy_w1, hy_b1, hy_w2, hy_b2, hy_w3, hy_decay, hy_bias,
           peer_wq, peer_keys, peer_u, peer_v, final_norm):
    p = dict(norm_mix=norm_mix, norm_ffn=norm_ffn, w_in=w_in, q_norm=q_norm, kv_norm=kv_norm, w_uq=w_uq,
             w_ukv=w_ukv, w_oa=w_oa, w_ob=w_ob, w_out=w_out, peer_wq=peer_wq, peer_keys=peer_keys,
             peer_u=peer_u, peer_v=peer_v, hy_conv_w=hy_conv_w)
    depth, d = norm_mix.shape
    bd = x_sample.shape[0]
    rows = -(-(bd + 1) // SUBLANES) * SUBLANES
    cc = jnp.zeros((rows, d), F32).at[:bd].set(c).at[bd].set(c_ctx)
    mods = _adaln(cc, w_ada, b_ada).reshape(depth, rows, 6, d)
    rope_tabs = _rope_tables(x_sample.shape[1])
    e_mat = jnp.zeros((QK_ROPE, N_HEADS, LANES), F32)
    e_mat = e_mat.at[:, :, QK_NOPE:QK_NOPE + QK_ROPE].set(jnp.eye(QK_ROPE, dtype=F32)[:, None, :])
    e_mat = e_mat.reshape(QK_ROPE, N_HEADS * LANES).astype(BF16)
    xp, xs = x_prompt, x_sample
    new_ckv, new_kr = [], []
    for l in range(depth):
        lw = _layer_weights(p, l)
        lp = dict(hy_conv_w=hy_conv_w[l], hy_conv_b=hy_conv_b[l], hy_w1=hy_w1[l], hy_b1=hy_b1[l],
                  hy_w2=hy_w2[l], hy_b2=hy_b2[l], hy_w3=hy_w3[l], hy_decay=hy_decay[l], hy_bias=hy_bias[l])
        xp, ckv_p, kr_p = _layer(xp, mods[l, bd:bd + 1], lw, lp, None, None)
        new_ckv.append(ckv_p)
        new_kr.append(kr_p)
        ctx_kv = _kvcache(cache_ckv, cache_krope, l, lw["wuk"], e_mat)
        xs, _, _ = _layer(xs, mods[l, :bd], lw, lp, rope_tabs, ctx_kv)
    return (_final_norm(xp, final_norm), _final_norm(xs, final_norm),
            jnp.stack(new_ckv, axis=1), jnp.stack(new_kr, axis=1))
```
